```python
import jax, jax.numpy as jnp
from jax import lax
import numpy as np

D_MODEL = 2048
BATCH = 4
SEQ = 2048
DEPTH = 4

HEAD_DIM = 128
GRID_W = 64
QBLK = 128
A_QBLK = 64
EPS = 1e-6
N_HEADS_TOTAL = D_MODEL // HEAD_DIM
A_HEADS = N_HEADS_TOTAL // 2
A_PATTERNS = ((128, 1), (512, 4), (2048, 16))
B_HEADS = N_HEADS_TOTAL - A_HEADS
NA_ROWS_MAX = 8
NA_COLS = 16
C_HEADS = N_HEADS_TOTAL // 2
C_KV_HEADS = C_HEADS // 4
C_WINDOW = 128
D_HEADS = N_HEADS_TOTAL - C_HEADS
D_KV_HEADS = D_HEADS // 4
ROPE_THETA = 10000.0

A_W = A_HEADS * HEAD_DIM
B_W = B_HEADS * HEAD_DIM
C_W = C_HEADS * HEAD_DIM
C_KV_W = C_KV_HEADS * HEAD_DIM
D_W = D_HEADS * HEAD_DIM
D_KV_W = D_KV_HEADS * HEAD_DIM
EVEN_SPLITS = (A_W, A_W, A_W, A_W, B_W, B_W, B_W, B_W)
ODD_SPLITS = (C_W, C_KV_W, C_KV_W, C_W, D_W, D_KV_W, D_KV_W, D_W)
EVEN_IN = sum(EVEN_SPLITS)
ODD_IN = sum(ODD_SPLITS)
EVEN_MIX = A_W + B_W
ODD_MIX = C_W + D_W

kernel_name = 'hybrid_dilated_neighbourhood_sink_axial_encoder'


def rms_norm(x, g):
    xf = x.astype(jnp.float32)
    y = xf * lax.rsqrt(jnp.mean(xf * xf, axis=-1, keepdims=True) + EPS)
    return (y * g.astype(jnp.float32)).astype(x.dtype)


def alibi_slopes(n):
    return 2.0 ** (-8.0 * jnp.arange(1, n + 1, dtype=jnp.float32) / n)


def _split(t, sizes):
    return jnp.split(t, np.cumsum(sizes)[:-1].tolist(), axis=-1)


def dilated_attention(q, k, v):
    b, s, h, hd = q.shape
    nb = s // A_QBLK
    slopes = alibi_slopes(h)
    scale = hd ** -0.5
    qb = q.reshape(b, nb, A_QBLK, h, hd).transpose(1, 0, 2, 3, 4)
    starts = jnp.arange(nb) * A_QBLK

    def one_block(args):
        q_blk, t0 = args
        t = t0 + jnp.arange(A_QBLK)
        outs, lses = [], []
        for window, dil in A_PATTERNS:
            n_side = (window // 2) // dil
            off = dil * jnp.arange(-n_side, n_side + 1)
            pos = t[:, None] + off[None, :]
            valid = (pos >= 0) & (pos < s)
            idx = jnp.clip(pos, 0, s - 1)
            k_g = k[:, idx]
            v_g = v[:, idx]
            logits = jnp.einsum('bqhd,bqkhd->bhqk', q_blk, k_g).astype(jnp.float32) * scale
            logits = logits - slopes[:, None, None] * jnp.abs(off).astype(jnp.float32)
            logits = jnp.where(valid[None, None], logits, -jnp.inf)
            m = jnp.max(logits, axis=-1, keepdims=True)
            p = jnp.exp(logits - m)
            den = jnp.sum(p, axis=-1, keepdims=True)
            o = jnp.einsum('bhqk,bqkhd->bhqd', p.astype(v.dtype), v_g).astype(jnp.float32) / den
            outs.append(o)
            lses.append(m + jnp.log(den))
        w = jax.nn.softmax(jnp.stack(lses), axis=0)
        o = jnp.sum(w * jnp.stack(outs), axis=0)
        return o.transpose(0, 2, 1, 3).astype(q.dtype)

    o = lax.map(one_block, (qb, starts))
    return o.transpose(1, 0, 2, 3, 4).reshape(b, s, h, hd)


def neighbourhood_attention(q, k, v, rpb):
    b, s, h, hd = q.shape
    rows = s // GRID_W
    kh = min(NA_ROWS_MAX, rows)
    kw = NA_COLS
    scale = hd ** -0.5
    qg = q.reshape(b, rows, GRID_W, h, hd).transpose(1, 0, 2, 3, 4)
    kg = k.reshape(b, rows, GRID_W, h, hd)
    vg = v.reshape(b, rows, GRID_W, h, hd)
    r = jnp.arange(rows)
    row_start = jnp.clip(r - kh // 2, 0, rows - kh)
    c = jnp.arange(GRID_W)
    col_idx = jnp.clip(c - kw // 2, 0, GRID_W - kw)[:, None] + jnp.arange(kw)[None, :]
    dcol = col_idx - c[:, None] + (NA_COLS - 1)

    def one_row(args):
        q_row, r0, rs = args
        k_rows = lax.dynamic_slice_in_dim(kg, rs, kh, axis=1)
        v_rows = lax.dynamic_slice_in_dim(vg, rs, kh, axis=1)
        k_n = k_rows[:, :, col_idx]
        v_n = v_rows[:, :, col_idx]
        drow = rs + jnp.arange(kh) - r0 + (NA_ROWS_MAX - 1)
        bias = rpb[:, drow[None, :, None], dcol[:, None, :]]
        logits = jnp.einsum('bchd,brckhd->bhcrk', q_row, k_n).astype(jnp.float32) * scale
        logits = logits + bias.astype(jnp.float32)[None]
        p = jax.nn.softmax(logits.reshape(b, h, GRID_W, kh * kw), axis=-1)
        p = p.reshape(b, h, GRID_W, kh, kw)
        o = jnp.einsum('bhcrk,brckhd->bchd', p.astype(v.dtype), v_n)
        return o.astype(q.dtype)

    o = lax.map(one_row, (qg, r, row_start))
    return o.transpose(1, 0, 2, 3, 4).reshape(b, s, h, hd)


def windowed_sink_attention(q, k, v, sinks):
    b, s, hq, hd = q.shape
    hkv = k.shape[2]
    g = hq // hkv
    nb = s // QBLK
    scale = hd ** -0.5
    slopes = alibi_slopes(hq).reshape(hkv, g)
    qb = q.reshape(b, nb, QBLK, hkv, g, hd)
    pad = ((0, 0), (QBLK, QBLK), (0, 0), (0, 0))
    kp = jnp.pad(k, pad).reshape(b, nb + 2, QBLK, hkv, hd)
    vp = jnp.pad(v, pad).reshape(b, nb + 2, QBLK, hkv, hd)
    kb = jnp.concatenate([kp[:, :-2], kp[:, 1:-1], kp[:, 2:]], axis=2)
    vb = jnp.concatenate([vp[:, :-2], vp[:, 1:-1], vp[:, 2:]], axis=2)
    qpos = jnp.arange(s).reshape(nb, QBLK)
    kpos = (jnp.arange(nb)[:, None] - 1) * QBLK + jnp.arange(3 * QBLK)[None, :]
    dist = jnp.abs(kpos[:, None, :] - qpos[:, :, None])
    valid = (dist <= C_WINDOW) & (kpos[:, None, :] >= 0) & (kpos[:, None, :] < s)
    logits = jnp.einsum('bnqhgd,bnshd->bhgnqs', qb, kb).astype(jnp.float32) * scale
    logits = logits - slopes[:, :, None, None, None] * dist.astype(jnp.float32)
    logits = jnp.where(valid, logits, -jnp.inf)
    sink = jnp.broadcast_to(sinks.astype(jnp.float32).reshape(hkv, g)[None, :, :, None, None, None],
                            logits.shape[:-1] + (1,))
    p = jax.nn.softmax(jnp.concatenate([logits, sink], axis=-1), axis=-1)[..., :-1]
    o = jnp.einsum('bhgnqs,bnshd->bnqhgd', p.astype(v.dtype), vb)
    return o.reshape(b, s, hq, hd).astype(q.dtype)


def _rope_1d(x, pos):
    hn = x.shape[-1] // 2
    inv = ROPE_THETA ** (-jnp.arange(hn, dtype=jnp.float32) / hn)
    ang = pos.astype(jnp.float32)[:, None] * inv[None, :]
    cos = jnp.cos(ang)[None, :, None, :]
    sin = jnp.sin(ang)[None, :, None, :]
    x1, x2 = x[..., :hn], x[..., hn:]
    return jnp.concatenate([x1 * cos - x2 * sin, x1 * sin + x2 * cos], axis=-1)


def axial_rope(x):
    s = x.shape[1]
    t = jnp.arange(s)
    half = x.shape[-1] // 2
    xf = x.astype(jnp.float32)
    y = jnp.concatenate([_rope_1d(xf[..., :half], t // GRID_W),
                         _rope_1d(xf[..., half:], t % GRID_W)], axis=-1)
    return y.astype(x.dtype)


def axial_rope_attention(q, k, v, gq, gk):
    b, s, hq, hd = q.shape
    hkv = k.shape[2]
    g = hq // hkv
    nb = s // QBLK
    scale = hd ** -0.5
    q = axial_rope(rms_norm(q, gq))
    k = axial_rope(rms_norm(k, gk))
    qb = q.reshape(b, nb, QBLK, hkv, g, hd).transpose(1, 0, 2, 3, 4, 5)

    def one_block(q_blk):
        logits = jnp.einsum('bqhgd,bshd->bhgqs', q_blk, k).astype(jnp.float32) * scale
        p = jax.nn.softmax(logits, axis=-1)
        return jnp.einsum('bhgqs,bshd->bqhgd', p.astype(v.dtype), v).astype(q.dtype)

    o = lax.map(one_block, qb)
    return o.transpose(1, 0, 2, 3, 4, 5).reshape(b, s, hq, hd)


def even_layer(y, w_in, w_out, rpb):
    b, s, _ = y.shape
    qa, ka, va, ga, qb, kb, vb, gb = _split(y @ w_in, EVEN_SPLITS)
    heads = lambda t, n: t.reshape(b, s, n, HEAD_DIM)
    oa = dilated_attention(heads(qa, A_HEADS), heads(ka, A_HEADS), heads(va, A_HEADS))
    ob = neighbourhood_attention(heads(qb, B_HEADS), heads(kb, B_HEADS), heads(vb, B_HEADS), rpb)
    mix = jnp.concatenate([oa.reshape(b, s, A_W) * jax.nn.silu(ga),
                           ob.reshape(b, s, B_W) * jax.nn.silu(gb)], axis=-1)
    return mix @ w_out


def odd_layer(y, w_in, w_out, sinks, gq, gk):
    b, s, _ = y.shape
    qc, kc, vc, gc, qd, kd, vd, gd = _split(y @ w_in, ODD_SPLITS)
    heads = lambda t, n: t.reshape(b, s, n, HEAD_DIM)
    oc = windowed_sink_attention(heads(qc, C_HEADS), heads(kc, C_KV_HEADS), heads(vc, C_KV_HEADS), sinks)
    od = axial_rope_attention(heads(qd, D_HEADS), heads(kd, D_KV_HEADS), heads(vd, D_KV_HEADS), gq, gk)
    mix = jnp.concatenate([oc.reshape(b, s, C_W) * jax.nn.silu(gc),
                           od.reshape(b, s, D_W) * jax.nn.silu(gd)], axis=-1)
    return mix @ w_out


def setup_inputs(seed: int = 0) -> dict:
    key = jax.random.key(seed)
    ks = jax.random.split(key, 12)
    n_even = (DEPTH + 1) // 2
    n_odd = DEPTH // 2
    nrm = jax.random.normal
    f32 = jnp.float32
    return {
        'x': nrm(ks[0], (BATCH, SEQ, D_MODEL), f32),
        'norm_g': 1.0 + 0.02 * nrm(ks[1], (DEPTH, D_MODEL), f32),
        'final_g': 1.0 + 0.02 * nrm(ks[2], (D_MODEL,), f32),
        'w_in_even': nrm(ks[3], (n_even, D_MODEL, EVEN_IN), f32) * D_MODEL ** -0.5,
        'w_out_even': nrm(ks[4], (n_even, EVEN_MIX, D_MODEL), f32) * EVEN_MIX ** -0.5,
        'rpb_b': 0.1 * nrm(ks[5], (n_even, B_HEADS, 2 * NA_ROWS_MAX - 1, 2 * NA_COLS - 1), f32),
        'w_in_odd': nrm(ks[6], (n_odd, D_MODEL, ODD_IN), f32) * D_MODEL ** -0.5,
        'w_out_odd': nrm(ks[7], (n_odd, ODD_MIX, D_MODEL), f32) * ODD_MIX ** -0.5,
        'sinks_c': 0.5 * nrm(ks[8], (n_odd, C_HEADS), f32),
        'qnorm_d': 1.0 + 0.02 * nrm(ks[9], (n_odd, HEAD_DIM), f32),
        'knorm_d': 1.0 + 0.02 * nrm(ks[10], (n_odd, HEAD_DIM), f32),
    }


def reference(x, norm_g, final_g, w_in_even, w_out_even, rpb_b, w_in_odd, w_out_odd,
              sinks_c, qnorm_d, knorm_d):
    h = x
    for layer in range(DEPTH):
        y = rms_norm(h, norm_g[layer])
        i = layer // 2
        if layer % 2 == 0:
            h = h + even_layer(y, w_in_even[i], w_out_even[i], rpb_b[i])
        else:
            h = h + odd_layer(y, w_in_odd[i], w_out_odd[i], sinks_c[i], qnorm_d[i], knorm_d[i])
    return rms_norm(h, final_g)
```

```python
import functools

import jax
import jax.numpy as jnp
import numpy as np
from jax import lax
from jax.experimental import pallas as pl
from jax.experimental.pallas import tpu as pltpu

D_MODEL = 2048
DEPTH = 4
HEAD_DIM = 128
GRID_W = 64
EPS = 1e-6
N_HEADS = 8
KV_HEADS = 2
GROUP = N_HEADS // KV_HEADS
MIX_W = N_HEADS * HEAD_DIM
KV_W = KV_HEADS * HEAD_DIM
A_PATTERNS = ((128, 1), (512, 4), (2048, 16))
A_SIDE = 64
NA_ROWS = 8
NA_COLS = 16
NA_ROWS_MAX = 8
C_WINDOW = 128
QBLK = 128
ROPE_THETA = 10000.0
SCALE = HEAD_DIM ** -0.5
MASKED = -1e30
FAR = 1e9

VMEM_LIMIT = 56 * 1024 * 1024


def _params(*sem):
    return pltpu.CompilerParams(dimension_semantics=sem, vmem_limit_bytes=VMEM_LIMIT)


def _silu(x):
    return x * (1.0 / (1.0 + jnp.exp(-x)))


def _dot_nt(a, b):
    return lax.dot_general(a, b, (((1,), (1,)), ((), ())), preferred_element_type=jnp.float32)


def _norm_kernel(h_ref, g_ref, y_ref):
    x = h_ref[...]
    y = x * lax.rsqrt(jnp.mean(x * x, axis=-1, keepdims=True) + EPS)
    y_ref[...] = (y * g_ref[...]).astype(y_ref.dtype)


def _rms_norm_rows(h, g, out_dtype, tm=512):
    m, d = h.shape
    return pl.pallas_call(
        _norm_kernel,
        grid=(m // tm,),
        in_specs=[pl.BlockSpec((tm, d), lambda i: (i, 0)), pl.BlockSpec((1, d), lambda i: (0, 0))],
        out_specs=pl.BlockSpec((tm, d), lambda i: (i, 0)),
        out_shape=jax.ShapeDtypeStruct((m, d), out_dtype),
        compiler_params=_params("parallel"),
        name="rms_norm",
    )(h, g.reshape(1, d))


def _matmul_kernel(x_ref, w_ref, o_ref):
    o_ref[...] = jnp.dot(x_ref[...], w_ref[...], preferred_element_type=jnp.float32).astype(o_ref.dtype)


def _in_proj(y, w, tm=512, tn=1024):
    m, k = y.shape
    n = w.shape[1]
    return pl.pallas_call(
        _matmul_kernel,
        grid=(n // tn, m // tm),
        in_specs=[pl.BlockSpec((tm, k), lambda j, i: (i, 0)), pl.BlockSpec((k, tn), lambda j, i: (0, j))],
        out_specs=pl.BlockSpec((tm, tn), lambda j, i: (i, j)),
        out_shape=jax.ShapeDtypeStruct((m, n), jnp.bfloat16),
        compiler_params=_params("parallel", "parallel"),
        name="in_proj",
    )(y, w)


def _out_proj_kernel(m1_ref, m2_ref, w1_ref, w2_ref, h_ref, g_ref, *out_refs, final):
    acc = jnp.dot(m1_ref[...], w1_ref[...], preferred_element_type=jnp.float32)
    acc = acc + jnp.dot(m2_ref[...], w2_ref[...], preferred_element_type=jnp.float32)
    h = h_ref[...] + acc
    y = h * lax.rsqrt(jnp.mean(h * h, axis=-1, keepdims=True) + EPS) * g_ref[...]
    if final:
        out_refs[0][...] = y
    else:
        out_refs[0][...] = h
        out_refs[1][...] = y.astype(out_refs[1].dtype)


def _out_proj(mix1, mix2, w, h, g_next, final, tm=256):
    m, d = h.shape
    half = mix1.shape[1]
    row = lambda i: (i, 0)
    if final:
        out_shape = jax.ShapeDtypeStruct((m, d), jnp.float32)
        out_specs = pl.BlockSpec((tm, d), row)
    else:
        out_shape = (jax.ShapeDtypeStruct((m, d), jnp.float32), jax.ShapeDtypeStruct((m, d), jnp.bfloat16))
        out_specs = (pl.BlockSpec((tm, d), row), pl.BlockSpec((tm, d), row))
    return pl.pallas_call(
        functools.partial(_out_proj_kernel, final=final),
        grid=(m // tm,),
        in_specs=[pl.BlockSpec((tm, half), row), pl.BlockSpec((tm, half), row),
                  pl.BlockSpec((half, d), lambda i: (0, 0)), pl.BlockSpec((half, d), lambda i: (1, 0)),
                  pl.BlockSpec((tm, d), row), pl.BlockSpec((1, d), lambda i: (0, 0))],
        out_specs=out_specs,
        out_shape=out_shape,
        compiler_params=_params("parallel"),
        name="out_proj_final" if final else "out_proj",
    )(mix1, mix2, w, w, h, g_next.reshape(1, d))


def _dilated_kernel(slopes_ref, q_ref, k_ref, v_ref, g_ref, o_ref,
                    qf, kf, vf, qp, kp, vp, acc_s, m_s, l_s):
    s_len = q_ref.shape[0]
    slope = slopes_ref[pl.program_id(1)]
    qf[...] = q_ref[...].astype(jnp.float32)
    kf[...] = k_ref[...].astype(jnp.float32)
    vf[...] = v_ref[...].astype(jnp.float32)
    n_blocks = s_len // QBLK

    for p_idx, (_, dil) in enumerate(A_PATTERNS):
        n = s_len // dil
        kw = min(2 * QBLK, n)
        blocks_per_class = n // QBLK
        for c in range(dil):
            rows = pl.ds(c, n, stride=dil) if dil > 1 else pl.ds(0, n)
            qp[c * n:(c + 1) * n, :] = qf[rows, :].astype(qp.dtype)
            kp[c * n:(c + 1) * n, :] = kf[rows, :].astype(kp.dtype)
            vp[c * n:(c + 1) * n, :] = vf[rows, :].astype(vp.dtype)
        rel0 = (lax.broadcasted_iota(jnp.int32, (QBLK, kw), 1)
                - lax.broadcasted_iota(jnp.int32, (QBLK, kw), 0))
        step = slope * float(dil)

        def block(t, carry, n=n, kw=kw, blocks_per_class=blocks_per_class, dil=dil, rel0=rel0,
                  step=step, p_idx=p_idx):
            c = t // blocks_per_class
            i0 = (t % blocks_per_class) * QBLK
            w0 = jnp.clip(i0 - A_SIDE, 0, n - kw)
            base = c * n
            q_blk = qp[pl.ds(pl.multiple_of(base + i0, QBLK), QBLK), :]
            k_win = kp[pl.ds(pl.multiple_of(base + w0, A_SIDE), kw), :]
            v_win = vp[pl.ds(pl.multiple_of(base + w0, A_SIDE), kw), :]
            dist = jnp.abs(rel0 + (w0 - i0)).astype(jnp.float32)
            dist = jnp.where(dist <= float(A_SIDE), dist, FAR)
            logits = _dot_nt(q_blk, k_win) * SCALE - step * dist
            m = jnp.max(logits, axis=-1, keepdims=True)
            p = jnp.exp(logits - m)
            l = jnp.sum(p, axis=-1, keepdims=True)
            acc = jnp.dot(p.astype(v_win.dtype), v_win, preferred_element_type=jnp.float32)
            start = c + dil * i0
            if dil > 1:
                nat = pl.ds(start, QBLK, stride=dil)
            else:
                nat = pl.ds(pl.multiple_of(start, QBLK), QBLK)
            acc_s[p_idx, nat, :] = acc
            m_s[p_idx, nat, :] = jnp.broadcast_to(m, (QBLK, HEAD_DIM))
            l_s[p_idx, nat, :] = jnp.broadcast_to(l, (QBLK, HEAD_DIM))
            return carry

        lax.fori_loop(0, n_blocks, block, 0)

    m_all = jnp.maximum(jnp.maximum(m_s[0], m_s[1]), m_s[2])
    num = jnp.zeros((s_len, HEAD_DIM), jnp.float32)
    den = jnp.zeros((s_len, HEAD_DIM), jnp.float32)
    for p_idx in range(len(A_PATTERNS)):
        e = jnp.exp(m_s[p_idx] - m_all)
        num = num + e * acc_s[p_idx]
        den = den + e * l_s[p_idx]
    o_ref[...] = ((num / den) * _silu(g_ref[...].astype(jnp.float32))).astype(o_ref.dtype)


def _dilated_attention(proj, slopes, q_off, k_off, v_off, g_off):
    b, s, _ = proj.shape
    col = lambda off: pl.BlockSpec((None, s, HEAD_DIM), lambda bi, hi: (bi, 0, off + hi))
    n_pat = len(A_PATTERNS)
    return pl.pallas_call(
        _dilated_kernel,
        grid=(b, N_HEADS),
        in_specs=[pl.BlockSpec(memory_space=pltpu.SMEM), col(q_off), col(k_off), col(v_off), col(g_off)],
        out_specs=pl.BlockSpec((None, s, HEAD_DIM), lambda bi, hi: (bi, 0, hi)),
        out_shape=jax.ShapeDtypeStruct((b, s, MIX_W), jnp.bfloat16),
        scratch_shapes=[pltpu.VMEM((s, HEAD_DIM), jnp.float32)] * 3
        + [pltpu.VMEM((s, HEAD_DIM), jnp.bfloat16)] * 3
        + [pltpu.VMEM((n_pat, s, HEAD_DIM), jnp.float32)] * 3,
        compiler_params=_params("parallel", "parallel"),
        name="dilated_attn",
    )(slopes, proj, proj, proj, proj)


def _na_bias_table(rpb, rows):
    kh = min(NA_ROWS, rows)
    r_rep = np.array([0, 1, 2, 3, 4, rows - 3, rows - 2, rows - 1])
    rs = np.clip(r_rep - kh // 2, 0, rows - kh)
    drow = rs[:, None] + np.arange(kh)[None, :] - r_rep[:, None] + (NA_ROWS_MAX - 1)
    c = np.arange(GRID_W)
    cs = np.clip(c - NA_COLS // 2, 0, GRID_W - NA_COLS)
    kc = np.arange(GRID_W)
    dcol = kc[None, :] - c[:, None] + (NA_COLS - 1)
    valid = (kc[None, :] >= cs[:, None]) & (kc[None, :] < cs[:, None] + NA_COLS)
    dcol = np.clip(dcol, 0, 2 * NA_COLS - 2)
    tab = rpb[:, drow[:, None, :, None], dcol[None, :, None, :]]
    tab = jnp.where(valid[None, None, :, None, :], tab.astype(jnp.float32), MASKED)
    return tab.reshape(rpb.shape[0], 8, GRID_W, kh * GRID_W)


def _na_kernel(q_ref, k_ref, v_ref, g_ref, bias_ref, o_ref):
    s_len = q_ref.shape[0]
    rows = s_len // GRID_W
    kh = min(NA_ROWS, rows)
    win = kh * GRID_W

    def row(r, carry):
        rs = jnp.clip(r - kh // 2, 0, rows - kh)
        variant = jnp.where(r < 4, r, jnp.where(r > rows - 4, r - (rows - 8), 4))
        q_row = q_ref[pl.ds(pl.multiple_of(r * GRID_W, GRID_W), GRID_W), :]
        k_win = k_ref[pl.ds(pl.multiple_of(rs * GRID_W, GRID_W), win), :]
        v_win = v_ref[pl.ds(pl.multiple_of(rs * GRID_W, GRID_W), win), :]
        logits = _dot_nt(q_row, k_win) * SCALE + bias_ref[variant]
        m = jnp.max(logits, axis=-1, keepdims=True)
        p = jnp.exp(logits - m)
        p = p / jnp.sum(p, axis=-1, keepdims=True)
        o = jnp.dot(p.astype(v_win.dtype), v_win, preferred_element_type=jnp.float32)
        sl = pl.ds(pl.multiple_of(r * GRID_W, GRID_W), GRID_W)
        o_ref[sl, :] = (o * _silu(g_ref[sl, :].astype(jnp.float32))).astype(o_ref.dtype)
        return carry

    lax.fori_loop(0, rows, row, 0)


def _neighbourhood_attention(proj, bias, q_off, k_off, v_off, g_off):
    b, s, _ = proj.shape
    col = lambda off: pl.BlockSpec((None, s, HEAD_DIM), lambda hi, bi: (bi, 0, off + hi))
    return pl.pallas_call(
        _na_kernel,
        grid=(N_HEADS, b),
        in_specs=[col(q_off), col(k_off), col(v_off), col(g_off),
                  pl.BlockSpec((None,) + bias.shape[1:], lambda hi, bi: (hi, 0, 0, 0))],
        out_specs=pl.BlockSpec((None, s, HEAD_DIM), lambda hi, bi: (bi, 0, hi)),
        out_shape=jax.ShapeDtypeStruct((b, s, MIX_W), jnp.bfloat16),
        compiler_params=_params("parallel", "parallel"),
        name="neighbourhood_attn",
    )(proj, proj, proj, proj, bias)


def _window_kernel(slopes_ref, sinks_ref, q_ref, k_ref, v_ref, g_ref, o_ref):
    s_len = q_ref.shape[0]
    hk = pl.program_id(1)
    kw = 3 * QBLK
    rel0 = (lax.broadcasted_iota(jnp.int32, (QBLK, kw), 1) - lax.broadcasted_iota(jnp.int32, (QBLK, kw), 0))

    def block(n, carry):
        q0 = pl.multiple_of(n * QBLK, QBLK)
        w0 = pl.multiple_of(jnp.clip(q0 - QBLK, 0, s_len - kw), QBLK)
        k_win = k_ref[pl.ds(w0, kw), :]
        v_win = v_ref[pl.ds(w0, kw), :]
        dist = jnp.abs(rel0 + (w0 - q0)).astype(jnp.float32)
        dist = jnp.where(dist <= float(C_WINDOW), dist, FAR)
        for gi in range(GROUP):
            head = hk * GROUP + gi
            slope = slopes_ref[head]
            sink = sinks_ref[head]
            lanes = slice(gi * HEAD_DIM, (gi + 1) * HEAD_DIM)
            logits = _dot_nt(q_ref[pl.ds(q0, QBLK), lanes], k_win) * SCALE - slope * dist
            m = jnp.maximum(jnp.max(logits, axis=-1, keepdims=True), sink)
            p = jnp.exp(logits - m)
            den = jnp.sum(p, axis=-1, keepdims=True) + jnp.exp(sink - m)
            o = jnp.dot(p.astype(v_win.dtype), v_win, preferred_element_type=jnp.float32) / den
            gate = _silu(g_ref[pl.ds(q0, QBLK), lanes].astype(jnp.float32))
            o_ref[pl.ds(q0, QBLK), lanes] = (o * gate).astype(o_ref.dtype)
        return carry

    lax.fori_loop(0, s_len // QBLK, block, 0)


def _gqa_specs(s, q_off, k_off, v_off, g_off):
    gw = GROUP * HEAD_DIM
    wide = lambda off: pl.BlockSpec((None, s, gw), lambda bi, hi: (bi, 0, off // GROUP + hi))
    col = lambda off: pl.BlockSpec((None, s, HEAD_DIM), lambda bi, hi: (bi, 0, off + hi))
    in_specs = [wide(q_off), col(k_off), col(v_off), wide(g_off)]
    out_spec = pl.BlockSpec((None, s, gw), lambda bi, hi: (bi, 0, hi))
    return in_specs, out_spec


def _windowed_sink_attention(proj, slopes, sinks, q_off, k_off, v_off, g_off):
    b, s, _ = proj.shape
    in_specs, out_spec = _gqa_specs(s, q_off, k_off, v_off, g_off)
    smem = pl.BlockSpec(memory_space=pltpu.SMEM)
    return pl.pallas_call(
        _window_kernel,
        grid=(b, KV_HEADS),
        in_specs=[smem, smem] + in_specs,
        out_specs=out_spec,
        out_shape=jax.ShapeDtypeStruct((b, s, MIX_W), jnp.bfloat16),
        compiler_params=_params("parallel", "parallel"),
        name="window_sink_attn",
    )(slopes, sinks, proj, proj, proj, proj)


def _rope_tables(s):
    t = np.arange(s)
    quarter = HEAD_DIM // 4
    inv = jnp.asarray(ROPE_THETA, jnp.float32) ** (-jnp.arange(quarter, dtype=jnp.float32) / quarter)
    ang_r = jnp.asarray(t // GRID_W, jnp.float32)[:, None] * inv[None, :]
    ang_c = jnp.asarray(t % GRID_W, jnp.float32)[:, None] * inv[None, :]
    cos = jnp.concatenate([jnp.cos(ang_r), jnp.cos(ang_r), jnp.cos(ang_c), jnp.cos(ang_c)], axis=-1)
    sin = jnp.concatenate([-jnp.sin(ang_r), jnp.sin(ang_r), -jnp.sin(ang_c), jnp.sin(ang_c)], axis=-1)
    return cos, sin


def _norm_rope(x, gain, cos, sin):
    y = x * lax.rsqrt(jnp.mean(x * x, axis=-1, keepdims=True) + EPS) * gain
    quarter = HEAD_DIM // 4
    lane = lax.broadcasted_iota(jnp.int32, y.shape, 1)
    partner = jnp.where((lane // quarter) % 2 == 0,
                        pltpu.roll(y, HEAD_DIM - quarter, 1),
                        pltpu.roll(y, quarter, 1))
    return y * cos + partner * sin


def _axial_kernel(q_ref, k_ref, v_ref, g_ref, cos_ref, sin_ref, gq_ref, gk_ref, o_ref, kp):
    s_len = q_ref.shape[0]
    kp[...] = _norm_rope(k_ref[...].astype(jnp.float32), gk_ref[...], cos_ref[...], sin_ref[...]).astype(kp.dtype)

    def block(n, carry):
        rows = pl.ds(pl.multiple_of(n * QBLK, QBLK), QBLK)
        cos = cos_ref[rows, :]
        sin = sin_ref[rows, :]
        for gi in range(GROUP):
            lanes = slice(gi * HEAD_DIM, (gi + 1) * HEAD_DIM)
            q = _norm_rope(q_ref[rows, lanes].astype(jnp.float32), gq_ref[...], cos, sin).astype(kp.dtype)
            logits = _dot_nt(q, kp[...]) * SCALE
            m = jnp.max(logits, axis=-1, keepdims=True)
            p = jnp.exp(logits - m)
            den = jnp.sum(p, axis=-1, keepdims=True)
            p = p / den
            o = jnp.dot(p.astype(v_ref.dtype), v_ref[...], preferred_element_type=jnp.float32)
            gate = _silu(g_ref[rows, lanes].astype(jnp.float32))
            o_ref[rows, lanes] = (o * gate).astype(o_ref.dtype)
        return carry

    lax.fori_loop(0, s_len // QBLK, block, 0)


def _axial_rope_attention(proj, cos, sin, gq, gk, q_off, k_off, v_off, g_off):
    b, s, _ = proj.shape
    in_specs, out_spec = _gqa_specs(s, q_off, k_off, v_off, g_off)
    table = pl.BlockSpec((s, HEAD_DIM), lambda bi, hi: (0, 0))
    gain = pl.BlockSpec((1, HEAD_DIM), lambda bi, hi: (0, 0))
    return pl.pallas_call(
        _axial_kernel,
        grid=(b, KV_HEADS),
        in_specs=in_specs + [table, table, gain, gain],
        out_specs=out_spec,
        out_shape=jax.ShapeDtypeStruct((b, s, MIX_W), jnp.bfloat16),
        scratch_shapes=[pltpu.VMEM((s, HEAD_DIM), jnp.bfloat16)],
        compiler_params=_params("parallel", "parallel"),
        name="axial_rope_attn",
    )(proj, proj, proj, proj, cos, sin, gq.reshape(1, HEAD_DIM), gk.reshape(1, HEAD_DIM))


def _alibi_slopes(n):
    return 2.0 ** (-8.0 * jnp.arange(1, n + 1, dtype=jnp.float32) / n)


def kernel(x, norm_g, final_g, w_in_even, w_out_even, rpb_b, w_in_odd, w_out_odd, sinks_c, qnorm_d, knorm_d):
    b, s, d = x.shape
    bf16 = jnp.bfloat16
    slopes = _alibi_slopes(N_HEADS)
    cos, sin = _rope_tables(s)
    blk = lambda width: width // HEAD_DIM

    h = x.reshape(b * s, d)
    y = _rms_norm_rows(h, norm_g[0], bf16)
    for layer in range(DEPTH):
        i = layer // 2
        final = layer == DEPTH - 1
        g_next = final_g if final else norm_g[layer + 1]
        if layer % 2 == 0:
            proj = _in_proj(y, w_in_even[i].astype(bf16)).reshape(b, s, -1)
            offs = [blk(MIX_W) * j for j in range(8)]
            mix1 = _dilated_attention(proj, slopes, *offs[:4])
            mix2 = _neighbourhood_attention(proj, _na_bias_table(rpb_b[i], s // GRID_W), *offs[4:])
            w_out = w_out_even[i]
        else:
            proj = _in_proj(y, w_in_odd[i].astype(bf16)).reshape(b, s, -1)
            widths = [MIX_W, KV_W, KV_W, MIX_W] * 2
            offs = [blk(int(sum(widths[:j]))) for j in range(8)]
            mix1 = _windowed_sink_attention(proj, slopes, sinks_c[i], *offs[:4])
            mix2 = _axial_rope_attention(proj, cos, sin, qnorm_d[i], knorm_d[i], *offs[4:])
            w_out = w_out_odd[i]
        res = _out_proj(mix1.reshape(b * s, MIX_W), mix2.reshape(b * s, MIX_W), w_out.astype(bf16),
                        h, g_next, final)
        if final:
            return res.reshape(b, s, d)
        h, y = res
```

```python
import functools

import jax
import jax.numpy as jnp
import numpy as np
from jax import lax
from jax.experimental import pallas as pl
from jax.experimental.pallas import tpu as pltpu

D_MODEL = 2048
DEPTH = 4
HEAD_DIM = 128
GRID_W = 64
EPS = 1e-6
N_HEADS = 8
KV_HEADS = 2
GROUP = N_HEADS // KV_HEADS
MIX_W = N_HEADS * HEAD_DIM
KV_W = KV_HEADS * HEAD_DIM
A_PATTERNS = ((128, 1), (512, 4), (2048, 16))
A_SIDE = 64
NA_ROWS = 8
NA_COLS = 16
NA_ROWS_MAX = 8
C_WINDOW = 128
QBLK = 128
ROPE_THETA = 10000.0
SCALE = HEAD_DIM ** -0.5
MASKED = -1e30
FAR = 1e9

VMEM_LIMIT = 56 * 1024 * 1024


def _params(*sem):
    return pltpu.CompilerParams(dimension_semantics=sem, vmem_limit_bytes=VMEM_LIMIT)


def _silu(x):
    return x * (1.0 / (1.0 + jnp.exp(-x)))


def _dot_nt(a, b):
    return lax.dot_general(a, b, (((1,), (1,)), ((), ())), preferred_element_type=jnp.float32)


def _norm_kernel(h_ref, g_ref, y_ref):
    x = h_ref[...]
    y = x * lax.rsqrt(jnp.mean(x * x, axis=-1, keepdims=True) + EPS)
    y_ref[...] = (y * g_ref[...]).astype(y_ref.dtype)


def _rms_norm_rows(h, g, out_dtype, tm=512):
    m, d = h.shape
    return pl.pallas_call(
        _norm_kernel,
        grid=(m // tm,),
        in_specs=[pl.BlockSpec((tm, d), lambda i: (i, 0)), pl.BlockSpec((1, d), lambda i: (0, 0))],
        out_specs=pl.BlockSpec((tm, d), lambda i: (i, 0)),
        out_shape=jax.ShapeDtypeStruct((m, d), out_dtype),
        compiler_params=_params("parallel"),
        name="rms_norm",
    )(h, g.reshape(1, d))


def _matmul_kernel(x_ref, w_ref, o_ref):
    o_ref[...] = jnp.dot(x_ref[...], w_ref[...], preferred_element_type=jnp.float32).astype(o_ref.dtype)


def _in_proj(y, w, tm=512, tn=1024):
    m, k = y.shape
    n = w.shape[1]
    return pl.pallas_call(
        _matmul_kernel,
        grid=(n // tn, m // tm),
        in_specs=[pl.BlockSpec((tm, k), lambda j, i: (i, 0)), pl.BlockSpec((k, tn), lambda j, i: (0, j))],
        out_specs=pl.BlockSpec((tm, tn), lambda j, i: (i, j)),
        out_shape=jax.ShapeDtypeStruct((m, n), jnp.bfloat16),
        compiler_params=_params("parallel", "parallel"),
        name="in_proj",
    )(y, w)


def _out_proj_kernel(m1_ref, m2_ref, w1_ref, w2_ref, h_ref, g_ref, *out_refs, final):
    acc = jnp.dot(m1_ref[...], w1_ref[...], preferred_element_type=jnp.float32)
    acc = acc + jnp.dot(m2_ref[...], w2_ref[...], preferred_element_type=jnp.float32)
    h = h_ref[...] + acc
    y = h * lax.rsqrt(jnp.mean(h * h, axis=-1, keepdims=True) + EPS) * g_ref[...]
    if final:
        out_refs[0][...] = y
    else:
        out_refs[0][...] = h
        out_refs[1][...] = y.astype(out_refs[1].dtype)


def _out_proj(mix1, mix2, w, h, g_next, final, tm=256):
    m, d = h.shape
    half = mix1.shape[1]
    row = lambda i: (i, 0)
    if final:
        out_shape = jax.ShapeDtypeStruct((m, d), jnp.float32)
        out_specs = pl.BlockSpec((tm, d), row)
    else:
        out_shape = (jax.ShapeDtypeStruct((m, d), jnp.float32), jax.ShapeDtypeStruct((m, d), jnp.bfloat16))
        out_specs = (pl.BlockSpec((tm, d), row), pl.BlockSpec((tm, d), row))
    return pl.pallas_call(
        functools.partial(_out_proj_kernel, final=final),
        grid=(m // tm,),
        in_specs=[pl.BlockSpec((tm, half), row), pl.BlockSpec((tm, half), row),
                  pl.BlockSpec((half, d), lambda i: (0, 0)), pl.BlockSpec((half, d), lambda i: (1, 0)),
                  pl.BlockSpec((tm, d), row), pl.BlockSpec((1, d), lambda i: (0, 0))],
        out_specs=out_specs,
        out_shape=out_shape,
        compiler_params=_params("parallel"),
        name="out_proj_final" if final else "out_proj",
    )(mix1, mix2, w, w, h, g_next.reshape(1, d))


def _dilated_kernel(slopes_ref, q_ref, k_ref, v_ref, g_ref, o_ref,
                    qf, kf, vf, qp, kp, vp, acc_s, m_s, l_s):
    s_len = q_ref.shape[0]
    slope = slopes_ref[pl.program_id(1)]
    qf[...] = q_ref[...].astype(jnp.float32)
    kf[...] = k_ref[...].astype(jnp.float32)
    vf[...] = v_ref[...].astype(jnp.float32)
    n_blocks = s_len // QBLK

    for p_idx, (_, dil) in enumerate(A_PATTERNS):
        n = s_len // dil
        kw = min(2 * QBLK, n)
        blocks_per_class = n // QBLK
        for c in range(dil):
            rows = pl.ds(c, n, stride=dil) if dil > 1 else pl.ds(0, n)
            qp[c * n:(c + 1) * n, :] = qf[rows, :].astype(qp.dtype)
            kp[c * n:(c + 1) * n, :] = kf[rows, :].astype(kp.dtype)
            vp[c * n:(c + 1) * n, :] = vf[rows, :].astype(vp.dtype)
        rel0 = (lax.broadcasted_iota(jnp.int32, (QBLK, kw), 1)
                - lax.broadcasted_iota(jnp.int32, (QBLK, kw), 0))
        step = slope * float(dil)
        bias = {}
        for t in range(n_blocks):
            c = t // blocks_per_class
            i0 = (t % blocks_per_class) * QBLK
            w0 = min(max(i0 - A_SIDE, 0), n - kw)
            base = c * n
            if w0 - i0 not in bias:
                dist = jnp.abs(rel0 + (w0 - i0)).astype(jnp.float32)
                bias[w0 - i0] = step * jnp.where(dist <= float(A_SIDE), dist, FAR)
            q_blk = qp[base + i0:base + i0 + QBLK, :]
            k_win = kp[base + w0:base + w0 + kw, :]
            v_win = vp[base + w0:base + w0 + kw, :]
            logits = _dot_nt(q_blk, k_win) * SCALE - bias[w0 - i0]
            m = jnp.max(logits, axis=-1, keepdims=True)
            p = jnp.exp(logits - m)
            l = jnp.sum(p, axis=-1, keepdims=True)
            acc = jnp.dot(p.astype(v_win.dtype), v_win, preferred_element_type=jnp.float32)
            start = c + dil * i0
            nat = pl.ds(start, QBLK, stride=dil) if dil > 1 else pl.ds(start, QBLK)
            acc_s[p_idx, nat, :] = acc
            m_s[p_idx, nat, :] = jnp.broadcast_to(m, (QBLK, HEAD_DIM))
            l_s[p_idx, nat, :] = jnp.broadcast_to(l, (QBLK, HEAD_DIM))

    m_all = jnp.maximum(jnp.maximum(m_s[0], m_s[1]), m_s[2])
    num = jnp.zeros((s_len, HEAD_DIM), jnp.float32)
    den = jnp.zeros((s_len, HEAD_DIM), jnp.float32)
    for p_idx in range(len(A_PATTERNS)):
        e = jnp.exp(m_s[p_idx] - m_all)
        num = num + e * acc_s[p_idx]
        den = den + e * l_s[p_idx]
    o_ref[...] = ((num / den) * _silu(g_ref[...].astype(jnp.float32))).astype(o_ref.dtype)


def _dilated_attention(proj, slopes, q_off, k_off, v_off, g_off):
    b, s, _ = proj.shape
    col = lambda off: pl.BlockSpec((None, s, HEAD_DIM), lambda bi, hi: (bi, 0, off + hi))
    n_pat = len(A_PATTERNS)
    return pl.pallas_call(
        _dilated_kernel,
        grid=(b, N_HEADS),
        in_specs=[pl.BlockSpec(memory_space=pltpu.SMEM), col(q_off), col(k_off), col(v_off), col(g_off)],
        out_specs=pl.BlockSpec((None, s, HEAD_DIM), lambda bi, hi: (bi, 0, hi)),
        out_shape=jax.ShapeDtypeStruct((b, s, MIX_W), jnp.bfloat16),
        scratch_shapes=[pltpu.VMEM((s, HEAD_DIM), jnp.float32)] * 3
        + [pltpu.VMEM((s, HEAD_DIM), jnp.bfloat16)] * 3
        + [pltpu.VMEM((n_pat, s, HEAD_DIM), jnp.float32)] * 3,
        compiler_params=_params("parallel", "parallel"),
        name="dilated_attn",
    )(slopes, proj, proj, proj, proj)


NA_VARIANTS = 8


def _na_variant_row(v, rows):
    return v if v <= 4 else rows - NA_VARIANTS + v


def _na_row_start(r, rows):
    kh = min(NA_ROWS, rows)
    return min(max(r - kh // 2, 0), rows - kh)


def _na_bias_kernel(rpb_ref, o_ref, pair_ref, *, rows):
    kh = min(NA_ROWS, rows)
    n_drow = 2 * NA_ROWS_MAX - 1
    n_dcol = 2 * NA_COLS - 1
    h = pl.program_id(0)
    shape = (GRID_W, 2 * GRID_W)
    lane = lax.broadcasted_iota(jnp.int32, shape, 1)
    c = lax.broadcasted_iota(jnp.int32, shape, 0)
    kc = lane % GRID_W
    second = lane >= GRID_W
    cs = jnp.clip(c - NA_COLS // 2, 0, GRID_W - NA_COLS)
    valid = (kc >= cs) & (kc < cs + NA_COLS)
    dcol = kc - c + (NA_COLS - 1)
    for d in range(n_drow - 1):
        acc = jnp.zeros(shape, jnp.float32)
        for j in range(n_dcol):
            lo = rpb_ref[(h * n_drow + d) * n_dcol + j]
            hi = rpb_ref[(h * n_drow + d + 1) * n_dcol + j]
            acc = jnp.where(dcol == j, jnp.where(second, hi, lo), acc)
        pair_ref[d] = jnp.where(valid, acc, MASKED)
    for v in range(NA_VARIANTS):
        r = _na_variant_row(v, rows)
        d0 = _na_row_start(r, rows) - r + (NA_ROWS_MAX - 1)
        for i in range(0, kh, 2):
            o_ref[v, :, i * GRID_W:(i + 2) * GRID_W] = pair_ref[d0 + i]


def _na_bias_table(rpb, rows):
    n_heads = rpb.shape[0]
    kh = min(NA_ROWS, rows)
    return pl.pallas_call(
        functools.partial(_na_bias_kernel, rows=rows),
        grid=(n_heads,),
        in_specs=[pl.BlockSpec(memory_space=pltpu.SMEM)],
        out_specs=pl.BlockSpec((None, NA_VARIANTS, GRID_W, kh * GRID_W), lambda hi: (hi, 0, 0, 0)),
        out_shape=jax.ShapeDtypeStruct((n_heads, NA_VARIANTS, GRID_W, kh * GRID_W), jnp.float32),
        scratch_shapes=[pltpu.VMEM((2 * NA_ROWS_MAX - 2, GRID_W, 2 * GRID_W), jnp.float32)],
        compiler_params=_params("parallel"),
        name="na_bias",
    )(rpb.reshape(-1))


def _na_kernel(q_ref, k_ref, v_ref, g_ref, bias_ref, o_ref):
    s_len = q_ref.shape[0]
    rows = s_len // GRID_W
    win = min(NA_ROWS, rows) * GRID_W
    for r in range(rows):
        k0 = _na_row_start(r, rows) * GRID_W
        variant = r if r < 4 else (r - (rows - NA_VARIANTS) if r > rows - 4 else 4)
        sl = slice(r * GRID_W, (r + 1) * GRID_W)
        k_win = k_ref[k0:k0 + win, :]
        v_win = v_ref[k0:k0 + win, :]
        logits = _dot_nt(q_ref[sl, :], k_win) * SCALE + bias_ref[variant]
        m = jnp.max(logits, axis=-1, keepdims=True)
        p = jnp.exp(logits - m)
        den = jnp.sum(p, axis=-1, keepdims=True)
        o = jnp.dot(p.astype(v_win.dtype), v_win, preferred_element_type=jnp.float32) / den
        o_ref[sl, :] = (o * _silu(g_ref[sl, :].astype(jnp.float32))).astype(o_ref.dtype)


def _neighbourhood_attention(proj, bias, q_off, k_off, v_off, g_off):
    b, s, _ = proj.shape
    col = lambda off: pl.BlockSpec((None, s, HEAD_DIM), lambda hi, bi: (bi, 0, off + hi))
    return pl.pallas_call(
        _na_kernel,
        grid=(N_HEADS, b),
        in_specs=[col(q_off), col(k_off), col(v_off), col(g_off),
                  pl.BlockSpec((None,) + bias.shape[1:], lambda hi, bi: (hi, 0, 0, 0))],
        out_specs=pl.BlockSpec((None, s, HEAD_DIM), lambda hi, bi: (bi, 0, hi)),
        out_shape=jax.ShapeDtypeStruct((b, s, MIX_W), jnp.bfloat16),
        compiler_params=_params("parallel", "parallel"),
        name="neighbourhood_attn",
    )(proj, proj, proj, proj, bias)


def _window_kernel(slopes_ref, sinks_ref, q_ref, k_ref, v_ref, g_ref, o_ref):
    s_len = q_ref.shape[0]
    hk = pl.program_id(1)
    kw = 3 * QBLK
    rel0 = (lax.broadcasted_iota(jnp.int32, (QBLK, kw), 1) - lax.broadcasted_iota(jnp.int32, (QBLK, kw), 0))
    slopes = [slopes_ref[hk * GROUP + gi] for gi in range(GROUP)]
    sinks = [sinks_ref[hk * GROUP + gi] for gi in range(GROUP)]
    bias = {}
    for n in range(s_len // QBLK):
        q0 = n * QBLK
        w0 = min(max(q0 - QBLK, 0), s_len - kw)
        if (w0 - q0, 0) not in bias:
            dist = jnp.abs(rel0 + (w0 - q0)).astype(jnp.float32)
            dist = jnp.where(dist <= float(C_WINDOW), dist, FAR)
            for gi in range(GROUP):
                bias[(w0 - q0, gi)] = slopes[gi] * dist
        k_win = k_ref[w0:w0 + kw, :]
        v_win = v_ref[w0:w0 + kw, :]
        for gi in range(GROUP):
            lanes = slice(gi * HEAD_DIM, (gi + 1) * HEAD_DIM)
            logits = _dot_nt(q_ref[q0:q0 + QBLK, lanes], k_win) * SCALE - bias[(w0 - q0, gi)]
            m = jnp.maximum(jnp.max(logits, axis=-1, keepdims=True), sinks[gi])
            p = jnp.exp(logits - m)
            den = jnp.sum(p, axis=-1, keepdims=True) + jnp.exp(sinks[gi] - m)
            o = jnp.dot(p.astype(v_win.dtype), v_win, preferred_element_type=jnp.float32) / den
            gate = _silu(g_ref[q0:q0 + QBLK, lanes].astype(jnp.float32))
            o_ref[q0:q0 + QBLK, lanes] = (o * gate).astype(o_ref.dtype)


def _gqa_specs(s, q_off, k_off, v_off, g_off):
    gw = GROUP * HEAD_DIM
    wide = lambda off: pl.BlockSpec((None, s, gw), lambda bi, hi: (bi, 0, off // GROUP + hi))
    col = lambda off: pl.BlockSpec((None, s, HEAD_DIM), lambda bi, hi: (bi, 0, off + hi))
    in_specs = [wide(q_off), col(k_off), col(v_off), wide(g_off)]
    out_spec = pl.BlockSpec((None, s, gw), lambda bi, hi: (bi, 0, hi))
    return in_specs, out_spec


def _windowed_sink_attention(proj, slopes, sinks, q_off, k_off, v_off, g_off):
    b, s, _ = proj.shape
    in_specs, out_spec = _gqa_specs(s, q_off, k_off, v_off, g_off)
    smem = pl.BlockSpec(memory_space=pltpu.SMEM)
    return pl.pallas_call(
        _window_kernel,
        grid=(b, KV_HEADS),
        in_specs=[smem, smem] + in_specs,
        out_specs=out_spec,
        out_shape=jax.ShapeDtypeStruct((b, s, MIX_W), jnp.bfloat16),
        compiler_params=_params("parallel", "parallel"),
        name="window_sink_attn",
    )(slopes, sinks, proj, proj, proj, proj)


def _rope_tables(s):
    t = np.arange(s)
    quarter = HEAD_DIM // 4
    inv = jnp.asarray(ROPE_THETA, jnp.float32) ** (-jnp.arange(quarter, dtype=jnp.float32) / quarter)
    ang_r = jnp.asarray(t // GRID_W, jnp.float32)[:, None] * inv[None, :]
    ang_c = jnp.asarray(t % GRID_W, jnp.float32)[:, None] * inv[None, :]
    cos = jnp.concatenate([jnp.cos(ang_r), jnp.cos(ang_r), jnp.cos(ang_c), jnp.cos(ang_c)], axis=-1)
    sin = jnp.concatenate([-jnp.sin(ang_r), jnp.sin(ang_r), -jnp.sin(ang_c), jnp.sin(ang_c)], axis=-1)
    return cos, sin


def _norm_rope(x, gain, cos, sin):
    y = x * lax.rsqrt(jnp.mean(x * x, axis=-1, keepdims=True) + EPS) * gain
    quarter = HEAD_DIM // 4
    lane = lax.broadcasted_iota(jnp.int32, y.shape, 1)
    partner = jnp.where((lane // quarter) % 2 == 0,
                        pltpu.roll(y, HEAD_DIM - quarter, 1),
                        pltpu.roll(y, quarter, 1))
    return y * cos + partner * sin


def _axial_kernel(q_ref, k_ref, v_ref, g_ref, cos_ref, sin_ref, gq_ref, gk_ref, o_ref, kp):
    s_len = q_ref.shape[0]
    kp[...] = _norm_rope(k_ref[...].astype(jnp.float32), gk_ref[...], cos_ref[...], sin_ref[...]).astype(kp.dtype)

    def block(n, carry):
        rows = pl.ds(pl.multiple_of(n * QBLK, QBLK), QBLK)
        cos = cos_ref[rows, :]
        sin = sin_ref[rows, :]
        for gi in range(GROUP):
            lanes = slice(gi * HEAD_DIM, (gi + 1) * HEAD_DIM)
            q = _norm_rope(q_ref[rows, lanes].astype(jnp.float32), gq_ref[...], cos, sin).astype(kp.dtype)
            logits = _dot_nt(q, kp[...]) * SCALE
            m = jnp.max(logits, axis=-1, keepdims=True)
            p = jnp.exp(logits - m)
            den = jnp.sum(p, axis=-1, keepdims=True)
            o = jnp.dot(p.astype(v_ref.dtype), v_ref[...], preferred_element_type=jnp.float32) / den
            gate = _silu(g_ref[rows, lanes].astype(jnp.float32))
            o_ref[rows, lanes] = (o * gate).astype(o_ref.dtype)
        return carry

    lax.fori_loop(0, s_len // QBLK, block, 0, unroll=2)


def _axial_rope_attention(proj, cos, sin, gq, gk, q_off, k_off, v_off, g_off):
    b, s, _ = proj.shape
    in_specs, out_spec = _gqa_specs(s, q_off, k_off, v_off, g_off)
    table = pl.BlockSpec((s, HEAD_DIM), lambda bi, hi: (0, 0))
    gain = pl.BlockSpec((1, HEAD_DIM), lambda bi, hi: (0, 0))
    return pl.pallas_call(
        _axial_kernel,
        grid=(b, KV_HEADS),
        in_specs=in_specs + [table, table, gain, gain],
        out_specs=out_spec,
        out_shape=jax.ShapeDtypeStruct((b, s, MIX_W), jnp.bfloat16),
        scratch_shapes=[pltpu.VMEM((s, HEAD_DIM), jnp.bfloat16)],
        compiler_params=_params("parallel", "parallel"),
        name="axial_rope_attn",
    )(proj, proj, proj, proj, cos, sin, gq.reshape(1, HEAD_DIM), gk.reshape(1, HEAD_DIM))


def _alibi_slopes(n):
    return 2.0 ** (-8.0 * jnp.arange(1, n + 1, dtype=jnp.float32) / n)


def kernel(x, norm_g, final_g, w_in_even, w_out_even, rpb_b, w_in_odd, w_out_odd, sinks_c, qnorm_d, knorm_d):
    b, s, d = x.shape
    bf16 = jnp.bfloat16
    slopes = _alibi_slopes(N_HEADS)
    cos, sin = _rope_tables(s)
    blk = lambda width: width // HEAD_DIM

    h = x.reshape(b * s, d)
    y = _rms_norm_rows(h, norm_g[0], bf16)
    for layer in range(DEPTH):
        i = layer // 2
        final = layer == DEPTH - 1
        g_next = final_g if final else norm_g[layer + 1]
        if layer % 2 == 0:
            proj = _in_proj(y, w_in_even[i].astype(bf16)).reshape(b, s, -1)
            offs = [blk(MIX_W) * j for j in range(8)]
            mix1 = _dilated_attention(proj, slopes, *offs[:4])
            mix2 = _neighbourhood_attention(proj, _na_bias_table(rpb_b[i], s // GRID_W), *offs[4:])
            w_out = w_out_even[i]
        else:
            proj = _in_proj(y, w_in_odd[i].astype(bf16)).reshape(b, s, -1)
            widths = [MIX_W, KV_W, KV_W, MIX_W] * 2
            offs = [blk(int(sum(widths[:j]))) for j in range(8)]
            mix1 = _windowed_sink_attention(proj, slopes, sinks_c[i], *offs[:4])
            mix2 = _axial_rope_attention(proj, cos, sin, qnorm_d[i], knorm_d[i], *offs[4:])
            w_out = w_out_odd[i]
        res = _out_proj(mix1.reshape(b * s, MIX_W), mix2.reshape(b * s, MIX_W), w_out.astype(bf16),
                        h, g_next, final)
        if final:
            return res.reshape(b, s, d)
        h, y = res
```

```python
import functools

import jax
import jax.numpy as jnp
import numpy as np
from jax import lax
from jax.experimental import pallas as pl
from jax.experimental.pallas import tpu as pltpu

D_MODEL = 2048
DEPTH = 4
HEAD_DIM = 128
GRID_W = 64
EPS = 1e-6
N_HEADS = 8
KV_HEADS = 2
GROUP = N_HEADS // KV_HEADS
MIX_W = N_HEADS * HEAD_DIM
KV_W = KV_HEADS * HEAD_DIM
A_PATTERNS = ((128, 1), (512, 4), (2048, 16))
A_SIDE = 64
NA_ROWS = 8
NA_COLS = 16
NA_ROWS_MAX = 8
C_WINDOW = 128
QBLK = 128
ROPE_THETA = 10000.0
SCALE = HEAD_DIM ** -0.5
LOG2E = 1.4426950408889634
MASKED = -1e30
FAR = 1e9
AHEAD = 4

VMEM_LIMIT = 56 * 1024 * 1024


def _params(*sem):
    return pltpu.CompilerParams(dimension_semantics=sem, vmem_limit_bytes=VMEM_LIMIT)


def _silu(x):
    return x * (1.0 / (1.0 + jnp.exp(-x)))


def _dot_nt(a, b):
    return lax.dot_general(a, b, (((1,), (1,)), ((), ())), preferred_element_type=jnp.float32)


def _norm_kernel(h_ref, g_ref, y_ref):
    x = h_ref[...]
    y = x * lax.rsqrt(jnp.mean(x * x, axis=-1, keepdims=True) + EPS)
    y_ref[...] = (y * g_ref[...]).astype(y_ref.dtype)


def _rms_norm_rows(h, g, out_dtype, tm=512):
    m, d = h.shape
    return pl.pallas_call(
        _norm_kernel,
        grid=(m // tm,),
        in_specs=[pl.BlockSpec((tm, d), lambda i: (i, 0)), pl.BlockSpec((1, d), lambda i: (0, 0))],
        out_specs=pl.BlockSpec((tm, d), lambda i: (i, 0)),
        out_shape=jax.ShapeDtypeStruct((m, d), out_dtype),
        compiler_params=_params("parallel"),
        name="rms_norm",
    )(h, g.reshape(1, d))


def _matmul_kernel(x_ref, w_ref, o_ref):
    o_ref[...] = jnp.dot(x_ref[...], w_ref[...], preferred_element_type=jnp.float32).astype(o_ref.dtype)


def _in_proj(y, w, tm=512, tn=1024):
    m, k = y.shape
    n = w.shape[1]
    return pl.pallas_call(
        _matmul_kernel,
        grid=(n // tn, m // tm),
        in_specs=[pl.BlockSpec((tm, k), lambda j, i: (i, 0)), pl.BlockSpec((k, tn), lambda j, i: (0, j))],
        out_specs=pl.BlockSpec((tm, tn), lambda j, i: (i, j)),
        out_shape=jax.ShapeDtypeStruct((m, n), jnp.bfloat16),
        compiler_params=_params("parallel", "parallel"),
        name="in_proj",
    )(y, w)


def _out_proj_kernel(m1_ref, m2_ref, w1_ref, w2_ref, h_ref, g_ref, *out_refs, final):
    acc = jnp.dot(m1_ref[...], w1_ref[...], preferred_element_type=jnp.float32)
    acc = acc + jnp.dot(m2_ref[...], w2_ref[...], preferred_element_type=jnp.float32)
    h = h_ref[...] + acc
    y = h * lax.rsqrt(jnp.mean(h * h, axis=-1, keepdims=True) + EPS) * g_ref[...]
    if final:
        out_refs[0][...] = y
    else:
        out_refs[0][...] = h
        out_refs[1][...] = y.astype(out_refs[1].dtype)


def _out_proj(mix1, mix2, w, h, g_next, final, tm=256):
    m, d = h.shape
    half = mix1.shape[1]
    row = lambda i: (i, 0)
    if final:
        out_shape = jax.ShapeDtypeStruct((m, d), jnp.float32)
        out_specs = pl.BlockSpec((tm, d), row)
    else:
        out_shape = (jax.ShapeDtypeStruct((m, d), jnp.float32), jax.ShapeDtypeStruct((m, d), jnp.bfloat16))
        out_specs = (pl.BlockSpec((tm, d), row), pl.BlockSpec((tm, d), row))
    return pl.pallas_call(
        functools.partial(_out_proj_kernel, final=final),
        grid=(m // tm,),
        in_specs=[pl.BlockSpec((tm, half), row), pl.BlockSpec((tm, half), row),
                  pl.BlockSpec((half, d), lambda i: (0, 0)), pl.BlockSpec((half, d), lambda i: (1, 0)),
                  pl.BlockSpec((tm, d), row), pl.BlockSpec((1, d), lambda i: (0, 0))],
        out_specs=out_specs,
        out_shape=out_shape,
        compiler_params=_params("parallel"),
        name="out_proj_final" if final else "out_proj",
    )(mix1, mix2, w, w, h, g_next.reshape(1, d))


def _dilated_kernel(slopes_ref, q_ref, k_ref, v_ref, g_ref, o_ref,
                    qf, kf, vf, qp, kp, vp, acc_s, m_s, l_s):
    s_len = q_ref.shape[0]
    slope = slopes_ref[pl.program_id(1)]
    qf[...] = q_ref[...].astype(jnp.float32)
    kf[...] = k_ref[...].astype(jnp.float32)
    vf[...] = v_ref[...].astype(jnp.float32)

    tiles = []
    srcs = []
    bias = {}
    for p_idx, (_, dil) in enumerate(A_PATTERNS):
        n = s_len // dil
        kw = min(2 * QBLK, n)
        if dil == 1:
            srcs.append((q_ref, k_ref, v_ref))
        else:
            for c in range(dil):
                rows = pl.ds(c, n, stride=dil)
                qp[p_idx - 1, c * n:(c + 1) * n, :] = qf[rows, :].astype(qp.dtype)
                kp[p_idx - 1, c * n:(c + 1) * n, :] = kf[rows, :].astype(kp.dtype)
                vp[p_idx - 1, c * n:(c + 1) * n, :] = vf[rows, :].astype(vp.dtype)
            srcs.append((qp.at[p_idx - 1], kp.at[p_idx - 1], vp.at[p_idx - 1]))
        rel0 = (lax.broadcasted_iota(jnp.int32, (QBLK, kw), 1)
                - lax.broadcasted_iota(jnp.int32, (QBLK, kw), 0))
        for t in range(s_len // QBLK):
            c, i0 = divmod(t * QBLK, n)
            w0 = min(max(i0 - A_SIDE, 0), n - kw)
            if (p_idx, w0 - i0) not in bias:
                dist = jnp.abs(rel0 + (w0 - i0)).astype(jnp.float32)
                bias[(p_idx, w0 - i0)] = (slope * (dil * LOG2E)) * jnp.where(dist <= float(A_SIDE), dist, FAR)
            start = c + dil * i0
            nat = pl.ds(start, QBLK, stride=dil) if dil > 1 else pl.ds(start, QBLK)
            tiles.append((p_idx, slice(c * n + i0, c * n + i0 + QBLK), slice(c * n + w0, c * n + w0 + kw),
                          w0 - i0, nat))

    def scores(tile):
        p_idx, q_rows, k_rows, _, _ = tile
        q_src, k_src, _ = srcs[p_idx]
        return _dot_nt(q_src[q_rows, :], k_src[k_rows, :])

    pending = [scores(t) for t in tiles[:AHEAD]]
    for idx, (p_idx, _, k_rows, off, nat) in enumerate(tiles):
        if idx + AHEAD < len(tiles):
            pending.append(scores(tiles[idx + AHEAD]))
        v_win = srcs[p_idx][2][k_rows, :]
        logits = pending.pop(0) * (SCALE * LOG2E) - bias[(p_idx, off)]
        m = jnp.max(logits, axis=-1, keepdims=True)
        p = jnp.exp2(logits - m)
        l = jnp.sum(p, axis=-1, keepdims=True)
        acc_s[p_idx, nat, :] = jnp.dot(p.astype(v_win.dtype), v_win, preferred_element_type=jnp.float32)
        m_s[p_idx, nat, :] = jnp.broadcast_to(m, (QBLK, HEAD_DIM))
        l_s[p_idx, nat, :] = jnp.broadcast_to(l, (QBLK, HEAD_DIM))

    m_all = jnp.maximum(jnp.maximum(m_s[0], m_s[1]), m_s[2])
    num = jnp.zeros((s_len, HEAD_DIM), jnp.float32)
    den = jnp.zeros((s_len, HEAD_DIM), jnp.float32)
    for p_idx in range(len(A_PATTERNS)):
        e = jnp.exp2(m_s[p_idx] - m_all)
        num = num + e * acc_s[p_idx]
        den = den + e * l_s[p_idx]
    o_ref[...] = ((num / den) * _silu(g_ref[...].astype(jnp.float32))).astype(o_ref.dtype)


def _dilated_attention(proj, slopes, q_off, k_off, v_off, g_off):
    b, s, _ = proj.shape
    col = lambda off: pl.BlockSpec((None, s, HEAD_DIM), lambda bi, hi: (bi, 0, off + hi))
    n_pat = len(A_PATTERNS)
    return pl.pallas_call(
        _dilated_kernel,
        grid=(b, N_HEADS),
        in_specs=[pl.BlockSpec(memory_space=pltpu.SMEM), col(q_off), col(k_off), col(v_off), col(g_off)],
        out_specs=pl.BlockSpec((None, s, HEAD_DIM), lambda bi, hi: (bi, 0, hi)),
        out_shape=jax.ShapeDtypeStruct((b, s, MIX_W), jnp.bfloat16),
        scratch_shapes=[pltpu.VMEM((s, HEAD_DIM), jnp.float32)] * 3
        + [pltpu.VMEM((n_pat - 1, s, HEAD_DIM), jnp.bfloat16)] * 3
        + [pltpu.VMEM((n_pat, s, HEAD_DIM), jnp.float32)] * 3,
        compiler_params=_params("parallel", "parallel"),
        name="dilated_attn",
    )(slopes, proj, proj, proj, proj)


NA_VARIANTS = 8


def _na_variant_row(v, rows):
    return v if v <= 4 else rows - NA_VARIANTS + v


def _na_row_start(r, rows):
    kh = min(NA_ROWS, rows)
    return min(max(r - kh // 2, 0), rows - kh)


def _na_bias_kernel(rpb_ref, o_ref, pair_ref, *, rows):
    kh = min(NA_ROWS, rows)
    n_drow = 2 * NA_ROWS_MAX - 1
    n_dcol = 2 * NA_COLS - 1
    h = pl.program_id(0)
    shape = (GRID_W, 2 * GRID_W)
    lane = lax.broadcasted_iota(jnp.int32, shape, 1)
    c = lax.broadcasted_iota(jnp.int32, shape, 0)
    kc = lane % GRID_W
    second = lane >= GRID_W
    cs = jnp.clip(c - NA_COLS // 2, 0, GRID_W - NA_COLS)
    valid = (kc >= cs) & (kc < cs + NA_COLS)
    dcol = kc - c + (NA_COLS - 1)
    for d in range(n_drow - 1):
        acc = jnp.zeros(shape, jnp.float32)
        for j in range(n_dcol):
            lo = rpb_ref[(h * n_drow + d) * n_dcol + j]
            hi = rpb_ref[(h * n_drow + d + 1) * n_dcol + j]
            acc = jnp.where(dcol == j, jnp.where(second, hi, lo), acc)
        pair_ref[d] = jnp.where(valid, acc * LOG2E, MASKED)
    for v in range(NA_VARIANTS):
        r = _na_variant_row(v, rows)
        d0 = _na_row_start(r, rows) - r + (NA_ROWS_MAX - 1)
        for i in range(0, kh, 2):
            o_ref[v, :, i * GRID_W:(i + 2) * GRID_W] = pair_ref[d0 + i]


def _na_bias_table(rpb, rows):
    n_heads = rpb.shape[0]
    kh = min(NA_ROWS, rows)
    return pl.pallas_call(
        functools.partial(_na_bias_kernel, rows=rows),
        grid=(n_heads,),
        in_specs=[pl.BlockSpec(memory_space=pltpu.SMEM)],
        out_specs=pl.BlockSpec((None, NA_VARIANTS, GRID_W, kh * GRID_W), lambda hi: (hi, 0, 0, 0)),
        out_shape=jax.ShapeDtypeStruct((n_heads, NA_VARIANTS, GRID_W, kh * GRID_W), jnp.float32),
        scratch_shapes=[pltpu.VMEM((2 * NA_ROWS_MAX - 2, GRID_W, 2 * GRID_W), jnp.float32)],
        compiler_params=_params("parallel"),
        name="na_bias",
    )(rpb.reshape(-1))


def _na_kernel(q_ref, k_ref, v_ref, g_ref, bias_ref, o_ref):
    s_len = q_ref.shape[0]
    rows = s_len // GRID_W
    win = min(NA_ROWS, rows) * GRID_W
    def scores(r):
        k0 = _na_row_start(r, rows) * GRID_W
        return _dot_nt(q_ref[r * GRID_W:(r + 1) * GRID_W, :], k_ref[k0:k0 + win, :])

    pending = [scores(r) for r in range(AHEAD)]
    for r in range(rows):
        if r + AHEAD < rows:
            pending.append(scores(r + AHEAD))
        k0 = _na_row_start(r, rows) * GRID_W
        variant = r if r < 4 else (r - (rows - NA_VARIANTS) if r > rows - 4 else 4)
        sl = slice(r * GRID_W, (r + 1) * GRID_W)
        v_win = v_ref[k0:k0 + win, :]
        logits = pending.pop(0) * (SCALE * LOG2E) + bias_ref[variant]
        m = jnp.max(logits, axis=-1, keepdims=True)
        p = jnp.exp2(logits - m)
        den = jnp.sum(p, axis=-1, keepdims=True)
        o = jnp.dot(p.astype(v_win.dtype), v_win, preferred_element_type=jnp.float32) / den
        o_ref[sl, :] = (o * _silu(g_ref[sl, :].astype(jnp.float32))).astype(o_ref.dtype)


def _neighbourhood_attention(proj, bias, q_off, k_off, v_off, g_off):
    b, s, _ = proj.shape
    col = lambda off: pl.BlockSpec((None, s, HEAD_DIM), lambda hi, bi: (bi, 0, off + hi))
    return pl.pallas_call(
        _na_kernel,
        grid=(N_HEADS, b),
        in_specs=[col(q_off), col(k_off), col(v_off), col(g_off),
                  pl.BlockSpec((None,) + bias.shape[1:], lambda hi, bi: (hi, 0, 0, 0))],
        out_specs=pl.BlockSpec((None, s, HEAD_DIM), lambda hi, bi: (bi, 0, hi)),
        out_shape=jax.ShapeDtypeStruct((b, s, MIX_W), jnp.bfloat16),
        compiler_params=_params("parallel", "parallel"),
        name="neighbourhood_attn",
    )(proj, proj, proj, proj, bias)


def _window_kernel(slopes_ref, sinks_ref, q_ref, k_ref, v_ref, g_ref, o_ref):
    s_len = q_ref.shape[0]
    hk = pl.program_id(1)
    kw = 3 * QBLK
    rel0 = (lax.broadcasted_iota(jnp.int32, (QBLK, kw), 1) - lax.broadcasted_iota(jnp.int32, (QBLK, kw), 0))
    slopes = [slopes_ref[hk * GROUP + gi] for gi in range(GROUP)]
    sinks = [sinks_ref[hk * GROUP + gi] for gi in range(GROUP)]
    sinks = [sink * LOG2E for sink in sinks]
    bias = {}
    tiles = []
    for n in range(s_len // QBLK):
        q0 = n * QBLK
        w0 = min(max(q0 - QBLK, 0), s_len - kw)
        if (w0 - q0, 0) not in bias:
            dist = jnp.abs(rel0 + (w0 - q0)).astype(jnp.float32)
            dist = jnp.where(dist <= float(C_WINDOW), dist, FAR)
            for gi in range(GROUP):
                bias[(w0 - q0, gi)] = (slopes[gi] * LOG2E) * dist
        tiles += [(slice(q0, q0 + QBLK), gi, slice(w0, w0 + kw), w0 - q0) for gi in range(GROUP)]

    def scores(tile):
        q_rows, gi, k_rows, _ = tile
        return _dot_nt(q_ref[q_rows, gi * HEAD_DIM:(gi + 1) * HEAD_DIM], k_ref[k_rows, :])

    pending = [scores(t) for t in tiles[:AHEAD]]
    for idx, (q_rows, gi, k_rows, off) in enumerate(tiles):
        if idx + AHEAD < len(tiles):
            pending.append(scores(tiles[idx + AHEAD]))
        lanes = slice(gi * HEAD_DIM, (gi + 1) * HEAD_DIM)
        v_win = v_ref[k_rows, :]
        logits = pending.pop(0) * (SCALE * LOG2E) - bias[(off, gi)]
        m = jnp.maximum(jnp.max(logits, axis=-1, keepdims=True), sinks[gi])
        p = jnp.exp2(logits - m)
        den = jnp.sum(p, axis=-1, keepdims=True) + jnp.exp2(sinks[gi] - m)
        o = jnp.dot(p.astype(v_win.dtype), v_win, preferred_element_type=jnp.float32) / den
        gate = _silu(g_ref[q_rows, lanes].astype(jnp.float32))
        o_ref[q_rows, lanes] = (o * gate).astype(o_ref.dtype)


def _gqa_specs(s, q_off, k_off, v_off, g_off):
    gw = GROUP * HEAD_DIM
    wide = lambda off: pl.BlockSpec((None, s, gw), lambda bi, hi: (bi, 0, off // GROUP + hi))
    col = lambda off: pl.BlockSpec((None, s, HEAD_DIM), lambda bi, hi: (bi, 0, off + hi))
    in_specs = [wide(q_off), col(k_off), col(v_off), wide(g_off)]
    out_spec = pl.BlockSpec((None, s, gw), lambda bi, hi: (bi, 0, hi))
    return in_specs, out_spec


def _windowed_sink_attention(proj, slopes, sinks, q_off, k_off, v_off, g_off):
    b, s, _ = proj.shape
    in_specs, out_spec = _gqa_specs(s, q_off, k_off, v_off, g_off)
    smem = pl.BlockSpec(memory_space=pltpu.SMEM)
    return pl.pallas_call(
        _window_kernel,
        grid=(b, KV_HEADS),
        in_specs=[smem, smem] + in_specs,
        out_specs=out_spec,
        out_shape=jax.ShapeDtypeStruct((b, s, MIX_W), jnp.bfloat16),
        compiler_params=_params("parallel", "parallel"),
        name="window_sink_attn",
    )(slopes, sinks, proj, proj, proj, proj)


def _rope_tables(s):
    t = np.arange(s)
    quarter = HEAD_DIM // 4
    inv = jnp.asarray(ROPE_THETA, jnp.float32) ** (-jnp.arange(quarter, dtype=jnp.float32) / quarter)
    ang_r = jnp.asarray(t // GRID_W, jnp.float32)[:, None] * inv[None, :]
    ang_c = jnp.asarray(t % GRID_W, jnp.float32)[:, None] * inv[None, :]
    cos = jnp.concatenate([jnp.cos(ang_r), jnp.cos(ang_r), jnp.cos(ang_c), jnp.cos(ang_c)], axis=-1)
    sin = jnp.concatenate([-jnp.sin(ang_r), jnp.sin(ang_r), -jnp.sin(ang_c), jnp.sin(ang_c)], axis=-1)
    return cos, sin


def _norm_rope(x, gain, cos, sin):
    y = x * lax.rsqrt(jnp.mean(x * x, axis=-1, keepdims=True) + EPS) * gain
    quarter = HEAD_DIM // 4
    lane = lax.broadcasted_iota(jnp.int32, y.shape, 1)
    partner = jnp.where((lane // quarter) % 2 == 0,
                        pltpu.roll(y, HEAD_DIM - quarter, 1),
                        pltpu.roll(y, quarter, 1))
    return y * cos + partner * sin


def _axial_kernel(q_ref, k_ref, v_ref, g_ref, cos_ref, sin_ref, gq_ref, gk_ref, o_ref, kp, q4, s_buf):
    s_len = q_ref.shape[0]
    n_blk = s_len // QBLK
    kp[...] = _norm_rope(k_ref[...].astype(jnp.float32), gk_ref[...], cos_ref[...], sin_ref[...]).astype(kp.dtype)

    def block_rows(n):
        return pl.ds(pl.multiple_of(n * QBLK, QBLK), QBLK)

    def scores(n, slot):
        rows = block_rows(n)
        cos = cos_ref[rows, :]
        sin = sin_ref[rows, :]
        for gi in range(GROUP):
            q = _norm_rope(q_ref[rows, gi * HEAD_DIM:(gi + 1) * HEAD_DIM].astype(jnp.float32), gq_ref[...], cos, sin)
            q4[slot, gi * QBLK:(gi + 1) * QBLK, :] = (q * (SCALE * LOG2E)).astype(q4.dtype)
        s_buf[slot] = _dot_nt(q4[slot], kp[...])

    def finish(n, slot):
        rows = block_rows(n)
        logits = s_buf[slot]
        p = jnp.exp2(logits - jnp.max(logits, axis=-1, keepdims=True))
        den = jnp.sum(p, axis=-1, keepdims=True)
        o = jnp.dot(p.astype(v_ref.dtype), v_ref[...], preferred_element_type=jnp.float32) / den
        for gi in range(GROUP):
            lanes = slice(gi * HEAD_DIM, (gi + 1) * HEAD_DIM)
            gate = _silu(g_ref[rows, lanes].astype(jnp.float32))
            o_ref[rows, lanes] = (o[gi * QBLK:(gi + 1) * QBLK, :] * gate).astype(o_ref.dtype)

    scores(0, 0)

    def pair(i, carry):
        n = 2 * i
        scores(n + 1, 1)
        finish(n, 0)
        scores(jnp.minimum(n + 2, n_blk - 1), 0)
        finish(n + 1, 1)
        return carry

    lax.fori_loop(0, n_blk // 2, pair, 0)


def _axial_rope_attention(proj, cos, sin, gq, gk, q_off, k_off, v_off, g_off):
    b, s, _ = proj.shape
    in_specs, out_spec = _gqa_specs(s, q_off, k_off, v_off, g_off)
    table = pl.BlockSpec((s, HEAD_DIM), lambda bi, hi: (0, 0))
    gain = pl.BlockSpec((1, HEAD_DIM), lambda bi, hi: (0, 0))
    return pl.pallas_call(
        _axial_kernel,
        grid=(b, KV_HEADS),
        in_specs=in_specs + [table, table, gain, gain],
        out_specs=out_spec,
        out_shape=jax.ShapeDtypeStruct((b, s, MIX_W), jnp.bfloat16),
        scratch_shapes=[pltpu.VMEM((s, HEAD_DIM), jnp.bfloat16),
                        pltpu.VMEM((2, GROUP * QBLK, HEAD_DIM), jnp.bfloat16),
                        pltpu.VMEM((2, GROUP * QBLK, s), jnp.float32)],
        compiler_params=_params("parallel", "parallel"),
        name="axial_rope_attn",
    )(proj, proj, proj, proj, cos, sin, gq.reshape(1, HEAD_DIM), gk.reshape(1, HEAD_DIM))


def _alibi_slopes(n):
    return 2.0 ** (-8.0 * jnp.arange(1, n + 1, dtype=jnp.float32) / n)


def kernel(x, norm_g, final_g, w_in_even, w_out_even, rpb_b, w_in_odd, w_out_odd, sinks_c, qnorm_d, knorm_d):
    b, s, d = x.shape
    bf16 = jnp.bfloat16
    slopes = _alibi_slopes(N_HEADS)
    cos, sin = _rope_tables(s)
    blk = lambda width: width // HEAD_DIM

    h = x.reshape(b * s, d)
    y = _rms_norm_rows(h, norm_g[0], bf16)
    for layer in range(DEPTH):
        i = layer // 2
        final = layer == DEPTH - 1
        g_next = final_g if final else norm_g[layer + 1]
        if layer % 2 == 0:
            proj = _in_proj(y, w_in_even[i].astype(bf16)).reshape(b, s, -1)
            offs = [blk(MIX_W) * j for j in range(8)]
            mix1 = _dilated_attention(proj, slopes, *offs[:4])
            mix2 = _neighbourhood_attention(proj, _na_bias_table(rpb_b[i], s // GRID_W), *offs[4:])
            w_out = w_out_even[i]
        else:
            proj = _in_proj(y, w_in_odd[i].astype(bf16)).reshape(b, s, -1)
            widths = [MIX_W, KV_W, KV_W, MIX_W] * 2
            offs = [blk(int(sum(widths[:j]))) for j in range(8)]
            mix1 = _windowed_sink_attention(proj, slopes, sinks_c[i], *offs[:4])
            mix2 = _axial_rope_attention(proj, cos, sin, qnorm_d[i], knorm_d[i], *offs[4:])
            w_out = w_out_odd[i]
        res = _out_proj(mix1.reshape(b * s, MIX_W), mix2.reshape(b * s, MIX_W), w_out.astype(bf16),
                        h, g_next, final)
        if final:
            return res.reshape(b, s, d)
        h, y = res
```

```python
import functools

import jax
import jax.numpy as jnp
import numpy as np
from jax import lax
from jax.experimental import pallas as pl
from jax.experimental.pallas import tpu as pltpu

D_MODEL = 2048
DEPTH = 4
HEAD_DIM = 128
GRID_W = 64
EPS = 1e-6
N_HEADS = 8
KV_HEADS = 2
GROUP = N_HEADS // KV_HEADS
MIX_W = N_HEADS * HEAD_DIM
KV_W = KV_HEADS * HEAD_DIM
A_PATTERNS = ((128, 1), (512, 4), (2048, 16))
A_SIDE = 64
NA_ROWS = 8
NA_COLS = 16
NA_ROWS_MAX = 8
C_WINDOW = 128
QBLK = 128
ROPE_THETA = 10000.0
SCALE = HEAD_DIM ** -0.5
LOG2E = 1.4426950408889634
MASKED = -1e30
FAR = 1e9
AHEAD = 4

VMEM_LIMIT = 56 * 1024 * 1024


def _params(*sem, flags=None):
    return pltpu.CompilerParams(dimension_semantics=sem, vmem_limit_bytes=VMEM_LIMIT, flags=flags)


def _silu(x):
    return x * (1.0 / (1.0 + jnp.exp(-x)))


def _dot_nt(a, b):
    return lax.dot_general(a, b, (((1,), (1,)), ((), ())), preferred_element_type=jnp.float32)


def _norm_kernel(h_ref, g_ref, y_ref):
    x = h_ref[...]
    y = x * lax.rsqrt(jnp.mean(x * x, axis=-1, keepdims=True) + EPS)
    y_ref[...] = (y * g_ref[...]).astype(y_ref.dtype)


def _rms_norm_rows(h, g, out_dtype, tm=512):
    m, d = h.shape
    return pl.pallas_call(
        _norm_kernel,
        grid=(m // tm,),
        in_specs=[pl.BlockSpec((tm, d), lambda i: (i, 0)), pl.BlockSpec((1, d), lambda i: (0, 0))],
        out_specs=pl.BlockSpec((tm, d), lambda i: (i, 0)),
        out_shape=jax.ShapeDtypeStruct((m, d), out_dtype),
        compiler_params=_params("parallel"),
        name="rms_norm",
    )(h, g.reshape(1, d))


def _in_proj_kernel(x_ref, w_ref, o_ref, wb_ref):
    @pl.when(pl.program_id(1) == 0)
    def _():
        wb_ref[...] = w_ref[...].astype(wb_ref.dtype)

    o_ref[...] = jnp.dot(x_ref[...], wb_ref[...], preferred_element_type=jnp.float32).astype(o_ref.dtype)


def _in_proj(y, w_stack, layer, tm=512, tn=1024):
    m, k = y.shape
    n = w_stack.shape[2]
    return pl.pallas_call(
        _in_proj_kernel,
        grid=(n // tn, m // tm),
        in_specs=[pl.BlockSpec((tm, k), lambda j, i: (i, 0)),
                  pl.BlockSpec((None, k, tn), lambda j, i: (layer, 0, j))],
        out_specs=pl.BlockSpec((tm, tn), lambda j, i: (i, j)),
        out_shape=jax.ShapeDtypeStruct((m, n), jnp.bfloat16),
        scratch_shapes=[pltpu.VMEM((k, tn), jnp.bfloat16)],
        compiler_params=_params("arbitrary", "arbitrary"),
        name="in_proj",
    )(y, w_stack)


def _out_proj_kernel(m1_ref, m2_ref, w_ref, h_ref, g_ref, *refs, final):
    out_refs, wb_ref = refs[:-1], refs[-1]
    half = m1_ref.shape[1]

    @pl.when(pl.program_id(0) == 0)
    def _():
        wb_ref[...] = w_ref[...].astype(wb_ref.dtype)

    acc = jnp.dot(m1_ref[...], wb_ref[:half, :], preferred_element_type=jnp.float32)
    acc = acc + jnp.dot(m2_ref[...], wb_ref[half:, :], preferred_element_type=jnp.float32)
    h = h_ref[...] + acc
    y = h * lax.rsqrt(jnp.mean(h * h, axis=-1, keepdims=True) + EPS) * g_ref[...]
    if final:
        out_refs[0][...] = y
    else:
        out_refs[0][...] = h
        out_refs[1][...] = y.astype(out_refs[1].dtype)


def _out_proj(mix1, mix2, w_stack, layer, h, g_next, final, tm=256):
    m, d = h.shape
    half = mix1.shape[1]
    row = lambda i: (i, 0)
    if final:
        out_shape = jax.ShapeDtypeStruct((m, d), jnp.float32)
        out_specs = pl.BlockSpec((tm, d), row)
    else:
        out_shape = (jax.ShapeDtypeStruct((m, d), jnp.float32), jax.ShapeDtypeStruct((m, d), jnp.bfloat16))
        out_specs = (pl.BlockSpec((tm, d), row), pl.BlockSpec((tm, d), row))
    return pl.pallas_call(
        functools.partial(_out_proj_kernel, final=final),
        grid=(m // tm,),
        in_specs=[pl.BlockSpec((tm, half), row), pl.BlockSpec((tm, half), row),
                  pl.BlockSpec((None, 2 * half, d), lambda i: (layer, 0, 0), pipeline_mode=pl.Buffered(1)),
                  pl.BlockSpec((tm, d), row), pl.BlockSpec((1, d), lambda i: (0, 0))],
        out_specs=out_specs,
        out_shape=out_shape,
        scratch_shapes=[pltpu.VMEM((2 * half, d), jnp.bfloat16)],
        compiler_params=_params("arbitrary"),
        name="out_proj_final" if final else "out_proj",
    )(mix1, mix2, w_stack, h, g_next.reshape(1, d))


def _dilated_kernel(slopes_ref, q_ref, k_ref, v_ref, g_ref, o_ref,
                    qf, kf, vf, qp, kp, vp, acc_s, m_s, l_s):
    s_len = q_ref.shape[0]
    slope = slopes_ref[pl.program_id(1)]
    qf[...] = q_ref[...].astype(jnp.float32)
    kf[...] = k_ref[...].astype(jnp.float32)
    vf[...] = v_ref[...].astype(jnp.float32)

    tiles = []
    srcs = []
    bias = {}
    for p_idx, (_, dil) in enumerate(A_PATTERNS):
        n = s_len // dil
        kw = min(2 * QBLK, n)
        if dil == 1:
            srcs.append((q_ref, k_ref, v_ref))
        else:
            for c in range(dil):
                rows = pl.ds(c, n, stride=dil)
                qp[p_idx - 1, c * n:(c + 1) * n, :] = qf[rows, :].astype(qp.dtype)
                kp[p_idx - 1, c * n:(c + 1) * n, :] = kf[rows, :].astype(kp.dtype)
                vp[p_idx - 1, c * n:(c + 1) * n, :] = vf[rows, :].astype(vp.dtype)
            srcs.append((qp.at[p_idx - 1], kp.at[p_idx - 1], vp.at[p_idx - 1]))
        rel0 = (lax.broadcasted_iota(jnp.int32, (QBLK, kw), 1)
                - lax.broadcasted_iota(jnp.int32, (QBLK, kw), 0))
        for t in range(s_len // QBLK):
            c, i0 = divmod(t * QBLK, n)
            w0 = min(max(i0 - A_SIDE, 0), n - kw)
            if (p_idx, w0 - i0) not in bias:
                dist = jnp.abs(rel0 + (w0 - i0)).astype(jnp.float32)
                bias[(p_idx, w0 - i0)] = (slope * (dil * LOG2E)) * jnp.where(dist <= float(A_SIDE), dist, FAR)
            start = c + dil * i0
            nat = pl.ds(start, QBLK, stride=dil) if dil > 1 else pl.ds(start, QBLK)
            tiles.append((p_idx, slice(c * n + i0, c * n + i0 + QBLK), slice(c * n + w0, c * n + w0 + kw),
                          w0 - i0, nat))

    def scores(tile):
        p_idx, q_rows, k_rows, _, _ = tile
        q_src, k_src, _ = srcs[p_idx]
        return _dot_nt(q_src[q_rows, :], k_src[k_rows, :])

    pending = [scores(t) for t in tiles[:AHEAD]]
    for idx, (p_idx, _, k_rows, off, nat) in enumerate(tiles):
        if idx + AHEAD < len(tiles):
            pending.append(scores(tiles[idx + AHEAD]))
        v_win = srcs[p_idx][2][k_rows, :]
        logits = pending.pop(0) * (SCALE * LOG2E) - bias[(p_idx, off)]
        m = jnp.max(logits, axis=-1, keepdims=True)
        p = jnp.exp2(logits - m)
        l = jnp.sum(p, axis=-1, keepdims=True)
        acc_s[p_idx, nat, :] = jnp.dot(p.astype(v_win.dtype), v_win, preferred_element_type=jnp.float32)
        m_s[p_idx, nat, :] = jnp.broadcast_to(m, (QBLK, HEAD_DIM))
        l_s[p_idx, nat, :] = jnp.broadcast_to(l, (QBLK, HEAD_DIM))

    m_all = jnp.maximum(jnp.maximum(m_s[0], m_s[1]), m_s[2])
    num = jnp.zeros((s_len, HEAD_DIM), jnp.float32)
    den = jnp.zeros((s_len, HEAD_DIM), jnp.float32)
    for p_idx in range(len(A_PATTERNS)):
        e = jnp.exp2(m_s[p_idx] - m_all)
        num = num + e * acc_s[p_idx]
        den = den + e * l_s[p_idx]
    o_ref[...] = ((num / den) * _silu(g_ref[...].astype(jnp.float32))).astype(o_ref.dtype)


def _dilated_attention(proj, slopes, q_off, k_off, v_off, g_off):
    b, s, _ = proj.shape
    col = lambda off: pl.BlockSpec((None, s, HEAD_DIM), lambda bi, hi: (bi, 0, off + hi))
    n_pat = len(A_PATTERNS)
    return pl.pallas_call(
        _dilated_kernel,
        grid=(b, N_HEADS),
        in_specs=[pl.BlockSpec(memory_space=pltpu.SMEM), col(q_off), col(k_off), col(v_off), col(g_off)],
        out_specs=pl.BlockSpec((None, s, HEAD_DIM), lambda bi, hi: (bi, 0, hi)),
        out_shape=jax.ShapeDtypeStruct((b, s, MIX_W), jnp.bfloat16),
        scratch_shapes=[pltpu.VMEM((s, HEAD_DIM), jnp.float32)] * 3
        + [pltpu.VMEM((n_pat - 1, s, HEAD_DIM), jnp.bfloat16)] * 3
        + [pltpu.VMEM((n_pat, s, HEAD_DIM), jnp.float32)] * 3,
        compiler_params=_params("parallel", "parallel"),
        name="dilated_attn",
    )(slopes, proj, proj, proj, proj)


NA_VARIANTS = 8


def _na_variant_row(v, rows):
    return v if v <= 4 else rows - NA_VARIANTS + v


def _na_row_start(r, rows):
    kh = min(NA_ROWS, rows)
    return min(max(r - kh // 2, 0), rows - kh)


def _na_bias_kernel(rpb_ref, o_ref, pair_ref, *, rows):
    kh = min(NA_ROWS, rows)
    n_drow = 2 * NA_ROWS_MAX - 1
    n_dcol = 2 * NA_COLS - 1
    h = pl.program_id(0)
    shape = (GRID_W, 2 * GRID_W)
    lane = lax.broadcasted_iota(jnp.int32, shape, 1)
    c = lax.broadcasted_iota(jnp.int32, shape, 0)
    kc = lane % GRID_W
    second = lane >= GRID_W
    cs = jnp.clip(c - NA_COLS // 2, 0, GRID_W - NA_COLS)
    valid = (kc >= cs) & (kc < cs + NA_COLS)
    dcol = kc - c + (NA_COLS - 1)
    for d in range(n_drow - 1):
        acc = jnp.zeros(shape, jnp.float32)
        for j in range(n_dcol):
            lo = rpb_ref[(h * n_drow + d) * n_dcol + j]
            hi = rpb_ref[(h * n_drow + d + 1) * n_dcol + j]
            acc = jnp.where(dcol == j, jnp.where(second, hi, lo), acc)
        pair_ref[d] = jnp.where(valid, acc * LOG2E, MASKED)
    for v in range(NA_VARIANTS):
        r = _na_variant_row(v, rows)
        d0 = _na_row_start(r, rows) - r + (NA_ROWS_MAX - 1)
        for i in range(0, kh, 2):
            o_ref[v, :, i * GRID_W:(i + 2) * GRID_W] = pair_ref[d0 + i]


def _na_bias_table(rpb, rows):
    n_heads = rpb.shape[0]
    kh = min(NA_ROWS, rows)
    return pl.pallas_call(
        functools.partial(_na_bias_kernel, rows=rows),
        grid=(n_heads,),
        in_specs=[pl.BlockSpec(memory_space=pltpu.SMEM)],
        out_specs=pl.BlockSpec((None, NA_VARIANTS, GRID_W, kh * GRID_W), lambda hi: (hi, 0, 0, 0)),
        out_shape=jax.ShapeDtypeStruct((n_heads, NA_VARIANTS, GRID_W, kh * GRID_W), jnp.float32),
        scratch_shapes=[pltpu.VMEM((2 * NA_ROWS_MAX - 2, GRID_W, 2 * GRID_W), jnp.float32)],
        compiler_params=_params("parallel"),
        name="na_bias",
    )(rpb.reshape(-1))


def _na_kernel(q_ref, k_ref, v_ref, g_ref, bias_ref, o_ref):
    s_len = q_ref.shape[0]
    rows = s_len // GRID_W
    win = min(NA_ROWS, rows) * GRID_W
    def scores(r):
        k0 = _na_row_start(r, rows) * GRID_W
        return _dot_nt(q_ref[r * GRID_W:(r + 1) * GRID_W, :], k_ref[k0:k0 + win, :])

    pending = [scores(r) for r in range(AHEAD)]
    for r in range(rows):
        if r + AHEAD < rows:
            pending.append(scores(r + AHEAD))
        k0 = _na_row_start(r, rows) * GRID_W
        variant = r if r < 4 else (r - (rows - NA_VARIANTS) if r > rows - 4 else 4)
        sl = slice(r * GRID_W, (r + 1) * GRID_W)
        v_win = v_ref[k0:k0 + win, :]
        logits = pending.pop(0) * (SCALE * LOG2E) + bias_ref[variant]
        m = jnp.max(logits, axis=-1, keepdims=True)
        p = jnp.exp2(logits - m)
        den = jnp.sum(p, axis=-1, keepdims=True)
        o = jnp.dot(p.astype(v_win.dtype), v_win, preferred_element_type=jnp.float32) / den
        o_ref[sl, :] = (o * _silu(g_ref[sl, :].astype(jnp.float32))).astype(o_ref.dtype)


def _neighbourhood_attention(proj, bias, q_off, k_off, v_off, g_off):
    b, s, _ = proj.shape
    col = lambda off: pl.BlockSpec((None, s, HEAD_DIM), lambda hi, bi: (bi, 0, off + hi))
    return pl.pallas_call(
        _na_kernel,
        grid=(N_HEADS, b),
        in_specs=[col(q_off), col(k_off), col(v_off), col(g_off),
                  pl.BlockSpec((None,) + bias.shape[1:], lambda hi, bi: (hi, 0, 0, 0))],
        out_specs=pl.BlockSpec((None, s, HEAD_DIM), lambda hi, bi: (bi, 0, hi)),
        out_shape=jax.ShapeDtypeStruct((b, s, MIX_W), jnp.bfloat16),
        compiler_params=_params("parallel", "parallel"),
        name="neighbourhood_attn",
    )(proj, proj, proj, proj, bias)


def _window_kernel(slopes_ref, sinks_ref, q_ref, k_ref, v_ref, g_ref, o_ref):
    s_len = q_ref.shape[0]
    hk = pl.program_id(1)
    kw = 3 * QBLK
    rel0 = (lax.broadcasted_iota(jnp.int32, (QBLK, kw), 1) - lax.broadcasted_iota(jnp.int32, (QBLK, kw), 0))
    slopes = [slopes_ref[hk * GROUP + gi] for gi in range(GROUP)]
    sinks = [sinks_ref[hk * GROUP + gi] for gi in range(GROUP)]
    sinks = [sink * LOG2E for sink in sinks]
    bias = {}
    tiles = []
    for n in range(s_len // QBLK):
        q0 = n * QBLK
        w0 = min(max(q0 - QBLK, 0), s_len - kw)
        if (w0 - q0, 0) not in bias:
            dist = jnp.abs(rel0 + (w0 - q0)).astype(jnp.float32)
            dist = jnp.where(dist <= float(C_WINDOW), dist, FAR)
            for gi in range(GROUP):
                bias[(w0 - q0, gi)] = (slopes[gi] * LOG2E) * dist
        tiles += [(slice(q0, q0 + QBLK), gi, slice(w0, w0 + kw), w0 - q0) for gi in range(GROUP)]

    def scores(tile):
        q_rows, gi, k_rows, _ = tile
        return _dot_nt(q_ref[q_rows, gi * HEAD_DIM:(gi + 1) * HEAD_DIM], k_ref[k_rows, :])

    pending = [scores(t) for t in tiles[:AHEAD]]
    for idx, (q_rows, gi, k_rows, off) in enumerate(tiles):
        if idx + AHEAD < len(tiles):
            pending.append(scores(tiles[idx + AHEAD]))
        lanes = slice(gi * HEAD_DIM, (gi + 1) * HEAD_DIM)
        v_win = v_ref[k_rows, :]
        logits = pending.pop(0) * (SCALE * LOG2E) - bias[(off, gi)]
        m = jnp.maximum(jnp.max(logits, axis=-1, keepdims=True), sinks[gi])
        p = jnp.exp2(logits - m)
        den = jnp.sum(p, axis=-1, keepdims=True) + jnp.exp2(sinks[gi] - m)
        o = jnp.dot(p.astype(v_win.dtype), v_win, preferred_element_type=jnp.float32) / den
        gate = _silu(g_ref[q_rows, lanes].astype(jnp.float32))
        o_ref[q_rows, lanes] = (o * gate).astype(o_ref.dtype)


def _gqa_specs(s, q_off, k_off, v_off, g_off):
    gw = GROUP * HEAD_DIM
    wide = lambda off: pl.BlockSpec((None, s, gw), lambda bi, hi: (bi, 0, off // GROUP + hi))
    col = lambda off: pl.BlockSpec((None, s, HEAD_DIM), lambda bi, hi: (bi, 0, off + hi))
    in_specs = [wide(q_off), col(k_off), col(v_off), wide(g_off)]
    out_spec = pl.BlockSpec((None, s, gw), lambda bi, hi: (bi, 0, hi))
    return in_specs, out_spec


def _windowed_sink_attention(proj, slopes, sinks, q_off, k_off, v_off, g_off):
    b, s, _ = proj.shape
    in_specs, out_spec = _gqa_specs(s, q_off, k_off, v_off, g_off)
    smem = pl.BlockSpec(memory_space=pltpu.SMEM)
    return pl.pallas_call(
        _window_kernel,
        grid=(b, KV_HEADS),
        in_specs=[smem, smem] + in_specs,
        out_specs=out_spec,
        out_shape=jax.ShapeDtypeStruct((b, s, MIX_W), jnp.bfloat16),
        compiler_params=_params("parallel", "parallel"),
        name="window_sink_attn",
    )(slopes, sinks, proj, proj, proj, proj)


def _rope_tables(s):
    t = np.arange(s)
    quarter = HEAD_DIM // 4
    inv = jnp.asarray(ROPE_THETA, jnp.float32) ** (-jnp.arange(quarter, dtype=jnp.float32) / quarter)
    ang_r = jnp.asarray(t // GRID_W, jnp.float32)[:, None] * inv[None, :]
    ang_c = jnp.asarray(t % GRID_W, jnp.float32)[:, None] * inv[None, :]
    cos = jnp.concatenate([jnp.cos(ang_r), jnp.cos(ang_r), jnp.cos(ang_c), jnp.cos(ang_c)], axis=-1)
    sin = jnp.concatenate([-jnp.sin(ang_r), jnp.sin(ang_r), -jnp.sin(ang_c), jnp.sin(ang_c)], axis=-1)
    return cos, sin


def _norm_rope(x, gain, cos, sin):
    y = x * lax.rsqrt(jnp.mean(x * x, axis=-1, keepdims=True) + EPS) * gain
    quarter = HEAD_DIM // 4
    lane = lax.broadcasted_iota(jnp.int32, y.shape, 1)
    partner = jnp.where((lane // quarter) % 2 == 0,
                        pltpu.roll(y, HEAD_DIM - quarter, 1),
                        pltpu.roll(y, quarter, 1))
    return y * cos + partner * sin


D_ROWS_PER_DOT = 256


def _lane_tile_reduce(x, op):
    out = x[:, :HEAD_DIM]
    for j in range(1, x.shape[1] // HEAD_DIM):
        out = op(out, x[:, j * HEAD_DIM:(j + 1) * HEAD_DIM])
    return out


def _axial_kernel(q_ref, k_ref, v_ref, g_ref, cos_ref, sin_ref, gq_ref, gk_ref, o_ref,
                  kp, q4, s_buf, m_buf, p_buf, l_buf):
    s_len = q_ref.shape[0]
    n_blk = s_len // QBLK
    row_groups = [slice(r, r + D_ROWS_PER_DOT) for r in range(0, GROUP * QBLK, D_ROWS_PER_DOT)]
    kp[...] = _norm_rope(k_ref[...].astype(jnp.float32), gk_ref[...], cos_ref[...], sin_ref[...]).astype(kp.dtype)

    def block_rows(n):
        return pl.ds(pl.multiple_of(n * QBLK, QBLK), QBLK)

    def step(slot, n_prep, n_scores, n_probs, n_out):
        other = 1 - slot
        if n_prep is not None:
            rows = block_rows(n_prep)
            cos = cos_ref[rows, :]
            sin = sin_ref[rows, :]
            for gi in range(GROUP):
                q = _norm_rope(q_ref[rows, gi * HEAD_DIM:(gi + 1) * HEAD_DIM].astype(jnp.float32), gq_ref[...],
                               cos, sin)
                q4[other, gi * QBLK:(gi + 1) * QBLK, :] = (q * (SCALE * LOG2E)).astype(q4.dtype)
        if n_scores is not None:
            for hs in row_groups:
                s = _dot_nt(q4[slot, hs, :], kp[...])
                s_buf[slot, hs, :] = s
                m_buf[slot, hs, :] = jnp.broadcast_to(jnp.max(s, axis=-1, keepdims=True), (D_ROWS_PER_DOT, HEAD_DIM))
        if n_probs is not None:
            for hs in row_groups:
                p = jnp.exp2(s_buf[other, hs, :] - jnp.tile(m_buf[other, hs, :], (1, s_len // HEAD_DIM)))
                l_buf[other, hs, :] = _lane_tile_reduce(p, jnp.add)
                p_buf[other, hs, :] = p.astype(p_buf.dtype)
        if n_out is not None:
            rows = block_rows(n_out)
            for hs in row_groups:
                acc = jnp.dot(p_buf[slot, hs, :], v_ref[...], preferred_element_type=jnp.float32)
                o = acc / jnp.sum(l_buf[slot, hs, :], axis=-1, keepdims=True)
                for gi in range(hs.start // QBLK, hs.stop // QBLK):
                    lanes = slice(gi * HEAD_DIM, (gi + 1) * HEAD_DIM)
                    gate = _silu(g_ref[rows, lanes].astype(jnp.float32))
                    o_ref[rows, lanes] = (o[gi * QBLK - hs.start:(gi + 1) * QBLK - hs.start, :] * gate).astype(o_ref.dtype)

    step(1, 0, None, None, None)
    step(0, 1, 0, None, None)
    step(1, 2, 1, 0, None)

    def pair(i, carry):
        n = 2 * i
        step(0, n + 3, n + 2, n + 1, n)
        step(1, jnp.minimum(n + 4, n_blk - 1), n + 3, n + 2, n + 1)
        return carry

    lax.fori_loop(0, n_blk // 2 - 1, pair, 0)
    step(0, None, None, n_blk - 1, n_blk - 2)
    step(1, None, None, None, n_blk - 1)


def _axial_rope_attention(proj, cos, sin, gq, gk, q_off, k_off, v_off, g_off):
    b, s, _ = proj.shape
    in_specs, out_spec = _gqa_specs(s, q_off, k_off, v_off, g_off)
    table = pl.BlockSpec((s, HEAD_DIM), lambda bi, hi: (0, 0))
    gain = pl.BlockSpec((1, HEAD_DIM), lambda bi, hi: (0, 0))
    stacked = GROUP * QBLK
    return pl.pallas_call(
        _axial_kernel,
        grid=(b, KV_HEADS),
        in_specs=in_specs + [table, table, gain, gain],
        out_specs=out_spec,
        out_shape=jax.ShapeDtypeStruct((b, s, MIX_W), jnp.bfloat16),
        scratch_shapes=[pltpu.VMEM((s, HEAD_DIM), jnp.bfloat16),
                        pltpu.VMEM((2, stacked, HEAD_DIM), jnp.bfloat16),
                        pltpu.VMEM((2, stacked, s), jnp.float32),
                        pltpu.VMEM((2, stacked, HEAD_DIM), jnp.float32),
                        pltpu.VMEM((2, stacked, s), jnp.bfloat16),
                        pltpu.VMEM((2, stacked, HEAD_DIM), jnp.float32)],
        compiler_params=_params("parallel", "parallel"),
        name="axial_rope_attn",
    )(proj, proj, proj, proj, cos, sin, gq.reshape(1, HEAD_DIM), gk.reshape(1, HEAD_DIM))


def _alibi_slopes(n):
    return 2.0 ** (-8.0 * jnp.arange(1, n + 1, dtype=jnp.float32) / n)


def kernel(x, norm_g, final_g, w_in_even, w_out_even, rpb_b, w_in_odd, w_out_odd, sinks_c, qnorm_d, knorm_d):
    b, s, d = x.shape
    bf16 = jnp.bfloat16
    slopes = _alibi_slopes(N_HEADS)
    cos, sin = _rope_tables(s)
    blk = lambda width: width // HEAD_DIM

    h = x.reshape(b * s, d)
    y = _rms_norm_rows(h, norm_g[0], bf16)
    for layer in range(DEPTH):
        i = layer // 2
        final = layer == DEPTH - 1
        g_next = final_g if final else norm_g[layer + 1]
        if layer % 2 == 0:
            proj = _in_proj(y, w_in_even, i).reshape(b, s, -1)
            offs = [blk(MIX_W) * j for j in range(8)]
            mix1 = _dilated_attention(proj, slopes, *offs[:4])
            mix2 = _neighbourhood_attention(proj, _na_bias_table(rpb_b[i], s // GRID_W), *offs[4:])
            w_out = w_out_even
        else:
            proj = _in_proj(y, w_in_odd, i).reshape(b, s, -1)
            widths = [MIX_W, KV_W, KV_W, MIX_W] * 2
            offs = [blk(int(sum(widths[:j]))) for j in range(8)]
            mix1 = _windowed_sink_attention(proj, slopes, sinks_c[i], *offs[:4])
            mix2 = _axial_rope_attention(proj, cos, sin, qnorm_d[i], knorm_d[i], *offs[4:])
            w_out = w_out_odd
        res = _out_proj(mix1.reshape(b * s, MIX_W), mix2.reshape(b * s, MIX_W), w_out, i, h, g_next, final)
        if final:
            return res.reshape(b, s, d)
        h, y = res
```

```python
import functools

import jax
import jax.numpy as jnp
import numpy as np
from jax import lax
from jax.experimental import pallas as pl
from jax.experimental.pallas import tpu as pltpu

D_MODEL = 2048
DEPTH = 4
HEAD_DIM = 128
GRID_W = 64
EPS = 1e-6
N_HEADS = 8
KV_HEADS = 2
GROUP = N_HEADS // KV_HEADS
MIX_W = N_HEADS * HEAD_DIM
KV_W = KV_HEADS * HEAD_DIM
A_PATTERNS = ((128, 1), (512, 4), (2048, 16))
A_SIDE = 64
NA_ROWS = 8
NA_COLS = 16
NA_ROWS_MAX = 8
C_WINDOW = 128
QBLK = 128
ROPE_THETA = 10000.0
SCALE = HEAD_DIM ** -0.5
LOG2E = 1.4426950408889634
MASKED = -1e30
FAR = 1e9
AHEAD = 4

VMEM_LIMIT = 56 * 1024 * 1024
IN_PROJ_ROWS = 1024
IN_PROJ_COLS = 1024
OUT_PROJ_ROWS = 512


def _params(*sem, flags=None):
    return pltpu.CompilerParams(dimension_semantics=sem, vmem_limit_bytes=VMEM_LIMIT, flags=flags)


def _silu(x):
    return x * (1.0 / (1.0 + jnp.exp(-x)))


def _dot_nt(a, b):
    return lax.dot_general(a, b, (((1,), (1,)), ((), ())), preferred_element_type=jnp.float32)


def _norm_kernel(h_ref, g_ref, y_ref):
    x = h_ref[...]
    y = x * lax.rsqrt(jnp.mean(x * x, axis=-1, keepdims=True) + EPS)
    y_ref[...] = (y * g_ref[...]).astype(y_ref.dtype)


def _rms_norm_rows(h, g, out_dtype, tm=512):
    m, d = h.shape
    return pl.pallas_call(
        _norm_kernel,
        grid=(m // tm,),
        in_specs=[pl.BlockSpec((tm, d), lambda i: (i, 0)), pl.BlockSpec((1, d), lambda i: (0, 0))],
        out_specs=pl.BlockSpec((tm, d), lambda i: (i, 0)),
        out_shape=jax.ShapeDtypeStruct((m, d), out_dtype),
        compiler_params=_params("parallel"),
        name="rms_norm",
    )(h, g.reshape(1, d))


def _in_proj_kernel(x_ref, w_ref, o_ref, wb_ref):
    @pl.when(pl.program_id(1) == 0)
    def _():
        wb_ref[...] = w_ref[...].astype(wb_ref.dtype)

    o_ref[...] = jnp.dot(x_ref[...], wb_ref[...], preferred_element_type=jnp.float32).astype(o_ref.dtype)


def _in_proj(y, w_stack, layer, tm=IN_PROJ_ROWS, tn=IN_PROJ_COLS):
    m, k = y.shape
    n = w_stack.shape[2]
    return pl.pallas_call(
        _in_proj_kernel,
        grid=(n // tn, m // tm),
        in_specs=[pl.BlockSpec((tm, k), lambda j, i: (i, 0)),
                  pl.BlockSpec((None, k, tn), lambda j, i: (layer, 0, j))],
        out_specs=pl.BlockSpec((tm, tn), lambda j, i: (i, j)),
        out_shape=jax.ShapeDtypeStruct((m, n), jnp.bfloat16),
        scratch_shapes=[pltpu.VMEM((k, tn), jnp.bfloat16)],
        compiler_params=_params("arbitrary", "arbitrary"),
        name="in_proj",
    )(y, w_stack)


def _out_proj_kernel(m1_ref, m2_ref, w_ref, h_ref, g_ref, *refs, final):
    out_refs, wb_ref = refs[:-1], refs[-1]
    half = m1_ref.shape[1]

    @pl.when(pl.program_id(0) == 0)
    def _():
        wb_ref[...] = w_ref[...].astype(wb_ref.dtype)

    acc = jnp.dot(m1_ref[...], wb_ref[:half, :], preferred_element_type=jnp.float32)
    acc = acc + jnp.dot(m2_ref[...], wb_ref[half:, :], preferred_element_type=jnp.float32)
    h = h_ref[...] + acc
    y = h * lax.rsqrt(jnp.mean(h * h, axis=-1, keepdims=True) + EPS) * g_ref[...]
    if final:
        out_refs[0][...] = y
    else:
        out_refs[0][...] = h
        out_refs[1][...] = y.astype(out_refs[1].dtype)


def _out_proj(mix1, mix2, w_stack, layer, h, g_next, final, tm=OUT_PROJ_ROWS):
    m, d = h.shape
    half = mix1.shape[1]
    row = lambda i: (i, 0)
    if final:
        out_shape = jax.ShapeDtypeStruct((m, d), jnp.float32)
        out_specs = pl.BlockSpec((tm, d), row)
    else:
        out_shape = (jax.ShapeDtypeStruct((m, d), jnp.float32), jax.ShapeDtypeStruct((m, d), jnp.bfloat16))
        out_specs = (pl.BlockSpec((tm, d), row), pl.BlockSpec((tm, d), row))
    return pl.pallas_call(
        functools.partial(_out_proj_kernel, final=final),
        grid=(m // tm,),
        in_specs=[pl.BlockSpec((tm, half), row), pl.BlockSpec((tm, half), row),
                  pl.BlockSpec((None, 2 * half, d), lambda i: (layer, 0, 0), pipeline_mode=pl.Buffered(1)),
                  pl.BlockSpec((tm, d), row), pl.BlockSpec((1, d), lambda i: (0, 0))],
        out_specs=out_specs,
        out_shape=out_shape,
        scratch_shapes=[pltpu.VMEM((2 * half, d), jnp.bfloat16)],
        compiler_params=_params("arbitrary"),
        name="out_proj_final" if final else "out_proj",
    )(mix1, mix2, w_stack, h, g_next.reshape(1, d))


A_REGROUP = 4


def _dilated_kernel(slopes_ref, q_ref, k_ref, v_ref, g_ref, o_ref, xf, x4, qp, kp, vp, o_s, lse_s, o_nat):
    assert tuple(d for _, d in A_PATTERNS) == (1, A_REGROUP, A_REGROUP ** 2)
    s_len = q_ref.shape[0]
    n4 = s_len // A_REGROUP
    n16 = n4 // A_REGROUP
    slope = slopes_ref[pl.program_id(1)]

    for a, (src, dst) in enumerate(((q_ref, qp), (k_ref, kp), (v_ref, vp))):
        xf[a] = src[...].astype(jnp.float32)
        for c4 in range(A_REGROUP):
            x = xf[a, pl.ds(c4, n4, stride=A_REGROUP), :]
            x4[a, c4 * n4:(c4 + 1) * n4, :] = x
            dst[0, c4 * n4:(c4 + 1) * n4, :] = x.astype(dst.dtype)
        for c4 in range(A_REGROUP):
            for c2 in range(A_REGROUP):
                c16 = A_REGROUP * c2 + c4
                x = x4[a, pl.ds(c4 * n4 + c2, n16, stride=A_REGROUP), :]
                dst[1, c16 * n16:(c16 + 1) * n16, :] = x.astype(dst.dtype)
    srcs = [(q_ref, k_ref, v_ref), (qp.at[0], kp.at[0], vp.at[0]), (qp.at[1], kp.at[1], vp.at[1])]

    tiles = []
    bias = {}
    for p_idx, (_, dil) in enumerate(A_PATTERNS):
        n = s_len // dil
        kw = min(2 * QBLK, n)
        rel0 = (lax.broadcasted_iota(jnp.int32, (QBLK, kw), 1)
                - lax.broadcasted_iota(jnp.int32, (QBLK, kw), 0))
        for t in range(s_len // QBLK):
            c, i0 = divmod(t * QBLK, n)
            w0 = min(max(i0 - A_SIDE, 0), n - kw)
            if (p_idx, w0 - i0) not in bias:
                dist = jnp.abs(rel0 + (w0 - i0)).astype(jnp.float32)
                bias[(p_idx, w0 - i0)] = (slope * (dil * LOG2E)) * jnp.where(dist <= float(A_SIDE), dist, FAR)
            if p_idx < 2:
                out_rows = pl.ds(c * n + i0, QBLK)
            else:
                c2, c4 = divmod(c, A_REGROUP)
                out_rows = pl.ds(c4 * n4 + c2 + A_REGROUP * i0, QBLK, stride=A_REGROUP)
            tiles.append((p_idx, slice(c * n + i0, c * n + i0 + QBLK), slice(c * n + w0, c * n + w0 + kw),
                          w0 - i0, out_rows))

    def scores(tile):
        p_idx, q_rows, k_rows, _, _ = tile
        q_src, k_src, _ = srcs[p_idx]
        return _dot_nt(q_src[q_rows, :], k_src[k_rows, :])

    pending = [scores(t) for t in tiles[:AHEAD]]
    for idx, (p_idx, _, k_rows, off, out_rows) in enumerate(tiles):
        if idx + AHEAD < len(tiles):
            pending.append(scores(tiles[idx + AHEAD]))
        v_win = srcs[p_idx][2][k_rows, :]
        logits = pending.pop(0) * (SCALE * LOG2E) - bias[(p_idx, off)]
        m = jnp.max(logits, axis=-1, keepdims=True)
        p = jnp.exp2(logits - m)
        l = jnp.sum(p, axis=-1, keepdims=True)
        o_s[p_idx, out_rows, :] = jnp.dot(p.astype(v_win.dtype), v_win, preferred_element_type=jnp.float32) / l
        lse_s[p_idx, out_rows, :] = jnp.broadcast_to(m + jnp.log2(l), (QBLK, HEAD_DIM))

    for c4 in range(A_REGROUP):
        blk = slice(c4 * n4, (c4 + 1) * n4)
        nat = pl.ds(c4, n4, stride=A_REGROUP)
        lses = (lse_s[0, nat, :], lse_s[1, blk, :], lse_s[2, blk, :])
        outs = (o_s[0, nat, :], o_s[1, blk, :], o_s[2, blk, :])
        top = jnp.maximum(jnp.maximum(lses[0], lses[1]), lses[2])
        ws = [jnp.exp2(ls - top) for ls in lses]
        num = ws[0] * outs[0] + ws[1] * outs[1] + ws[2] * outs[2]
        o_nat[nat, :] = num / (ws[0] + ws[1] + ws[2])
    o_ref[...] = (o_nat[...] * _silu(g_ref[...].astype(jnp.float32))).astype(o_ref.dtype)


def _dilated_attention(proj, slopes, q_off, k_off, v_off, g_off):
    b, s, _ = proj.shape
    col = lambda off: pl.BlockSpec((None, s, HEAD_DIM), lambda bi, hi: (bi, 0, off + hi))
    n_pat = len(A_PATTERNS)
    f32_rows = lambda n: pltpu.VMEM((n, s, HEAD_DIM), jnp.float32)
    return pl.pallas_call(
        _dilated_kernel,
        grid=(b, N_HEADS),
        in_specs=[pl.BlockSpec(memory_space=pltpu.SMEM), col(q_off), col(k_off), col(v_off), col(g_off)],
        out_specs=pl.BlockSpec((None, s, HEAD_DIM), lambda bi, hi: (bi, 0, hi)),
        out_shape=jax.ShapeDtypeStruct((b, s, MIX_W), jnp.bfloat16),
        scratch_shapes=[f32_rows(3), f32_rows(3)]
        + [pltpu.VMEM((n_pat - 1, s, HEAD_DIM), jnp.bfloat16)] * 3
        + [f32_rows(n_pat), f32_rows(n_pat), pltpu.VMEM((s, HEAD_DIM), jnp.float32)],
        compiler_params=_params("parallel", "parallel"),
        name="dilated_attn",
    )(slopes, proj, proj, proj, proj)


NA_VARIANTS = 8


def _na_variant_row(v, rows):
    return v if v <= 4 else rows - NA_VARIANTS + v


def _na_row_start(r, rows):
    kh = min(NA_ROWS, rows)
    return min(max(r - kh // 2, 0), rows - kh)


def _na_bias_kernel(rpb_ref, o_ref, pair_ref, *, rows):
    kh = min(NA_ROWS, rows)
    n_drow = 2 * NA_ROWS_MAX - 1
    n_dcol = 2 * NA_COLS - 1
    h = pl.program_id(0)
    shape = (GRID_W, 2 * GRID_W)
    lane = lax.broadcasted_iota(jnp.int32, shape, 1)
    c = lax.broadcasted_iota(jnp.int32, shape, 0)
    kc = lane % GRID_W
    second = lane >= GRID_W
    cs = jnp.clip(c - NA_COLS // 2, 0, GRID_W - NA_COLS)
    valid = (kc >= cs) & (kc < cs + NA_COLS)
    dcol = kc - c + (NA_COLS - 1)
    for d in range(n_drow - 1):
        acc = jnp.zeros(shape, jnp.float32)
        for j in range(n_dcol):
            lo = rpb_ref[(h * n_drow + d) * n_dcol + j]
            hi = rpb_ref[(h * n_drow + d + 1) * n_dcol + j]
            acc = jnp.where(dcol == j, jnp.where(second, hi, lo), acc)
        pair_ref[d] = jnp.where(valid, acc * LOG2E, MASKED)
    for v in range(NA_VARIANTS):
        r = _na_variant_row(v, rows)
        d0 = _na_row_start(r, rows) - r + (NA_ROWS_MAX - 1)
        for i in range(0, kh, 2):
            o_ref[v, :, i * GRID_W:(i + 2) * GRID_W] = pair_ref[d0 + i]


def _na_bias_table(rpb, rows):
    n_heads = rpb.shape[0]
    kh = min(NA_ROWS, rows)
    return pl.pallas_call(
        functools.partial(_na_bias_kernel, rows=rows),
        grid=(n_heads,),
        in_specs=[pl.BlockSpec(memory_space=pltpu.SMEM)],
        out_specs=pl.BlockSpec((None, NA_VARIANTS, GRID_W, kh * GRID_W), lambda hi: (hi, 0, 0, 0)),
        out_shape=jax.ShapeDtypeStruct((n_heads, NA_VARIANTS, GRID_W, kh * GRID_W), jnp.float32),
        scratch_shapes=[pltpu.VMEM((2 * NA_ROWS_MAX - 2, GRID_W, 2 * GRID_W), jnp.float32)],
        compiler_params=_params("parallel"),
        name="na_bias",
    )(rpb.reshape(-1))


def _na_kernel(q_ref, k_ref, v_ref, g_ref, bias_ref, o_ref):
    s_len = q_ref.shape[0]
    rows = s_len // GRID_W
    win = min(NA_ROWS, rows) * GRID_W
    def scores(r):
        k0 = _na_row_start(r, rows) * GRID_W
        return _dot_nt(q_ref[r * GRID_W:(r + 1) * GRID_W, :], k_ref[k0:k0 + win, :])

    pending = [scores(r) for r in range(AHEAD)]
    for r in range(rows):
        if r + AHEAD < rows:
            pending.append(scores(r + AHEAD))
        k0 = _na_row_start(r, rows) * GRID_W
        variant = r if r < 4 else (r - (rows - NA_VARIANTS) if r > rows - 4 else 4)
        sl = slice(r * GRID_W, (r + 1) * GRID_W)
        v_win = v_ref[k0:k0 + win, :]
        logits = pending.pop(0) * (SCALE * LOG2E) + bias_ref[variant]
        m = jnp.max(logits, axis=-1, keepdims=True)
        p = jnp.exp2(logits - m)
        den = jnp.sum(p, axis=-1, keepdims=True)
        o = jnp.dot(p.astype(v_win.dtype), v_win, preferred_element_type=jnp.float32) / den
        o_ref[sl, :] = (o * _silu(g_ref[sl, :].astype(jnp.float32))).astype(o_ref.dtype)


def _neighbourhood_attention(proj, bias, q_off, k_off, v_off, g_off):
    b, s, _ = proj.shape
    col = lambda off: pl.BlockSpec((None, s, HEAD_DIM), lambda hi, bi: (bi, 0, off + hi))
    return pl.pallas_call(
        _na_kernel,
        grid=(N_HEADS, b),
        in_specs=[col(q_off), col(k_off), col(v_off), col(g_off),
                  pl.BlockSpec((None,) + bias.shape[1:], lambda hi, bi: (hi, 0, 0, 0))],
        out_specs=pl.BlockSpec((None, s, HEAD_DIM), lambda hi, bi: (bi, 0, hi)),
        out_shape=jax.ShapeDtypeStruct((b, s, MIX_W), jnp.bfloat16),
        compiler_params=_params("parallel", "parallel"),
        name="neighbourhood_attn",
    )(proj, proj, proj, proj, bias)


def _window_kernel(slopes_ref, sinks_ref, q_ref, k_ref, v_ref, g_ref, o_ref):
    s_len = q_ref.shape[0]
    hk = pl.program_id(1)
    kw = 3 * QBLK
    rel0 = (lax.broadcasted_iota(jnp.int32, (QBLK, kw), 1) - lax.broadcasted_iota(jnp.int32, (QBLK, kw), 0))
    slopes = [slopes_ref[hk * GROUP + gi] for gi in range(GROUP)]
    sinks = [sinks_ref[hk * GROUP + gi] for gi in range(GROUP)]
    sinks = [sink * LOG2E for sink in sinks]
    bias = {}
    tiles = []
    for n in range(s_len // QBLK):
        q0 = n * QBLK
        w0 = min(max(q0 - QBLK, 0), s_len - kw)
        if (w0 - q0, 0) not in bias:
            dist = jnp.abs(rel0 + (w0 - q0)).astype(jnp.float32)
            dist = jnp.where(dist <= float(C_WINDOW), dist, FAR)
            for gi in range(GROUP):
                bias[(w0 - q0, gi)] = (slopes[gi] * LOG2E) * dist
        tiles += [(slice(q0, q0 + QBLK), gi, slice(w0, w0 + kw), w0 - q0) for gi in range(GROUP)]

    def scores(tile):
        q_rows, gi, k_rows, _ = tile
        return _dot_nt(q_ref[q_rows, gi * HEAD_DIM:(gi + 1) * HEAD_DIM], k_ref[k_rows, :])

    pending = [scores(t) for t in tiles[:AHEAD]]
    for idx, (q_rows, gi, k_rows, off) in enumerate(tiles):
        if idx + AHEAD < len(tiles):
            pending.append(scores(tiles[idx + AHEAD]))
        lanes = slice(gi * HEAD_DIM, (gi + 1) * HEAD_DIM)
        v_win = v_ref[k_rows, :]
        logits = pending.pop(0) * (SCALE * LOG2E) - bias[(off, gi)]
        m = jnp.maximum(jnp.max(logits, axis=-1, keepdims=True), sinks[gi])
        p = jnp.exp2(logits - m)
        den = jnp.sum(p, axis=-1, keepdims=True) + jnp.exp2(sinks[gi] - m)
        o = jnp.dot(p.astype(v_win.dtype), v_win, preferred_element_type=jnp.float32) / den
        gate = _silu(g_ref[q_rows, lanes].astype(jnp.float32))
        o_ref[q_rows, lanes] = (o * gate).astype(o_ref.dtype)


def _gqa_specs(s, q_off, k_off, v_off, g_off):
    gw = GROUP * HEAD_DIM
    wide = lambda off: pl.BlockSpec((None, s, gw), lambda bi, hi: (bi, 0, off // GROUP + hi))
    col = lambda off: pl.BlockSpec((None, s, HEAD_DIM), lambda bi, hi: (bi, 0, off + hi))
    in_specs = [wide(q_off), col(k_off), col(v_off), wide(g_off)]
    out_spec = pl.BlockSpec((None, s, gw), lambda bi, hi: (bi, 0, hi))
    return in_specs, out_spec


def _windowed_sink_attention(proj, slopes, sinks, q_off, k_off, v_off, g_off):
    b, s, _ = proj.shape
    in_specs, out_spec = _gqa_specs(s, q_off, k_off, v_off, g_off)
    smem = pl.BlockSpec(memory_space=pltpu.SMEM)
    return pl.pallas_call(
        _window_kernel,
        grid=(b, KV_HEADS),
        in_specs=[smem, smem] + in_specs,
        out_specs=out_spec,
        out_shape=jax.ShapeDtypeStruct((b, s, MIX_W), jnp.bfloat16),
        compiler_params=_params("parallel", "parallel"),
        name="window_sink_attn",
    )(slopes, sinks, proj, proj, proj, proj)


def _rope_tables(s):
    t = np.arange(s)
    quarter = HEAD_DIM // 4
    inv = jnp.asarray(ROPE_THETA, jnp.float32) ** (-jnp.arange(quarter, dtype=jnp.float32) / quarter)
    ang_r = jnp.asarray(t // GRID_W, jnp.float32)[:, None] * inv[None, :]
    ang_c = jnp.asarray(t % GRID_W, jnp.float32)[:, None] * inv[None, :]
    cos = jnp.concatenate([jnp.cos(ang_r), jnp.cos(ang_r), jnp.cos(ang_c), jnp.cos(ang_c)], axis=-1)
    sin = jnp.concatenate([-jnp.sin(ang_r), jnp.sin(ang_r), -jnp.sin(ang_c), jnp.sin(ang_c)], axis=-1)
    return cos, sin


def _norm_rope(x, gain, cos, sin):
    y = x * lax.rsqrt(jnp.mean(x * x, axis=-1, keepdims=True) + EPS) * gain
    quarter = HEAD_DIM // 4
    lane = lax.broadcasted_iota(jnp.int32, y.shape, 1)
    partner = jnp.where((lane // quarter) % 2 == 0,
                        pltpu.roll(y, HEAD_DIM - quarter, 1),
                        pltpu.roll(y, quarter, 1))
    return y * cos + partner * sin


D_ROWS_PER_DOT = 256


def _lane_tile_reduce(x, op):
    out = x[:, :HEAD_DIM]
    for j in range(1, x.shape[1] // HEAD_DIM):
        out = op(out, x[:, j * HEAD_DIM:(j + 1) * HEAD_DIM])
    return out


def _axial_kernel(q_ref, k_ref, v_ref, g_ref, cos_ref, sin_ref, gq_ref, gk_ref, o_ref,
                  kp, q4, s_buf, m_buf, p_buf, l_buf):
    s_len = q_ref.shape[0]
    n_blk = s_len // QBLK
    row_groups = [slice(r, r + D_ROWS_PER_DOT) for r in range(0, GROUP * QBLK, D_ROWS_PER_DOT)]
    kp[...] = _norm_rope(k_ref[...].astype(jnp.float32), gk_ref[...], cos_ref[...], sin_ref[...]).astype(kp.dtype)

    def block_rows(n):
        return pl.ds(pl.multiple_of(n * QBLK, QBLK), QBLK)

    def step(slot, n_prep, n_scores, n_probs, n_out):
        other = 1 - slot
        if n_prep is not None:
            rows = block_rows(n_prep)
            cos = cos_ref[rows, :]
            sin = sin_ref[rows, :]
            for gi in range(GROUP):
                q = _norm_rope(q_ref[rows, gi * HEAD_DIM:(gi + 1) * HEAD_DIM].astype(jnp.float32), gq_ref[...],
                               cos, sin)
                q4[other, gi * QBLK:(gi + 1) * QBLK, :] = (q * (SCALE * LOG2E)).astype(q4.dtype)
        if n_scores is not None:
            for hs in row_groups:
                s = _dot_nt(q4[slot, hs, :], kp[...])
                s_buf[slot, hs, :] = s
                m_buf[slot, hs, :] = jnp.broadcast_to(jnp.max(s, axis=-1, keepdims=True), (D_ROWS_PER_DOT, HEAD_DIM))
        if n_probs is not None:
            for hs in row_groups:
                p = jnp.exp2(s_buf[other, hs, :] - jnp.tile(m_buf[other, hs, :], (1, s_len // HEAD_DIM)))
                l_buf[other, hs, :] = _lane_tile_reduce(p, jnp.add)
                p_buf[other, hs, :] = p.astype(p_buf.dtype)
        if n_out is not None:
            rows = block_rows(n_out)
            for hs in row_groups:
                acc = jnp.dot(p_buf[slot, hs, :], v_ref[...], preferred_element_type=jnp.float32)
                o = acc / jnp.sum(l_buf[slot, hs, :], axis=-1, keepdims=True)
                for gi in range(hs.start // QBLK, hs.stop // QBLK):
                    lanes = slice(gi * HEAD_DIM, (gi + 1) * HEAD_DIM)
                    gate = _silu(g_ref[rows, lanes].astype(jnp.float32))
                    o_ref[rows, lanes] = (o[gi * QBLK - hs.start:(gi + 1) * QBLK - hs.start, :] * gate).astype(o_ref.dtype)

    step(1, 0, None, None, None)
    step(0, 1, 0, None, None)
    step(1, 2, 1, 0, None)

    def pair(i, carry):
        n = 2 * i
        step(0, n + 3, n + 2, n + 1, n)
        step(1, jnp.minimum(n + 4, n_blk - 1), n + 3, n + 2, n + 1)
        return carry

    lax.fori_loop(0, n_blk // 2 - 1, pair, 0)
    step(0, None, None, n_blk - 1, n_blk - 2)
    step(1, None, None, None, n_blk - 1)


def _axial_rope_attention(proj, cos, sin, gq, gk, q_off, k_off, v_off, g_off):
    b, s, _ = proj.shape
    in_specs, out_spec = _gqa_specs(s, q_off, k_off, v_off, g_off)
    table = pl.BlockSpec((s, HEAD_DIM), lambda bi, hi: (0, 0))
    gain = pl.BlockSpec((1, HEAD_DIM), lambda bi, hi: (0, 0))
    stacked = GROUP * QBLK
    return pl.pallas_call(
        _axial_kernel,
        grid=(b, KV_HEADS),
        in_specs=in_specs + [table, table, gain, gain],
        out_specs=out_spec,
        out_shape=jax.ShapeDtypeStruct((b, s, MIX_W), jnp.bfloat16),
        scratch_shapes=[pltpu.VMEM((s, HEAD_DIM), jnp.bfloat16),
                        pltpu.VMEM((2, stacked, HEAD_DIM), jnp.bfloat16),
                        pltpu.VMEM((2, stacked, s), jnp.float32),
                        pltpu.VMEM((2, stacked, HEAD_DIM), jnp.float32),
                        pltpu.VMEM((2, stacked, s), jnp.bfloat16),
                        pltpu.VMEM((2, stacked, HEAD_DIM), jnp.float32)],
        compiler_params=_params("parallel", "parallel"),
        name="axial_rope_attn",
    )(proj, proj, proj, proj, cos, sin, gq.reshape(1, HEAD_DIM), gk.reshape(1, HEAD_DIM))


def _alibi_slopes(n):
    return 2.0 ** (-8.0 * jnp.arange(1, n + 1, dtype=jnp.float32) / n)


def kernel(x, norm_g, final_g, w_in_even, w_out_even, rpb_b, w_in_odd, w_out_odd, sinks_c, qnorm_d, knorm_d):
    b, s, d = x.shape
    bf16 = jnp.bfloat16
    slopes = _alibi_slopes(N_HEADS)
    cos, sin = _rope_tables(s)
    blk = lambda width: width // HEAD_DIM

    h = x.reshape(b * s, d)
    y = _rms_norm_rows(h, norm_g[0], bf16)
    for layer in range(DEPTH):
        i = layer // 2
        final = layer == DEPTH - 1
        g_next = final_g if final else norm_g[layer + 1]
        if layer % 2 == 0:
            proj = _in_proj(y, w_in_even, i).reshape(b, s, -1)
            offs = [blk(MIX_W) * j for j in range(8)]
            mix1 = _dilated_attention(proj, slopes, *offs[:4])
            mix2 = _neighbourhood_attention(proj, _na_bias_table(rpb_b[i], s // GRID_W), *offs[4:])
            w_out = w_out_even
        else:
            proj = _in_proj(y, w_in_odd, i).reshape(b, s, -1)
            widths = [MIX_W, KV_W, KV_W, MIX_W] * 2
            offs = [blk(int(sum(widths[:j]))) for j in range(8)]
            mix1 = _windowed_sink_attention(proj, slopes, sinks_c[i], *offs[:4])
            mix2 = _axial_rope_attention(proj, cos, sin, qnorm_d[i], knorm_d[i], *offs[4:])
            w_out = w_out_odd
        res = _out_proj(mix1.reshape(b * s, MIX_W), mix2.reshape(b * s, MIX_W), w_out, i, h, g_next, final)
        if final:
            return res.reshape(b, s, d)
        h, y = res
```

```python
import functools

import jax
import jax.numpy as jnp
import numpy as np
from jax import lax
from jax.experimental import pallas as pl
from jax.experimental.pallas import tpu as pltpu

D_MODEL = 2048
DEPTH = 4
HEAD_DIM = 128
GRID_W = 64
EPS = 1e-6
N_HEADS = 8
KV_HEADS = 2
GROUP = N_HEADS // KV_HEADS
MIX_W = N_HEADS * HEAD_DIM
KV_W = KV_HEADS * HEAD_DIM
A_PATTERNS = ((128, 1), (512, 4), (2048, 16))
A_SIDE = 64
NA_ROWS = 8
NA_COLS = 16
NA_ROWS_MAX = 8
C_WINDOW = 128
QBLK = 128
ROPE_THETA = 10000.0
SCALE = HEAD_DIM ** -0.5
LOG2E = 1.4426950408889634
MASKED = -1e30
FAR = 1e9
AHEAD = 6

VMEM_LIMIT = 56 * 1024 * 1024
IN_PROJ_ROWS = 1024
IN_PROJ_COLS = 1024
OUT_PROJ_ROWS = 512


def _params(*sem):
    return pltpu.CompilerParams(dimension_semantics=sem, vmem_limit_bytes=VMEM_LIMIT)


def _silu(x):
    return x * (1.0 / (1.0 + jnp.exp(-x)))


def _dot_nt(a, b):
    return lax.dot_general(a, b, (((1,), (1,)), ((), ())), preferred_element_type=jnp.float32)


def _run_pipelined(items):
    pending = [scores() for scores, _ in items[:AHEAD]]
    for idx, (_, finish) in enumerate(items):
        if idx + AHEAD < len(items):
            pending.append(items[idx + AHEAD][0]())
        finish(pending.pop(0))


def _interleave(a, b):
    keyed = [((i + 0.5) / len(a), 0, x) for i, x in enumerate(a)] + [((i + 0.5) / len(b), 1, x) for i, x in enumerate(b)]
    return [x for _, _, x in sorted(keyed, key=lambda t: t[:2])]


def _norm_kernel(h_ref, g_ref, y_ref):
    x = h_ref[...]
    y = x * lax.rsqrt(jnp.mean(x * x, axis=-1, keepdims=True) + EPS)
    y_ref[...] = (y * g_ref[...]).astype(y_ref.dtype)


def _rms_norm_rows(h, g, out_dtype, tm=512):
    m, d = h.shape
    return pl.pallas_call(
        _norm_kernel,
        grid=(m // tm,),
        in_specs=[pl.BlockSpec((tm, d), lambda i: (i, 0)), pl.BlockSpec((1, d), lambda i: (0, 0))],
        out_specs=pl.BlockSpec((tm, d), lambda i: (i, 0)),
        out_shape=jax.ShapeDtypeStruct((m, d), out_dtype),
        compiler_params=_params("parallel"),
        name="rms_norm",
    )(h, g.reshape(1, d))


def _in_proj_kernel(x_ref, w_ref, scale_ref, o_ref, wb_ref):
    @pl.when(pl.program_id(1) == 0)
    def _():
        wb_ref[...] = (w_ref[...] * scale_ref[...]).astype(wb_ref.dtype)

    o_ref[...] = jnp.dot(x_ref[...], wb_ref[...], preferred_element_type=jnp.float32).astype(o_ref.dtype)


def _in_proj(y, w_stack, layer, col_scale, tm=IN_PROJ_ROWS, tn=IN_PROJ_COLS):
    m, k = y.shape
    n = w_stack.shape[2]
    return pl.pallas_call(
        _in_proj_kernel,
        grid=(n // tn, m // tm),
        in_specs=[pl.BlockSpec((tm, k), lambda j, i: (i, 0)),
                  pl.BlockSpec((None, k, tn), lambda j, i: (layer, 0, j)),
                  pl.BlockSpec((1, tn), lambda j, i: (0, j))],
        out_specs=pl.BlockSpec((tm, tn), lambda j, i: (i, j)),
        out_shape=jax.ShapeDtypeStruct((m, n), jnp.bfloat16),
        scratch_shapes=[pltpu.VMEM((k, tn), jnp.bfloat16)],
        compiler_params=_params("arbitrary", "arbitrary"),
        name="in_proj",
    )(y, w_stack, col_scale.reshape(1, n))


def _out_proj_kernel(m1_ref, m2_ref, w_ref, h_ref, g_ref, *refs, final):
    out_refs, wb_ref = refs[:-1], refs[-1]
    half = m1_ref.shape[1]

    @pl.when(pl.program_id(0) == 0)
    def _():
        wb_ref[...] = w_ref[...].astype(wb_ref.dtype)

    acc = jnp.dot(m1_ref[...], wb_ref[:half, :], preferred_element_type=jnp.float32)
    acc = acc + jnp.dot(m2_ref[...], wb_ref[half:, :], preferred_element_type=jnp.float32)
    h = h_ref[...] + acc
    y = h * lax.rsqrt(jnp.mean(h * h, axis=-1, keepdims=True) + EPS) * g_ref[...]
    if final:
        out_refs[0][...] = y
    else:
        out_refs[0][...] = h
        out_refs[1][...] = y.astype(out_refs[1].dtype)


def _out_proj(mix1, mix2, w_stack, layer, h, g_next, final, tm=OUT_PROJ_ROWS):
    m, d = h.shape
    half = mix1.shape[1]
    row = lambda i: (i, 0)
    if final:
        out_shape = jax.ShapeDtypeStruct((m, d), jnp.float32)
        out_specs = pl.BlockSpec((tm, d), row)
    else:
        out_shape = (jax.ShapeDtypeStruct((m, d), jnp.float32), jax.ShapeDtypeStruct((m, d), jnp.bfloat16))
        out_specs = (pl.BlockSpec((tm, d), row), pl.BlockSpec((tm, d), row))
    return pl.pallas_call(
        functools.partial(_out_proj_kernel, final=final),
        grid=(m // tm,),
        in_specs=[pl.BlockSpec((tm, half), row), pl.BlockSpec((tm, half), row),
                  pl.BlockSpec((None, 2 * half, d), lambda i: (layer, 0, 0), pipeline_mode=pl.Buffered(1)),
                  pl.BlockSpec((tm, d), row), pl.BlockSpec((1, d), lambda i: (0, 0))],
        out_specs=out_specs,
        out_shape=out_shape,
        scratch_shapes=[pltpu.VMEM((2 * half, d), jnp.bfloat16)],
        compiler_params=_params("arbitrary"),
        name="out_proj_final" if final else "out_proj",
    )(mix1, mix2, w_stack, h, g_next.reshape(1, d))


A_REGROUP = 4


def _dilated_items(slope, q_ref, k_ref, v_ref, xf, x4, qp, kp, vp, o_s, lse_s):
    assert tuple(d for _, d in A_PATTERNS) == (1, A_REGROUP, A_REGROUP ** 2)
    s_len = q_ref.shape[0]
    n4 = s_len // A_REGROUP
    n16 = n4 // A_REGROUP

    for a, (src, dst) in enumerate(((q_ref, qp), (k_ref, kp), (v_ref, vp))):
        xf[a] = src[...].astype(jnp.float32)
        for c4 in range(A_REGROUP):
            x = xf[a, pl.ds(c4, n4, stride=A_REGROUP), :]
            x4[a, c4 * n4:(c4 + 1) * n4, :] = x
            dst[0, c4 * n4:(c4 + 1) * n4, :] = x.astype(dst.dtype)
        for c4 in range(A_REGROUP):
            for c2 in range(A_REGROUP):
                c16 = A_REGROUP * c2 + c4
                x = x4[a, pl.ds(c4 * n4 + c2, n16, stride=A_REGROUP), :]
                dst[1, c16 * n16:(c16 + 1) * n16, :] = x.astype(dst.dtype)
    srcs = [(q_ref, k_ref, v_ref), (qp.at[0], kp.at[0], vp.at[0]), (qp.at[1], kp.at[1], vp.at[1])]

    tiles = []
    bias = {}
    for p_idx, (_, dil) in enumerate(A_PATTERNS):
        n = s_len // dil
        kw = min(2 * QBLK, n)
        rel0 = (lax.broadcasted_iota(jnp.int32, (QBLK, kw), 1)
                - lax.broadcasted_iota(jnp.int32, (QBLK, kw), 0))
        for t in range(s_len // QBLK):
            c, i0 = divmod(t * QBLK, n)
            w0 = min(max(i0 - A_SIDE, 0), n - kw)
            if (p_idx, w0 - i0) not in bias:
                dist = jnp.abs(rel0 + (w0 - i0)).astype(jnp.float32)
                bias[(p_idx, w0 - i0)] = (slope * (-dil * LOG2E)) * jnp.where(dist <= float(A_SIDE), dist, FAR)
            if p_idx < 2:
                out_rows = pl.ds(c * n + i0, QBLK)
            else:
                c2, c4 = divmod(c, A_REGROUP)
                out_rows = pl.ds(c4 * n4 + c2 + A_REGROUP * i0, QBLK, stride=A_REGROUP)
            tiles.append((p_idx, slice(c * n + i0, c * n + i0 + QBLK), slice(c * n + w0, c * n + w0 + kw),
                          w0 - i0, out_rows))

    def item(p_idx, q_rows, k_rows, off, out_rows):
        q_src, k_src, v_src = srcs[p_idx]

        def scores():
            return _dot_nt(q_src[q_rows, :], k_src[k_rows, :])

        def finish(s):
            v_win = v_src[k_rows, :]
            logits = bias[(p_idx, off)] + s
            m = jnp.max(logits, axis=-1, keepdims=True)
            p = jnp.exp2(logits - m)
            l = jnp.sum(p, axis=-1, keepdims=True)
            o_s[p_idx, out_rows, :] = jnp.dot(p.astype(v_win.dtype), v_win, preferred_element_type=jnp.float32) / l
            lse_s[p_idx, out_rows, :] = jnp.broadcast_to(m + jnp.log2(l), (QBLK, HEAD_DIM))

        return scores, finish

    return [item(*t) for t in tiles]


def _dilated_merge(g_ref, o_ref, o_s, lse_s, o_nat):
    n4 = o_ref.shape[0] // A_REGROUP
    for c4 in range(A_REGROUP):
        blk = slice(c4 * n4, (c4 + 1) * n4)
        nat = pl.ds(c4, n4, stride=A_REGROUP)
        lses = (lse_s[0, nat, :], lse_s[1, blk, :], lse_s[2, blk, :])
        outs = (o_s[0, nat, :], o_s[1, blk, :], o_s[2, blk, :])
        top = jnp.maximum(jnp.maximum(lses[0], lses[1]), lses[2])
        ws = [jnp.exp2(ls - top) for ls in lses]
        num = ws[0] * outs[0] + ws[1] * outs[1] + ws[2] * outs[2]
        o_nat[nat, :] = num / (ws[0] + ws[1] + ws[2])
    o_ref[...] = (o_nat[...] * _silu(g_ref[...].astype(jnp.float32))).astype(o_ref.dtype)


NA_VARIANTS = 8


def _na_variant_row(v, rows):
    return v if v <= 4 else rows - NA_VARIANTS + v


def _na_row_start(r, rows):
    kh = min(NA_ROWS, rows)
    return min(max(r - kh // 2, 0), rows - kh)


def _na_bias_kernel(rpb_ref, o_ref, pair_ref, *, rows):
    kh = min(NA_ROWS, rows)
    n_drow = 2 * NA_ROWS_MAX - 1
    n_dcol = 2 * NA_COLS - 1
    h = pl.program_id(0)
    shape = (GRID_W, 2 * GRID_W)
    lane = lax.broadcasted_iota(jnp.int32, shape, 1)
    c = lax.broadcasted_iota(jnp.int32, shape, 0)
    kc = lane % GRID_W
    second = lane >= GRID_W
    cs = jnp.clip(c - NA_COLS // 2, 0, GRID_W - NA_COLS)
    valid = (kc >= cs) & (kc < cs + NA_COLS)
    dcol = kc - c + (NA_COLS - 1)
    for d in range(n_drow - 1):
        acc = jnp.zeros(shape, jnp.float32)
        for j in range(n_dcol):
            lo = rpb_ref[(h * n_drow + d) * n_dcol + j]
            hi = rpb_ref[(h * n_drow + d + 1) * n_dcol + j]
            acc = jnp.where(dcol == j, jnp.where(second, hi, lo), acc)
        pair_ref[d] = jnp.where(valid, acc * LOG2E, MASKED)
    for v in range(NA_VARIANTS):
        r = _na_variant_row(v, rows)
        d0 = _na_row_start(r, rows) - r + (NA_ROWS_MAX - 1)
        for i in range(0, kh, 2):
            o_ref[v, :, i * GRID_W:(i + 2) * GRID_W] = pair_ref[d0 + i]


def _na_bias_table(rpb, rows):
    n_heads = rpb.shape[0]
    kh = min(NA_ROWS, rows)
    return pl.pallas_call(
        functools.partial(_na_bias_kernel, rows=rows),
        grid=(n_heads,),
        in_specs=[pl.BlockSpec(memory_space=pltpu.SMEM)],
        out_specs=pl.BlockSpec((None, NA_VARIANTS, GRID_W, kh * GRID_W), lambda hi: (hi, 0, 0, 0)),
        out_shape=jax.ShapeDtypeStruct((n_heads, NA_VARIANTS, GRID_W, kh * GRID_W), jnp.float32),
        scratch_shapes=[pltpu.VMEM((2 * NA_ROWS_MAX - 2, GRID_W, 2 * GRID_W), jnp.float32)],
        compiler_params=_params("parallel"),
        name="na_bias",
    )(rpb.reshape(-1))


def _na_items(q_ref, k_ref, v_ref, g_ref, bias_ref, o_ref):
    s_len = q_ref.shape[0]
    rows = s_len // GRID_W
    win = min(NA_ROWS, rows) * GRID_W

    def item(r):
        k_rows = slice(_na_row_start(r, rows) * GRID_W, _na_row_start(r, rows) * GRID_W + win)
        variant = r if r < 4 else (r - (rows - NA_VARIANTS) if r > rows - 4 else 4)
        sl = slice(r * GRID_W, (r + 1) * GRID_W)

        def scores():
            return _dot_nt(q_ref[sl, :], k_ref[k_rows, :])

        def finish(s):
            v_win = v_ref[k_rows, :]
            logits = bias_ref[variant] + s
            m = jnp.max(logits, axis=-1, keepdims=True)
            p = jnp.exp2(logits - m)
            den = jnp.sum(p, axis=-1, keepdims=True)
            o = jnp.dot(p.astype(v_win.dtype), v_win, preferred_element_type=jnp.float32) / den
            o_ref[sl, :] = (o * _silu(g_ref[sl, :].astype(jnp.float32))).astype(o_ref.dtype)

        return scores, finish

    return [item(r) for r in range(rows)]


def _even_mixers_kernel(slopes_ref, qa_ref, ka_ref, va_ref, ga_ref, qb_ref, kb_ref, vb_ref, gb_ref, bias_ref,
                        oa_ref, ob_ref, xf, x4, qp, kp, vp, o_s, lse_s, o_nat):
    a_items = _dilated_items(slopes_ref[pl.program_id(0)], qa_ref, ka_ref, va_ref, xf, x4, qp, kp, vp, o_s, lse_s)
    b_items = _na_items(qb_ref, kb_ref, vb_ref, gb_ref, bias_ref, ob_ref)
    _run_pipelined(_interleave(a_items, b_items))
    _dilated_merge(ga_ref, oa_ref, o_s, lse_s, o_nat)


def _even_mixers(proj, slopes, bias, offs):
    b, s, _ = proj.shape
    col = lambda off: pl.BlockSpec((None, s, HEAD_DIM), lambda hi, bi: (bi, 0, off + hi))
    out_spec = pl.BlockSpec((None, s, HEAD_DIM), lambda hi, bi: (bi, 0, hi))
    out_shape = jax.ShapeDtypeStruct((b, s, MIX_W), jnp.bfloat16)
    n_pat = len(A_PATTERNS)
    f32_rows = lambda n: pltpu.VMEM((n, s, HEAD_DIM), jnp.float32)
    return pl.pallas_call(
        _even_mixers_kernel,
        grid=(N_HEADS, b),
        in_specs=[pl.BlockSpec(memory_space=pltpu.SMEM)] + [col(off) for off in offs]
        + [pl.BlockSpec((None,) + bias.shape[1:], lambda hi, bi: (hi, 0, 0, 0))],
        out_specs=(out_spec, out_spec),
        out_shape=(out_shape, out_shape),
        scratch_shapes=[f32_rows(3), f32_rows(3)]
        + [pltpu.VMEM((n_pat - 1, s, HEAD_DIM), jnp.bfloat16)] * 3
        + [f32_rows(n_pat), f32_rows(n_pat), pltpu.VMEM((s, HEAD_DIM), jnp.float32)],
        compiler_params=_params("parallel", "parallel"),
        name="even_mixers",
    )(slopes, *([proj] * 8), bias)


def _window_kernel(slopes_ref, sinks_ref, q_ref, k_ref, v_ref, g_ref, o_ref):
    s_len = q_ref.shape[0]
    hk = pl.program_id(1)
    kw = 3 * QBLK
    rel0 = (lax.broadcasted_iota(jnp.int32, (QBLK, kw), 1) - lax.broadcasted_iota(jnp.int32, (QBLK, kw), 0))
    slopes = [slopes_ref[hk * GROUP + gi] for gi in range(GROUP)]
    sinks = [sinks_ref[hk * GROUP + gi] for gi in range(GROUP)]
    sinks = [sink * LOG2E for sink in sinks]
    bias = {}
    tiles = []
    for n in range(s_len // QBLK):
        q0 = n * QBLK
        w0 = min(max(q0 - QBLK, 0), s_len - kw)
        if (w0 - q0, 0) not in bias:
            dist = jnp.abs(rel0 + (w0 - q0)).astype(jnp.float32)
            dist = jnp.where(dist <= float(C_WINDOW), dist, FAR)
            for gi in range(GROUP):
                bias[(w0 - q0, gi)] = (slopes[gi] * -LOG2E) * dist
        tiles += [(slice(q0, q0 + QBLK), gi, slice(w0, w0 + kw), w0 - q0) for gi in range(GROUP)]

    def item(q_rows, gi, k_rows, off):
        lanes = slice(gi * HEAD_DIM, (gi + 1) * HEAD_DIM)

        def scores():
            return _dot_nt(q_ref[q_rows, lanes], k_ref[k_rows, :])

        def finish(s):
            v_win = v_ref[k_rows, :]
            logits = bias[(off, gi)] + s
            m = jnp.maximum(jnp.max(logits, axis=-1, keepdims=True), sinks[gi])
            p = jnp.exp2(logits - m)
            den = jnp.sum(p, axis=-1, keepdims=True) + jnp.exp2(sinks[gi] - m)
            o = jnp.dot(p.astype(v_win.dtype), v_win, preferred_element_type=jnp.float32) / den
            gate = _silu(g_ref[q_rows, lanes].astype(jnp.float32))
            o_ref[q_rows, lanes] = (o * gate).astype(o_ref.dtype)

        return scores, finish

    _run_pipelined([item(*t) for t in tiles])


def _gqa_specs(s, q_off, k_off, v_off, g_off):
    gw = GROUP * HEAD_DIM
    wide = lambda off: pl.BlockSpec((None, s, gw), lambda bi, hi: (bi, 0, off // GROUP + hi))
    col = lambda off: pl.BlockSpec((None, s, HEAD_DIM), lambda bi, hi: (bi, 0, off + hi))
    in_specs = [wide(q_off), col(k_off), col(v_off), wide(g_off)]
    out_spec = pl.BlockSpec((None, s, gw), lambda bi, hi: (bi, 0, hi))
    return in_specs, out_spec


def _windowed_sink_attention(proj, slopes, sinks, q_off, k_off, v_off, g_off):
    b, s, _ = proj.shape
    in_specs, out_spec = _gqa_specs(s, q_off, k_off, v_off, g_off)
    smem = pl.BlockSpec(memory_space=pltpu.SMEM)
    return pl.pallas_call(
        _window_kernel,
        grid=(b, KV_HEADS),
        in_specs=[smem, smem] + in_specs,
        out_specs=out_spec,
        out_shape=jax.ShapeDtypeStruct((b, s, MIX_W), jnp.bfloat16),
        compiler_params=_params("parallel", "parallel"),
        name="window_sink_attn",
    )(slopes, sinks, proj, proj, proj, proj)


def _rope_tables(s):
    t = np.arange(s)
    quarter = HEAD_DIM // 4
    inv = jnp.asarray(ROPE_THETA, jnp.float32) ** (-jnp.arange(quarter, dtype=jnp.float32) / quarter)
    ang_r = jnp.asarray(t // GRID_W, jnp.float32)[:, None] * inv[None, :]
    ang_c = jnp.asarray(t % GRID_W, jnp.float32)[:, None] * inv[None, :]
    cos = jnp.concatenate([jnp.cos(ang_r), jnp.cos(ang_r), jnp.cos(ang_c), jnp.cos(ang_c)], axis=-1)
    sin = jnp.concatenate([-jnp.sin(ang_r), jnp.sin(ang_r), -jnp.sin(ang_c), jnp.sin(ang_c)], axis=-1)
    return cos, sin


def _norm_rope(x, gain, cos, sin):
    y = x * lax.rsqrt(jnp.mean(x * x, axis=-1, keepdims=True) + EPS) * gain
    quarter = HEAD_DIM // 4
    lane = lax.broadcasted_iota(jnp.int32, y.shape, 1)
    partner = jnp.where((lane // quarter) % 2 == 0,
                        pltpu.roll(y, HEAD_DIM - quarter, 1),
                        pltpu.roll(y, quarter, 1))
    return y * cos + partner * sin


D_ROWS_PER_DOT = 256


def _lane_tile_reduce(x, op):
    out = x[:, :HEAD_DIM]
    for j in range(1, x.shape[1] // HEAD_DIM):
        out = op(out, x[:, j * HEAD_DIM:(j + 1) * HEAD_DIM])
    return out


def _axial_kernel(q_ref, k_ref, v_ref, g_ref, cos_ref, sin_ref, gq_ref, gk_ref, o_ref,
                  kp, q4, s_buf, m_buf, p_buf, l_buf):
    s_len = q_ref.shape[0]
    n_blk = s_len // QBLK
    row_groups = [slice(r, r + D_ROWS_PER_DOT) for r in range(0, GROUP * QBLK, D_ROWS_PER_DOT)]
    kp[...] = _norm_rope(k_ref[...].astype(jnp.float32), gk_ref[...], cos_ref[...], sin_ref[...]).astype(kp.dtype)

    def block_rows(n):
        return pl.ds(pl.multiple_of(n * QBLK, QBLK), QBLK)

    def step(slot, n_prep, n_scores, n_probs, n_out):
        other = 1 - slot
        if n_prep is not None:
            rows = block_rows(n_prep)
            cos = cos_ref[rows, :]
            sin = sin_ref[rows, :]
            for gi in range(GROUP):
                q = _norm_rope(q_ref[rows, gi * HEAD_DIM:(gi + 1) * HEAD_DIM].astype(jnp.float32), gq_ref[...],
                               cos, sin)
                q4[other, gi * QBLK:(gi + 1) * QBLK, :] = (q * (SCALE * LOG2E)).astype(q4.dtype)
        if n_scores is not None:
            for hs in row_groups:
                s = _dot_nt(q4[slot, hs, :], kp[...])
                s_buf[slot, hs, :] = s
                m_buf[slot, hs, :] = jnp.broadcast_to(jnp.max(s, axis=-1, keepdims=True), (D_ROWS_PER_DOT, HEAD_DIM))
        if n_probs is not None:
            for hs in row_groups:
                p = jnp.exp2(s_buf[other, hs, :] - jnp.tile(m_buf[other, hs, :], (1, s_len // HEAD_DIM)))
                l_buf[other, hs, :] = _lane_tile_reduce(p, jnp.add)
                p_buf[other, hs, :] = p.astype(p_buf.dtype)
        if n_out is not None:
            rows = block_rows(n_out)
            for hs in row_groups:
                acc = jnp.dot(p_buf[slot, hs, :], v_ref[...], preferred_element_type=jnp.float32)
                o = acc / jnp.sum(l_buf[slot, hs, :], axis=-1, keepdims=True)
                for gi in range(hs.start // QBLK, hs.stop // QBLK):
                    lanes = slice(gi * HEAD_DIM, (gi + 1) * HEAD_DIM)
                    gate = _silu(g_ref[rows, lanes].astype(jnp.float32))
                    o_ref[rows, lanes] = (o[gi * QBLK - hs.start:(gi + 1) * QBLK - hs.start, :] * gate).astype(o_ref.dtype)

    step(1, 0, None, None, None)
    step(0, 1, 0, None, None)
    step(1, 2, 1, 0, None)

    def pair(i, carry):
        n = 2 * i
        step(0, n + 3, n + 2, n + 1, n)
        step(1, jnp.minimum(n + 4, n_blk - 1), n + 3, n + 2, n + 1)
        return carry

    lax.fori_loop(0, n_blk // 2 - 1, pair, 0)
    step(0, None, None, n_blk - 1, n_blk - 2)
    step(1, None, None, None, n_blk - 1)


def _axial_rope_attention(proj, cos, sin, gq, gk, q_off, k_off, v_off, g_off):
    b, s, _ = proj.shape
    in_specs, out_spec = _gqa_specs(s, q_off, k_off, v_off, g_off)
    table = pl.BlockSpec((s, HEAD_DIM), lambda bi, hi: (0, 0))
    gain = pl.BlockSpec((1, HEAD_DIM), lambda bi, hi: (0, 0))
    stacked = GROUP * QBLK
    return pl.pallas_call(
        _axial_kernel,
        grid=(b, KV_HEADS),
        in_specs=in_specs + [table, table, gain, gain],
        out_specs=out_spec,
        out_shape=jax.ShapeDtypeStruct((b, s, MIX_W), jnp.bfloat16),
        scratch_shapes=[pltpu.VMEM((s, HEAD_DIM), jnp.bfloat16),
                        pltpu.VMEM((2, stacked, HEAD_DIM), jnp.bfloat16),
                        pltpu.VMEM((2, stacked, s), jnp.float32),
                        pltpu.VMEM((2, stacked, HEAD_DIM), jnp.float32),
                        pltpu.VMEM((2, stacked, s), jnp.bfloat16),
                        pltpu.VMEM((2, stacked, HEAD_DIM), jnp.float32)],
        compiler_params=_params("parallel", "parallel"),
        name="axial_rope_attn",
    )(proj, proj, proj, proj, cos, sin, gq.reshape(1, HEAD_DIM), gk.reshape(1, HEAD_DIM))


def _alibi_slopes(n):
    return 2.0 ** (-8.0 * jnp.arange(1, n + 1, dtype=jnp.float32) / n)


def _q_column_scale(widths, q_sections):
    scale = np.ones(sum(widths), np.float32)
    for sec in q_sections:
        start = sum(widths[:sec])
        scale[start:start + widths[sec]] = SCALE * LOG2E
    return jnp.asarray(scale)


def kernel(x, norm_g, final_g, w_in_even, w_out_even, rpb_b, w_in_odd, w_out_odd, sinks_c, qnorm_d, knorm_d):
    b, s, d = x.shape
    bf16 = jnp.bfloat16
    slopes = _alibi_slopes(N_HEADS)
    cos, sin = _rope_tables(s)
    blk = lambda width: width // HEAD_DIM

    h = x.reshape(b * s, d)
    y = _rms_norm_rows(h, norm_g[0], bf16)
    for layer in range(DEPTH):
        i = layer // 2
        final = layer == DEPTH - 1
        g_next = final_g if final else norm_g[layer + 1]
        if layer % 2 == 0:
            widths = [MIX_W] * 8
            offs = [blk(MIX_W) * j for j in range(8)]
            proj = _in_proj(y, w_in_even, i, _q_column_scale(widths, (0, 4))).reshape(b, s, -1)
            mix1, mix2 = _even_mixers(proj, slopes, _na_bias_table(rpb_b[i], s // GRID_W), offs)
            w_out = w_out_even
        else:
            widths = [MIX_W, KV_W, KV_W, MIX_W] * 2
            offs = [blk(int(sum(widths[:j]))) for j in range(8)]
            proj = _in_proj(y, w_in_odd, i, _q_column_scale(widths, (0,))).reshape(b, s, -1)
            mix1 = _windowed_sink_attention(proj, slopes, sinks_c[i], *offs[:4])
            mix2 = _axial_rope_attention(proj, cos, sin, qnorm_d[i], knorm_d[i], *offs[4:])
            w_out = w_out_odd
        res = _out_proj(mix1.reshape(b * s, MIX_W), mix2.reshape(b * s, MIX_W), w_out, i, h, g_next, final)
        if final:
            return res.reshape(b, s, d)
        h, y = res
```

```python
import functools

import jax
import jax.numpy as jnp
import numpy as np
from jax import lax
from jax.experimental import pallas as pl
from jax.experimental.pallas import tpu as pltpu

D_MODEL = 2048
DEPTH = 4
HEAD_DIM = 128
GRID_W = 64
EPS = 1e-6
N_HEADS = 8
KV_HEADS = 2
GROUP = N_HEADS // KV_HEADS
MIX_W = N_HEADS * HEAD_DIM
KV_W = KV_HEADS * HEAD_DIM
A_PATTERNS = ((128, 1), (512, 4), (2048, 16))
A_SIDE = 64
NA_ROWS = 8
NA_COLS = 16
NA_ROWS_MAX = 8
C_WINDOW = 128
QBLK = 128
ROPE_THETA = 10000.0
SCALE = HEAD_DIM ** -0.5
LOG2E = 1.4426950408889634
MASKED = -1e30
FAR = 1e9
AHEAD = 6

VMEM_LIMIT = 56 * 1024 * 1024
IN_PROJ_ROWS = 1024
IN_PROJ_COLS = 1024
OUT_PROJ_ROWS = 512


def _params(*sem):
    return pltpu.CompilerParams(dimension_semantics=sem, vmem_limit_bytes=VMEM_LIMIT)


def _silu(x):
    return x * (1.0 / (1.0 + jnp.exp(-x)))


def _dot_nt(a, b):
    return lax.dot_general(a, b, (((1,), (1,)), ((), ())), preferred_element_type=jnp.float32)


def _run_pipelined(items):
    pending = [scores() for scores, _ in items[:AHEAD]]
    for idx, (_, finish) in enumerate(items):
        if idx + AHEAD < len(items):
            pending.append(items[idx + AHEAD][0]())
        finish(pending.pop(0))


def _interleave(a, b):
    keyed = [((i + 0.5) / len(a), 0, x) for i, x in enumerate(a)] + [((i + 0.5) / len(b), 1, x) for i, x in enumerate(b)]
    return [x for _, _, x in sorted(keyed, key=lambda t: t[:2])]


def _norm_kernel(h_ref, g_ref, y_ref):
    x = h_ref[...]
    y = x * lax.rsqrt(jnp.mean(x * x, axis=-1, keepdims=True) + EPS)
    y_ref[...] = (y * g_ref[...]).astype(y_ref.dtype)


def _rms_norm_rows(h, g, out_dtype, tm=512):
    m, d = h.shape
    return pl.pallas_call(
        _norm_kernel,
        grid=(m // tm,),
        in_specs=[pl.BlockSpec((tm, d), lambda i: (i, 0)), pl.BlockSpec((1, d), lambda i: (0, 0))],
        out_specs=pl.BlockSpec((tm, d), lambda i: (i, 0)),
        out_shape=jax.ShapeDtypeStruct((m, d), out_dtype),
        compiler_params=_params("parallel"),
        name="rms_norm",
    )(h, g.reshape(1, d))


def _in_proj_kernel(x_ref, w_ref, scale_ref, o_ref, wb_ref):
    @pl.when(pl.program_id(1) == 0)
    def _():
        wb_ref[...] = (w_ref[...] * scale_ref[...]).astype(wb_ref.dtype)

    o_ref[...] = jnp.dot(x_ref[...], wb_ref[...], preferred_element_type=jnp.float32).astype(o_ref.dtype)


def _in_proj(y, w_stack, layer, col_scale, tm=IN_PROJ_ROWS, tn=IN_PROJ_COLS):
    m, k = y.shape
    n = w_stack.shape[2]
    return pl.pallas_call(
        _in_proj_kernel,
        grid=(n // tn, m // tm),
        in_specs=[pl.BlockSpec((tm, k), lambda j, i: (i, 0)),
                  pl.BlockSpec((None, k, tn), lambda j, i: (layer, 0, j)),
                  pl.BlockSpec((1, tn), lambda j, i: (0, j))],
        out_specs=pl.BlockSpec((tm, tn), lambda j, i: (i, j)),
        out_shape=jax.ShapeDtypeStruct((m, n), jnp.bfloat16),
        scratch_shapes=[pltpu.VMEM((k, tn), jnp.bfloat16)],
        compiler_params=_params("arbitrary", "arbitrary"),
        name="in_proj",
    )(y, w_stack, col_scale.reshape(1, n))


def _out_proj_kernel(m1_ref, m2_ref, w_ref, h_ref, g_ref, *refs, final):
    out_refs, wb_ref = refs[:-1], refs[-1]
    half = m1_ref.shape[1]

    @pl.when(pl.program_id(0) == 0)
    def _():
        wb_ref[...] = w_ref[...].astype(wb_ref.dtype)

    acc = jnp.dot(m1_ref[...], wb_ref[:half, :], preferred_element_type=jnp.float32)
    acc = acc + jnp.dot(m2_ref[...], wb_ref[half:, :], preferred_element_type=jnp.float32)
    h = h_ref[...] + acc
    y = h * lax.rsqrt(jnp.mean(h * h, axis=-1, keepdims=True) + EPS) * g_ref[...]
    if final:
        out_refs[0][...] = y
    else:
        out_refs[0][...] = h
        out_refs[1][...] = y.astype(out_refs[1].dtype)


def _out_proj(mix1, mix2, w_stack, layer, h, g_next, final, tm=OUT_PROJ_ROWS):
    m, d = h.shape
    half = mix1.shape[1]
    row = lambda i: (i, 0)
    if final:
        out_shape = jax.ShapeDtypeStruct((m, d), jnp.float32)
        out_specs = pl.BlockSpec((tm, d), row)
    else:
        out_shape = (jax.ShapeDtypeStruct((m, d), jnp.float32), jax.ShapeDtypeStruct((m, d), jnp.bfloat16))
        out_specs = (pl.BlockSpec((tm, d), row), pl.BlockSpec((tm, d), row))
    return pl.pallas_call(
        functools.partial(_out_proj_kernel, final=final),
        grid=(m // tm,),
        in_specs=[pl.BlockSpec((tm, half), row), pl.BlockSpec((tm, half), row),
                  pl.BlockSpec((None, 2 * half, d), lambda i: (layer, 0, 0), pipeline_mode=pl.Buffered(1)),
                  pl.BlockSpec((tm, d), row), pl.BlockSpec((1, d), lambda i: (0, 0))],
        out_specs=out_specs,
        out_shape=out_shape,
        scratch_shapes=[pltpu.VMEM((2 * half, d), jnp.bfloat16)],
        compiler_params=_params("arbitrary"),
        name="out_proj_final" if final else "out_proj",
    )(mix1, mix2, w_stack, h, g_next.reshape(1, d))


A_REGROUP = 4


def _dilated_items(slope, q_ref, k_ref, v_ref, xf, x4, qp, kp, vp, o_s, lse_s):
    assert tuple(d for _, d in A_PATTERNS) == (1, A_REGROUP, A_REGROUP ** 2)
    s_len = q_ref.shape[0]
    n4 = s_len // A_REGROUP
    n16 = n4 // A_REGROUP

    for a, (src, dst) in enumerate(((q_ref, qp), (k_ref, kp), (v_ref, vp))):
        xf[a] = src[...].astype(jnp.float32)
        for c4 in range(A_REGROUP):
            x = xf[a, pl.ds(c4, n4, stride=A_REGROUP), :]
            x4[a, c4 * n4:(c4 + 1) * n4, :] = x
            dst[1, c4 * n4:(c4 + 1) * n4, :HEAD_DIM] = x.astype(dst.dtype)
        for c4 in range(A_REGROUP):
            for c2 in range(A_REGROUP):
                c16 = A_REGROUP * c2 + c4
                x = x4[a, pl.ds(c4 * n4 + c2, n16, stride=A_REGROUP), :]
                dst[2, c16 * n16:(c16 + 1) * n16, :HEAD_DIM] = x.astype(dst.dtype)
    vp[0, :, :HEAD_DIM] = v_ref[...]
    vp[:, :, HEAD_DIM:] = jnp.ones((len(A_PATTERNS), s_len, HEAD_DIM), vp.dtype)
    srcs = [(q_ref, k_ref, vp.at[0]), (qp.at[1], kp.at[1], vp.at[1]), (qp.at[2], kp.at[2], vp.at[2])]

    tiles = []
    bias = {}
    for p_idx, (_, dil) in enumerate(A_PATTERNS):
        n = s_len // dil
        kw = min(2 * QBLK, n)
        rel0 = (lax.broadcasted_iota(jnp.int32, (QBLK, kw), 1)
                - lax.broadcasted_iota(jnp.int32, (QBLK, kw), 0))
        for t in range(s_len // QBLK):
            c, i0 = divmod(t * QBLK, n)
            w0 = min(max(i0 - A_SIDE, 0), n - kw)
            if (p_idx, w0 - i0) not in bias:
                dist = jnp.abs(rel0 + (w0 - i0)).astype(jnp.float32)
                bias[(p_idx, w0 - i0)] = (slope * (-dil * LOG2E)) * jnp.where(dist <= float(A_SIDE), dist, FAR)
            if p_idx < 2:
                out_rows = pl.ds(c * n + i0, QBLK)
            else:
                c2, c4 = divmod(c, A_REGROUP)
                out_rows = pl.ds(c4 * n4 + c2 + A_REGROUP * i0, QBLK, stride=A_REGROUP)
            tiles.append((p_idx, slice(c * n + i0, c * n + i0 + QBLK), slice(c * n + w0, c * n + w0 + kw),
                          w0 - i0, out_rows))

    def item(p_idx, q_rows, k_rows, off, out_rows):
        q_src, k_src, v_src = srcs[p_idx]

        def scores():
            return _dot_nt(q_src[q_rows, :], k_src[k_rows, :])

        def finish(s):
            v_win = v_src[k_rows, :]
            logits = bias[(p_idx, off)] + s
            m = jnp.max(logits, axis=-1, keepdims=True)
            p = jnp.exp2((logits - m).astype(v_win.dtype))
            acc = jnp.dot(p, v_win, preferred_element_type=jnp.float32)
            o_s[p_idx, out_rows, :] = acc[:, :HEAD_DIM] / acc[:, HEAD_DIM:]
            lse_s[p_idx, out_rows, :] = m + jnp.log2(acc[:, HEAD_DIM:])

        return scores, finish

    return [item(*t) for t in tiles]


def _dilated_merge(g_ref, o_ref, o_s, lse_s, o_nat):
    n4 = o_ref.shape[0] // A_REGROUP
    for c4 in range(A_REGROUP):
        blk = slice(c4 * n4, (c4 + 1) * n4)
        nat = pl.ds(c4, n4, stride=A_REGROUP)
        lses = (lse_s[0, nat, :], lse_s[1, blk, :], lse_s[2, blk, :])
        outs = (o_s[0, nat, :], o_s[1, blk, :], o_s[2, blk, :])
        top = jnp.maximum(jnp.maximum(lses[0], lses[1]), lses[2])
        ws = [jnp.exp2(ls - top) for ls in lses]
        num = ws[0] * outs[0] + ws[1] * outs[1] + ws[2] * outs[2]
        o_nat[nat, :] = num / (ws[0] + ws[1] + ws[2])
    o_ref[...] = (o_nat[...] * _silu(g_ref[...].astype(jnp.float32))).astype(o_ref.dtype)


NA_VARIANTS = 8


def _na_variant_row(v, rows):
    return v if v <= 4 else rows - NA_VARIANTS + v


def _na_row_start(r, rows):
    kh = min(NA_ROWS, rows)
    return min(max(r - kh // 2, 0), rows - kh)


def _na_bias_kernel(rpb_ref, o_ref, pair_ref, *, rows):
    kh = min(NA_ROWS, rows)
    n_drow = 2 * NA_ROWS_MAX - 1
    n_dcol = 2 * NA_COLS - 1
    h = pl.program_id(0)
    shape = (GRID_W, 2 * GRID_W)
    lane = lax.broadcasted_iota(jnp.int32, shape, 1)
    c = lax.broadcasted_iota(jnp.int32, shape, 0)
    kc = lane % GRID_W
    second = lane >= GRID_W
    cs = jnp.clip(c - NA_COLS // 2, 0, GRID_W - NA_COLS)
    valid = (kc >= cs) & (kc < cs + NA_COLS)
    dcol = kc - c + (NA_COLS - 1)
    for d in range(n_drow - 1):
        acc = jnp.zeros(shape, jnp.float32)
        for j in range(n_dcol):
            lo = rpb_ref[(h * n_drow + d) * n_dcol + j]
            hi = rpb_ref[(h * n_drow + d + 1) * n_dcol + j]
            acc = jnp.where(dcol == j, jnp.where(second, hi, lo), acc)
        pair_ref[d] = jnp.where(valid, acc * LOG2E, MASKED)
    for v in range(NA_VARIANTS):
        r = _na_variant_row(v, rows)
        d0 = _na_row_start(r, rows) - r + (NA_ROWS_MAX - 1)
        for i in range(0, kh, 2):
            o_ref[v, :, i * GRID_W:(i + 2) * GRID_W] = pair_ref[d0 + i]


def _na_bias_table(rpb, rows):
    n_heads = rpb.shape[0]
    kh = min(NA_ROWS, rows)
    return pl.pallas_call(
        functools.partial(_na_bias_kernel, rows=rows),
        grid=(n_heads,),
        in_specs=[pl.BlockSpec(memory_space=pltpu.SMEM)],
        out_specs=pl.BlockSpec((None, NA_VARIANTS, GRID_W, kh * GRID_W), lambda hi: (hi, 0, 0, 0)),
        out_shape=jax.ShapeDtypeStruct((n_heads, NA_VARIANTS, GRID_W, kh * GRID_W), jnp.float32),
        scratch_shapes=[pltpu.VMEM((2 * NA_ROWS_MAX - 2, GRID_W, 2 * GRID_W), jnp.float32)],
        compiler_params=_params("parallel"),
        name="na_bias",
    )(rpb.reshape(-1))


def _na_items(q_ref, k_ref, v_ref, g_ref, bias_ref, o_ref, v1):
    s_len = q_ref.shape[0]
    v1[:, :HEAD_DIM] = v_ref[...]
    v1[:, HEAD_DIM:] = jnp.ones((s_len, HEAD_DIM), v1.dtype)
    rows = s_len // GRID_W
    win = min(NA_ROWS, rows) * GRID_W

    def item(r):
        k_rows = slice(_na_row_start(r, rows) * GRID_W, _na_row_start(r, rows) * GRID_W + win)
        variant = r if r < 4 else (r - (rows - NA_VARIANTS) if r > rows - 4 else 4)
        sl = slice(r * GRID_W, (r + 1) * GRID_W)

        def scores():
            return _dot_nt(q_ref[sl, :], k_ref[k_rows, :])

        def finish(s):
            v_win = v1[k_rows, :]
            logits = bias_ref[variant] + s
            m = jnp.max(logits, axis=-1, keepdims=True)
            p = jnp.exp2((logits - m).astype(v_win.dtype))
            acc = jnp.dot(p, v_win, preferred_element_type=jnp.float32)
            o = acc[:, :HEAD_DIM] / acc[:, HEAD_DIM:]
            o_ref[sl, :] = (o * _silu(g_ref[sl, :].astype(jnp.float32))).astype(o_ref.dtype)

        return scores, finish

    return [item(r) for r in range(rows)]


def _even_mixers_kernel(slopes_ref, qa_ref, ka_ref, va_ref, ga_ref, qb_ref, kb_ref, vb_ref, gb_ref, bias_ref,
                        oa_ref, ob_ref, xf, x4, qp, kp, vp, o_s, lse_s, o_nat, vb1):
    a_items = _dilated_items(slopes_ref[pl.program_id(0)], qa_ref, ka_ref, va_ref, xf, x4, qp, kp, vp, o_s, lse_s)
    b_items = _na_items(qb_ref, kb_ref, vb_ref, gb_ref, bias_ref, ob_ref, vb1)
    _run_pipelined(_interleave(a_items, b_items))
    _dilated_merge(ga_ref, oa_ref, o_s, lse_s, o_nat)


def _even_mixers(proj, slopes, bias, offs):
    b, s, _ = proj.shape
    col = lambda off: pl.BlockSpec((None, s, HEAD_DIM), lambda hi, bi: (bi, 0, off + hi))
    out_spec = pl.BlockSpec((None, s, HEAD_DIM), lambda hi, bi: (bi, 0, hi))
    out_shape = jax.ShapeDtypeStruct((b, s, MIX_W), jnp.bfloat16)
    n_pat = len(A_PATTERNS)
    f32_rows = lambda n: pltpu.VMEM((n, s, HEAD_DIM), jnp.float32)
    return pl.pallas_call(
        _even_mixers_kernel,
        grid=(N_HEADS, b),
        in_specs=[pl.BlockSpec(memory_space=pltpu.SMEM)] + [col(off) for off in offs]
        + [pl.BlockSpec((None,) + bias.shape[1:], lambda hi, bi: (hi, 0, 0, 0))],
        out_specs=(out_spec, out_spec),
        out_shape=(out_shape, out_shape),
        scratch_shapes=[f32_rows(3), f32_rows(3)]
        + [pltpu.VMEM((n_pat, s, HEAD_DIM), jnp.bfloat16)] * 2
        + [pltpu.VMEM((n_pat, s, 2 * HEAD_DIM), jnp.bfloat16)]
        + [f32_rows(n_pat), f32_rows(n_pat), pltpu.VMEM((s, HEAD_DIM), jnp.float32)]
        + [pltpu.VMEM((s, 2 * HEAD_DIM), jnp.bfloat16)],
        compiler_params=_params("parallel", "parallel"),
        name="even_mixers",
    )(slopes, *([proj] * 8), bias)


def _window_kernel(slopes_ref, sinks_ref, q_ref, k_ref, v_ref, g_ref, o_ref, v1):
    s_len = q_ref.shape[0]
    v1[:, :HEAD_DIM] = v_ref[...]
    v1[:, HEAD_DIM:] = jnp.ones((s_len, HEAD_DIM), v1.dtype)
    hk = pl.program_id(1)
    kw = 3 * QBLK
    rel0 = (lax.broadcasted_iota(jnp.int32, (QBLK, kw), 1) - lax.broadcasted_iota(jnp.int32, (QBLK, kw), 0))
    slopes = [slopes_ref[hk * GROUP + gi] for gi in range(GROUP)]
    sinks = [sinks_ref[hk * GROUP + gi] for gi in range(GROUP)]
    sinks = [sink * LOG2E for sink in sinks]
    bias = {}
    tiles = []
    for n in range(s_len // QBLK):
        q0 = n * QBLK
        w0 = min(max(q0 - QBLK, 0), s_len - kw)
        if (w0 - q0, 0) not in bias:
            dist = jnp.abs(rel0 + (w0 - q0)).astype(jnp.float32)
            dist = jnp.where(dist <= float(C_WINDOW), dist, FAR)
            for gi in range(GROUP):
                bias[(w0 - q0, gi)] = (slopes[gi] * -LOG2E) * dist
        tiles += [(slice(q0, q0 + QBLK), gi, slice(w0, w0 + kw), w0 - q0) for gi in range(GROUP)]

    def item(q_rows, gi, k_rows, off):
        lanes = slice(gi * HEAD_DIM, (gi + 1) * HEAD_DIM)

        def scores():
            return _dot_nt(q_ref[q_rows, lanes], k_ref[k_rows, :])

        def finish(s):
            v_win = v1[k_rows, :]
            logits = bias[(off, gi)] + s
            m = jnp.maximum(jnp.max(logits, axis=-1, keepdims=True), sinks[gi])
            p = jnp.exp2((logits - m).astype(v_win.dtype))
            acc = jnp.dot(p, v_win, preferred_element_type=jnp.float32)
            o = acc[:, :HEAD_DIM] / (acc[:, HEAD_DIM:] + jnp.exp2(sinks[gi] - m))
            gate = _silu(g_ref[q_rows, lanes].astype(jnp.float32))
            o_ref[q_rows, lanes] = (o * gate).astype(o_ref.dtype)

        return scores, finish

    _run_pipelined([item(*t) for t in tiles])


def _gqa_specs(s, q_off, k_off, v_off, g_off):
    gw = GROUP * HEAD_DIM
    wide = lambda off: pl.BlockSpec((None, s, gw), lambda bi, hi: (bi, 0, off // GROUP + hi))
    col = lambda off: pl.BlockSpec((None, s, HEAD_DIM), lambda bi, hi: (bi, 0, off + hi))
    in_specs = [wide(q_off), col(k_off), col(v_off), wide(g_off)]
    out_spec = pl.BlockSpec((None, s, gw), lambda bi, hi: (bi, 0, hi))
    return in_specs, out_spec


def _windowed_sink_attention(proj, slopes, sinks, q_off, k_off, v_off, g_off):
    b, s, _ = proj.shape
    in_specs, out_spec = _gqa_specs(s, q_off, k_off, v_off, g_off)
    smem = pl.BlockSpec(memory_space=pltpu.SMEM)
    return pl.pallas_call(
        _window_kernel,
        grid=(b, KV_HEADS),
        in_specs=[smem, smem] + in_specs,
        out_specs=out_spec,
        out_shape=jax.ShapeDtypeStruct((b, s, MIX_W), jnp.bfloat16),
        scratch_shapes=[pltpu.VMEM((s, 2 * HEAD_DIM), jnp.bfloat16)],
        compiler_params=_params("parallel", "parallel"),
        name="window_sink_attn",
    )(slopes, sinks, proj, proj, proj, proj)


def _rope_tables(s):
    t = np.arange(s)
    quarter = HEAD_DIM // 4
    inv = jnp.asarray(ROPE_THETA, jnp.float32) ** (-jnp.arange(quarter, dtype=jnp.float32) / quarter)
    ang_r = jnp.asarray(t // GRID_W, jnp.float32)[:, None] * inv[None, :]
    ang_c = jnp.asarray(t % GRID_W, jnp.float32)[:, None] * inv[None, :]
    cos = jnp.concatenate([jnp.cos(ang_r), jnp.cos(ang_r), jnp.cos(ang_c), jnp.cos(ang_c)], axis=-1)
    sin = jnp.concatenate([-jnp.sin(ang_r), jnp.sin(ang_r), -jnp.sin(ang_c), jnp.sin(ang_c)], axis=-1)
    return cos, sin


def _norm_rope(x, gain, cos, sin):
    y = x * lax.rsqrt(jnp.mean(x * x, axis=-1, keepdims=True) + EPS) * gain
    quarter = HEAD_DIM // 4
    lane = lax.broadcasted_iota(jnp.int32, y.shape, 1)
    partner = jnp.where((lane // quarter) % 2 == 0,
                        pltpu.roll(y, HEAD_DIM - quarter, 1),
                        pltpu.roll(y, quarter, 1))
    return y * cos + partner * sin


D_ROWS_PER_DOT = 256


def _axial_kernel(q_ref, k_ref, v_ref, g_ref, cos_ref, sin_ref, gq_ref, gk_ref, o_ref,
                  kp, v1, q4, s_buf, m_buf, p_buf):
    s_len = q_ref.shape[0]
    n_blk = s_len // QBLK
    row_groups = [slice(r, r + D_ROWS_PER_DOT) for r in range(0, GROUP * QBLK, D_ROWS_PER_DOT)]
    kp[...] = _norm_rope(k_ref[...].astype(jnp.float32), gk_ref[...], cos_ref[...], sin_ref[...]).astype(kp.dtype)
    v1[:, :HEAD_DIM] = v_ref[...]
    v1[:, HEAD_DIM:] = jnp.ones((s_len, HEAD_DIM), v1.dtype)

    def block_rows(n):
        return pl.ds(pl.multiple_of(n * QBLK, QBLK), QBLK)

    def step(slot, n_prep, n_scores, n_probs, n_out):
        other = 1 - slot
        if n_prep is not None:
            rows = block_rows(n_prep)
            cos = cos_ref[rows, :]
            sin = sin_ref[rows, :]
            for gi in range(GROUP):
                q = _norm_rope(q_ref[rows, gi * HEAD_DIM:(gi + 1) * HEAD_DIM].astype(jnp.float32), gq_ref[...],
                               cos, sin)
                q4[other, gi * QBLK:(gi + 1) * QBLK, :] = (q * (SCALE * LOG2E)).astype(q4.dtype)
        if n_scores is not None:
            for hs in row_groups:
                s = _dot_nt(q4[slot, hs, :], kp[...])
                s_buf[slot, hs, :] = s
                m_buf[slot, hs, :] = jnp.broadcast_to(jnp.max(s, axis=-1, keepdims=True), (D_ROWS_PER_DOT, HEAD_DIM))
        if n_probs is not None:
            for hs in row_groups:
                x = s_buf[other, hs, :] - jnp.tile(m_buf[other, hs, :], (1, s_len // HEAD_DIM))
                p_buf[other, hs, :] = jnp.exp2(x.astype(p_buf.dtype))
        if n_out is not None:
            rows = block_rows(n_out)
            for hs in row_groups:
                acc = jnp.dot(p_buf[slot, hs, :], v1[...], preferred_element_type=jnp.float32)
                o = acc[:, :HEAD_DIM] / acc[:, HEAD_DIM:]
                for gi in range(hs.start // QBLK, hs.stop // QBLK):
                    lanes = slice(gi * HEAD_DIM, (gi + 1) * HEAD_DIM)
                    gate = _silu(g_ref[rows, lanes].astype(jnp.float32))
                    o_ref[rows, lanes] = (o[gi * QBLK - hs.start:(gi + 1) * QBLK - hs.start, :] * gate).astype(o_ref.dtype)

    step(1, 0, None, None, None)
    step(0, 1, 0, None, None)
    step(1, 2, 1, 0, None)

    def pair(i, carry):
        n = 2 * i
        step(0, n + 3, n + 2, n + 1, n)
        step(1, jnp.minimum(n + 4, n_blk - 1), n + 3, n + 2, n + 1)
        return carry

    lax.fori_loop(0, n_blk // 2 - 1, pair, 0)
    step(0, None, None, n_blk - 1, n_blk - 2)
    step(1, None, None, None, n_blk - 1)


def _axial_rope_attention(proj, cos, sin, gq, gk, q_off, k_off, v_off, g_off):
    b, s, _ = proj.shape
    in_specs, out_spec = _gqa_specs(s, q_off, k_off, v_off, g_off)
    table = pl.BlockSpec((s, HEAD_DIM), lambda bi, hi: (0, 0))
    gain = pl.BlockSpec((1, HEAD_DIM), lambda bi, hi: (0, 0))
    stacked = GROUP * QBLK
    return pl.pallas_call(
        _axial_kernel,
        grid=(b, KV_HEADS),
        in_specs=in_specs + [table, table, gain, gain],
        out_specs=out_spec,
        out_shape=jax.ShapeDtypeStruct((b, s, MIX_W), jnp.bfloat16),
        scratch_shapes=[pltpu.VMEM((s, HEAD_DIM), jnp.bfloat16),
                        pltpu.VMEM((s, 2 * HEAD_DIM), jnp.bfloat16),
                        pltpu.VMEM((2, stacked, HEAD_DIM), jnp.bfloat16),
                        pltpu.VMEM((2, stacked, s), jnp.float32),
                        pltpu.VMEM((2, stacked, HEAD_DIM), jnp.float32),
                        pltpu.VMEM((2, stacked, s), jnp.bfloat16)],
        compiler_params=_params("parallel", "parallel"),
        name="axial_rope_attn",
    )(proj, proj, proj, proj, cos, sin, gq.reshape(1, HEAD_DIM), gk.reshape(1, HEAD_DIM))


def _alibi_slopes(n):
    return 2.0 ** (-8.0 * jnp.arange(1, n + 1, dtype=jnp.float32) / n)


def _q_column_scale(widths, q_sections):
    scale = np.ones(sum(widths), np.float32)
    for sec in q_sections:
        start = sum(widths[:sec])
        scale[start:start + widths[sec]] = SCALE * LOG2E
    return jnp.asarray(scale)


def kernel(x, norm_g, final_g, w_in_even, w_out_even, rpb_b, w_in_odd, w_out_odd, sinks_c, qnorm_d, knorm_d):
    b, s, d = x.shape
    bf16 = jnp.bfloat16
    slopes = _alibi_slopes(N_HEADS)
    cos, sin = _rope_tables(s)
    blk = lambda width: width // HEAD_DIM

    h = x.reshape(b * s, d)
    y = _rms_norm_rows(h, norm_g[0], bf16)
    for layer in range(DEPTH):
        i = layer // 2
        final = layer == DEPTH - 1
        g_next = final_g if final else norm_g[layer + 1]
        if layer % 2 == 0:
            widths = [MIX_W] * 8
            offs = [blk(MIX_W) * j for j in range(8)]
            proj = _in_proj(y, w_in_even, i, _q_column_scale(widths, (0, 4))).reshape(b, s, -1)
            mix1, mix2 = _even_mixers(proj, slopes, _na_bias_table(rpb_b[i], s // GRID_W), offs)
            w_out = w_out_even
        else:
            widths = [MIX_W, KV_W, KV_W, MIX_W] * 2
            offs = [blk(int(sum(widths[:j]))) for j in range(8)]
            proj = _in_proj(y, w_in_odd, i, _q_column_scale(widths, (0,))).reshape(b, s, -1)
            mix1 = _windowed_sink_attention(proj, slopes, sinks_c[i], *offs[:4])
            mix2 = _axial_rope_attention(proj, cos, sin, qnorm_d[i], knorm_d[i], *offs[4:])
            w_out = w_out_odd
        res = _out_proj(mix1.reshape(b * s, MIX_W), mix2.reshape(b * s, MIX_W), w_out, i, h, g_next, final)
        if final:
            return res.reshape(b, s, d)
        h, y = res
```

```python
import functools

import jax
import jax.numpy as jnp
import numpy as np
from jax import lax
from jax.experimental import pallas as pl
from jax.experimental.pallas import tpu as pltpu

D_MODEL = 2048
DEPTH = 4
HEAD_DIM = 128
GRID_W = 64
EPS = 1e-6
N_HEADS = 8
KV_HEADS = 2
GROUP = N_HEADS // KV_HEADS
MIX_W = N_HEADS * HEAD_DIM
KV_W = KV_HEADS * HEAD_DIM
A_PATTERNS = ((128, 1), (512, 4), (2048, 16))
A_SIDE = 64
NA_ROWS = 8
NA_COLS = 16
NA_ROWS_MAX = 8
C_WINDOW = 128
QBLK = 128
ROPE_THETA = 10000.0
SCALE = HEAD_DIM ** -0.5
LOG2E = 1.4426950408889634
MASKED = -1e30
FAR = 1e9
AHEAD = 6

VMEM_LIMIT = 56 * 1024 * 1024
IN_PROJ_ROWS = 2048
IN_PROJ_COLS = 1024
OUT_PROJ_ROWS = 512


def _params(*sem):
    return pltpu.CompilerParams(dimension_semantics=sem, vmem_limit_bytes=VMEM_LIMIT)


def _silu(x):
    return x * (1.0 / (1.0 + jnp.exp(-x)))


def _dot_nt(a, b):
    return lax.dot_general(a, b, (((1,), (1,)), ((), ())), preferred_element_type=jnp.float32)


def _run_pipelined(items):
    pending = [scores() for scores, _ in items[:AHEAD]]
    for idx, (_, finish) in enumerate(items):
        if idx + AHEAD < len(items):
            pending.append(items[idx + AHEAD][0]())
        finish(pending.pop(0))


def _interleave(a, b):
    keyed = [((i + 0.5) / len(a), 0, x) for i, x in enumerate(a)] + [((i + 0.5) / len(b), 1, x) for i, x in enumerate(b)]
    return [x for _, _, x in sorted(keyed, key=lambda t: t[:2])]


def _norm_kernel(h_ref, g_ref, y_ref):
    x = h_ref[...]
    y = x * lax.rsqrt(jnp.mean(x * x, axis=-1, keepdims=True) + EPS)
    y_ref[...] = (y * g_ref[...]).astype(y_ref.dtype)


def _rms_norm_rows(h, g, out_dtype, tm=512):
    m, d = h.shape
    return pl.pallas_call(
        _norm_kernel,
        grid=(m // tm,),
        in_specs=[pl.BlockSpec((tm, d), lambda i: (i, 0)), pl.BlockSpec((1, d), lambda i: (0, 0))],
        out_specs=pl.BlockSpec((tm, d), lambda i: (i, 0)),
        out_shape=jax.ShapeDtypeStruct((m, d), out_dtype),
        compiler_params=_params("parallel"),
        name="rms_norm",
    )(h, g.reshape(1, d))


def _in_proj_kernel(x_ref, w_ref, scale_ref, o_ref, wb_ref):
    @pl.when(pl.program_id(1) == 0)
    def _():
        wb_ref[...] = (w_ref[...] * scale_ref[...]).astype(wb_ref.dtype)

    o_ref[...] = jnp.dot(x_ref[...], wb_ref[...], preferred_element_type=jnp.float32).astype(o_ref.dtype)


def _in_proj(y, w_stack, layer, col_scale, tm=IN_PROJ_ROWS, tn=IN_PROJ_COLS):
    m, k = y.shape
    n = w_stack.shape[2]
    return pl.pallas_call(
        _in_proj_kernel,
        grid=(n // tn, m // tm),
        in_specs=[pl.BlockSpec((tm, k), lambda j, i: (i, 0)),
                  pl.BlockSpec((None, k, tn), lambda j, i: (layer, 0, j)),
                  pl.BlockSpec((1, tn), lambda j, i: (0, j))],
        out_specs=pl.BlockSpec((tm, tn), lambda j, i: (i, j)),
        out_shape=jax.ShapeDtypeStruct((m, n), jnp.bfloat16),
        scratch_shapes=[pltpu.VMEM((k, tn), jnp.bfloat16)],
        compiler_params=_params("arbitrary", "arbitrary"),
        name="in_proj",
    )(y, w_stack, col_scale.reshape(1, n))


def _out_proj_kernel(m1_ref, m2_ref, w_ref, h_ref, g_ref, *refs, final):
    out_refs, wb_ref = refs[:-1], refs[-1]
    half = m1_ref.shape[1]

    @pl.when(pl.program_id(0) == 0)
    def _():
        wb_ref[...] = w_ref[...].astype(wb_ref.dtype)

    acc = jnp.dot(m1_ref[...], wb_ref[:half, :], preferred_element_type=jnp.float32)
    acc = acc + jnp.dot(m2_ref[...], wb_ref[half:, :], preferred_element_type=jnp.float32)
    h = h_ref[...] + acc
    y = h * lax.rsqrt(jnp.mean(h * h, axis=-1, keepdims=True) + EPS) * g_ref[...]
    if final:
        out_refs[0][...] = y
    else:
        out_refs[0][...] = h
        out_refs[1][...] = y.astype(out_refs[1].dtype)


def _out_proj(mix1, mix2, w_stack, layer, h, g_next, final, tm=OUT_PROJ_ROWS):
    m, d = h.shape
    half = mix1.shape[1]
    row = lambda i: (i, 0)
    if final:
        out_shape = jax.ShapeDtypeStruct((m, d), jnp.float32)
        out_specs = pl.BlockSpec((tm, d), row)
    else:
        out_shape = (jax.ShapeDtypeStruct((m, d), jnp.float32), jax.ShapeDtypeStruct((m, d), jnp.bfloat16))
        out_specs = (pl.BlockSpec((tm, d), row), pl.BlockSpec((tm, d), row))
    return pl.pallas_call(
        functools.partial(_out_proj_kernel, final=final),
        grid=(m // tm,),
        in_specs=[pl.BlockSpec((tm, half), row), pl.BlockSpec((tm, half), row),
                  pl.BlockSpec((None, 2 * half, d), lambda i: (layer, 0, 0), pipeline_mode=pl.Buffered(1)),
                  pl.BlockSpec((tm, d), row), pl.BlockSpec((1, d), lambda i: (0, 0))],
        out_specs=out_specs,
        out_shape=out_shape,
        scratch_shapes=[pltpu.VMEM((2 * half, d), jnp.bfloat16)],
        compiler_params=_params("arbitrary"),
        name="out_proj_final" if final else "out_proj",
    )(mix1, mix2, w_stack, h, g_next.reshape(1, d))


A_REGROUP = 4


def _dilated_items(slope, q_ref, k_ref, v_ref, xf, x4, qp, kp, vp, o_s, lse_s):
    assert tuple(d for _, d in A_PATTERNS) == (1, A_REGROUP, A_REGROUP ** 2)
    s_len = q_ref.shape[0]
    n4 = s_len // A_REGROUP
    n16 = n4 // A_REGROUP

    for a, (src, dst) in enumerate(((q_ref, qp), (k_ref, kp), (v_ref, vp))):
        xf[a] = src[...].astype(jnp.float32)
        for c4 in range(A_REGROUP):
            x = xf[a, pl.ds(c4, n4, stride=A_REGROUP), :]
            x4[a, c4 * n4:(c4 + 1) * n4, :] = x
            dst[0, c4 * n4:(c4 + 1) * n4, :] = x.astype(dst.dtype)
        for c4 in range(A_REGROUP):
            for c2 in range(A_REGROUP):
                c16 = A_REGROUP * c2 + c4
                x = x4[a, pl.ds(c4 * n4 + c2, n16, stride=A_REGROUP), :]
                dst[1, c16 * n16:(c16 + 1) * n16, :] = x.astype(dst.dtype)
    srcs = [(q_ref, k_ref, v_ref), (qp.at[0], kp.at[0], vp.at[0]), (qp.at[1], kp.at[1], vp.at[1])]

    tiles = []
    bias = {}
    for p_idx, (_, dil) in enumerate(A_PATTERNS):
        n = s_len // dil
        kw = min(2 * QBLK, n)
        rel0 = (lax.broadcasted_iota(jnp.int32, (QBLK, kw), 1)
                - lax.broadcasted_iota(jnp.int32, (QBLK, kw), 0))
        for t in range(s_len // QBLK):
            c, i0 = divmod(t * QBLK, n)
            w0 = min(max(i0 - A_SIDE, 0), n - kw)
            if (p_idx, w0 - i0) not in bias:
                dist = jnp.abs(rel0 + (w0 - i0)).astype(jnp.float32)
                bias[(p_idx, w0 - i0)] = (slope * (-dil * LOG2E)) * jnp.where(dist <= float(A_SIDE), dist, FAR)
            if p_idx < 2:
                out_rows = pl.ds(c * n + i0, QBLK)
            else:
                c2, c4 = divmod(c, A_REGROUP)
                out_rows = pl.ds(c4 * n4 + c2 + A_REGROUP * i0, QBLK, stride=A_REGROUP)
            tiles.append((p_idx, slice(c * n + i0, c * n + i0 + QBLK), slice(c * n + w0, c * n + w0 + kw),
                          w0 - i0, out_rows))

    def item(p_idx, q_rows, k_rows, off, out_rows):
        q_src, k_src, v_src = srcs[p_idx]

        def scores():
            return _dot_nt(q_src[q_rows, :], k_src[k_rows, :])

        def finish(s):
            v_win = v_src[k_rows, :]
            logits = bias[(p_idx, off)] + s
            m = jnp.max(logits, axis=-1, keepdims=True)
            p = jnp.exp2(logits - m)
            l = jnp.sum(p, axis=-1, keepdims=True)
            o_s[p_idx, out_rows, :] = jnp.dot(p.astype(v_win.dtype), v_win, preferred_element_type=jnp.float32) / l
            lse_s[p_idx, out_rows, :] = jnp.broadcast_to(m + jnp.log2(l), (QBLK, HEAD_DIM))

        return scores, finish

    return [item(*t) for t in tiles]


def _dilated_merge(g_ref, o_ref, o_s, lse_s, o_nat):
    n4 = o_ref.shape[0] // A_REGROUP
    for c4 in range(A_REGROUP):
        blk = slice(c4 * n4, (c4 + 1) * n4)
        nat = pl.ds(c4, n4, stride=A_REGROUP)
        lses = (lse_s[0, nat, :], lse_s[1, blk, :], lse_s[2, blk, :])
        outs = (o_s[0, nat, :], o_s[1, blk, :], o_s[2, blk, :])
        top = jnp.maximum(jnp.maximum(lses[0], lses[1]), lses[2])
        ws = [jnp.exp2(ls - top) for ls in lses]
        num = ws[0] * outs[0] + ws[1] * outs[1] + ws[2] * outs[2]
        o_nat[nat, :] = num / (ws[0] + ws[1] + ws[2])
    o_ref[...] = (o_nat[...] * _silu(g_ref[...].astype(jnp.float32))).astype(o_ref.dtype)


NA_VARIANTS = 8


def _na_variant_row(v, rows):
    return v if v <= 4 else rows - NA_VARIANTS + v


def _na_row_start(r, rows):
    kh = min(NA_ROWS, rows)
    return min(max(r - kh // 2, 0), rows - kh)


def _na_bias_kernel(rpb_ref, o_ref, pair_ref, *, rows):
    kh = min(NA_ROWS, rows)
    n_drow = 2 * NA_ROWS_MAX - 1
    n_dcol = 2 * NA_COLS - 1
    h = pl.program_id(0)
    shape = (GRID_W, 2 * GRID_W)
    lane = lax.broadcasted_iota(jnp.int32, shape, 1)
    c = lax.broadcasted_iota(jnp.int32, shape, 0)
    kc = lane % GRID_W
    second = lane >= GRID_W
    cs = jnp.clip(c - NA_COLS // 2, 0, GRID_W - NA_COLS)
    valid = (kc >= cs) & (kc < cs + NA_COLS)
    dcol = kc - c + (NA_COLS - 1)
    for d in range(n_drow - 1):
        acc = jnp.zeros(shape, jnp.float32)
        for j in range(n_dcol):
            lo = rpb_ref[(h * n_drow + d) * n_dcol + j]
            hi = rpb_ref[(h * n_drow + d + 1) * n_dcol + j]
            acc = jnp.where(dcol == j, jnp.where(second, hi, lo), acc)
        pair_ref[d] = jnp.where(valid, acc * LOG2E, MASKED)
    for v in range(NA_VARIANTS):
        r = _na_variant_row(v, rows)
        d0 = _na_row_start(r, rows) - r + (NA_ROWS_MAX - 1)
        for i in range(0, kh, 2):
            o_ref[v, :, i * GRID_W:(i + 2) * GRID_W] = pair_ref[d0 + i]


def _na_bias_table(rpb, rows):
    n_heads = rpb.shape[0]
    kh = min(NA_ROWS, rows)
    return pl.pallas_call(
        functools.partial(_na_bias_kernel, rows=rows),
        grid=(n_heads,),
        in_specs=[pl.BlockSpec(memory_space=pltpu.SMEM)],
        out_specs=pl.BlockSpec((None, NA_VARIANTS, GRID_W, kh * GRID_W), lambda hi: (hi, 0, 0, 0)),
        out_shape=jax.ShapeDtypeStruct((n_heads, NA_VARIANTS, GRID_W, kh * GRID_W), jnp.float32),
        scratch_shapes=[pltpu.VMEM((2 * NA_ROWS_MAX - 2, GRID_W, 2 * GRID_W), jnp.float32)],
        compiler_params=_params("parallel"),
        name="na_bias",
    )(rpb.reshape(-1))


def _na_items(q_ref, k_ref, v_ref, g_ref, bias_ref, o_ref):
    s_len = q_ref.shape[0]
    rows = s_len // GRID_W
    win = min(NA_ROWS, rows) * GRID_W

    def item(r):
        k_rows = slice(_na_row_start(r, rows) * GRID_W, _na_row_start(r, rows) * GRID_W + win)
        variant = r if r < 4 else (r - (rows - NA_VARIANTS) if r > rows - 4 else 4)
        sl = slice(r * GRID_W, (r + 1) * GRID_W)

        def scores():
            return _dot_nt(q_ref[sl, :], k_ref[k_rows, :])

        def finish(s):
            v_win = v_ref[k_rows, :]
            logits = bias_ref[variant] + s
            m = jnp.max(logits, axis=-1, keepdims=True)
            p = jnp.exp2(logits - m)
            den = jnp.sum(p, axis=-1, keepdims=True)
            o = jnp.dot(p.astype(v_win.dtype), v_win, preferred_element_type=jnp.float32) / den
            o_ref[sl, :] = (o * _silu(g_ref[sl, :].astype(jnp.float32))).astype(o_ref.dtype)

        return scores, finish

    return [item(r) for r in range(rows)]


def _even_mixers_kernel(slopes_ref, qa_ref, ka_ref, va_ref, ga_ref, qb_ref, kb_ref, vb_ref, gb_ref, bias_ref,
                        oa_ref, ob_ref, xf, x4, qp, kp, vp, o_s, lse_s, o_nat):
    a_items = _dilated_items(slopes_ref[pl.program_id(0)], qa_ref, ka_ref, va_ref, xf, x4, qp, kp, vp, o_s, lse_s)
    b_items = _na_items(qb_ref, kb_ref, vb_ref, gb_ref, bias_ref, ob_ref)
    _run_pipelined(_interleave(a_items, b_items))
    _dilated_merge(ga_ref, oa_ref, o_s, lse_s, o_nat)


def _even_mixers(proj, slopes, bias, offs):
    b, s, _ = proj.shape
    col = lambda off: pl.BlockSpec((None, s, HEAD_DIM), lambda hi, bi: (bi, 0, off + hi))
    out_spec = pl.BlockSpec((None, s, HEAD_DIM), lambda hi, bi: (bi, 0, hi))
    out_shape = jax.ShapeDtypeStruct((b, s, MIX_W), jnp.bfloat16)
    n_pat = len(A_PATTERNS)
    f32_rows = lambda n: pltpu.VMEM((n, s, HEAD_DIM), jnp.float32)
    return pl.pallas_call(
        _even_mixers_kernel,
        grid=(N_HEADS, b),
        in_specs=[pl.BlockSpec(memory_space=pltpu.SMEM)] + [col(off) for off in offs]
        + [pl.BlockSpec((None,) + bias.shape[1:], lambda hi, bi: (hi, 0, 0, 0))],
        out_specs=(out_spec, out_spec),
        out_shape=(out_shape, out_shape),
        scratch_shapes=[f32_rows(3), f32_rows(3)]
        + [pltpu.VMEM((n_pat - 1, s, HEAD_DIM), jnp.bfloat16)] * 3
        + [f32_rows(n_pat), f32_rows(n_pat), pltpu.VMEM((s, HEAD_DIM), jnp.float32)],
        compiler_params=_params("parallel", "parallel"),
        name="even_mixers",
    )(slopes, *([proj] * 8), bias)


def _window_kernel(slopes_ref, sinks_ref, q_ref, k_ref, v_ref, g_ref, o_ref, v1):
    s_len = q_ref.shape[0]
    v1[:, :HEAD_DIM] = v_ref[...]
    v1[:, HEAD_DIM:] = jnp.ones((s_len, HEAD_DIM), v1.dtype)
    hk = pl.program_id(1)
    kw = 3 * QBLK
    rel0 = (lax.broadcasted_iota(jnp.int32, (QBLK, kw), 1) - lax.broadcasted_iota(jnp.int32, (QBLK, kw), 0))
    slopes = [slopes_ref[hk * GROUP + gi] for gi in range(GROUP)]
    sinks = [sinks_ref[hk * GROUP + gi] for gi in range(GROUP)]
    sinks = [sink * LOG2E for sink in sinks]
    bias = {}
    tiles = []
    for n in range(s_len // QBLK):
        q0 = n * QBLK
        w0 = min(max(q0 - QBLK, 0), s_len - kw)
        if (w0 - q0, 0) not in bias:
            dist = jnp.abs(rel0 + (w0 - q0)).astype(jnp.float32)
            dist = jnp.where(dist <= float(C_WINDOW), dist, FAR)
            for gi in range(GROUP):
                bias[(w0 - q0, gi)] = (slopes[gi] * -LOG2E) * dist
        tiles += [(slice(q0, q0 + QBLK), gi, slice(w0, w0 + kw), w0 - q0) for gi in range(GROUP)]

    def item(q_rows, gi, k_rows, off):
        lanes = slice(gi * HEAD_DIM, (gi + 1) * HEAD_DIM)

        def scores():
            return _dot_nt(q_ref[q_rows, lanes], k_ref[k_rows, :])

        def finish(s):
            v_win = v1[k_rows, :]
            logits = bias[(off, gi)] + s
            m = jnp.maximum(jnp.max(logits, axis=-1, keepdims=True), sinks[gi])
            p = jnp.exp2((logits - m).astype(v_win.dtype))
            acc = jnp.dot(p, v_win, preferred_element_type=jnp.float32)
            o = acc[:, :HEAD_DIM] / (acc[:, HEAD_DIM:] + jnp.exp2(sinks[gi] - m))
            gate = _silu(g_ref[q_rows, lanes].astype(jnp.float32))
            o_ref[q_rows, lanes] = (o * gate).astype(o_ref.dtype)

        return scores, finish

    _run_pipelined([item(*t) for t in tiles])


def _gqa_specs(s, q_off, k_off, v_off, g_off):
    gw = GROUP * HEAD_DIM
    wide = lambda off: pl.BlockSpec((None, s, gw), lambda bi, hi: (bi, 0, off // GROUP + hi))
    col = lambda off: pl.BlockSpec((None, s, HEAD_DIM), lambda bi, hi: (bi, 0, off + hi))
    in_specs = [wide(q_off), col(k_off), col(v_off), wide(g_off)]
    out_spec = pl.BlockSpec((None, s, gw), lambda bi, hi: (bi, 0, hi))
    return in_specs, out_spec


def _windowed_sink_attention(proj, slopes, sinks, q_off, k_off, v_off, g_off):
    b, s, _ = proj.shape
    in_specs, out_spec = _gqa_specs(s, q_off, k_off, v_off, g_off)
    smem = pl.BlockSpec(memory_space=pltpu.SMEM)
    return pl.pallas_call(
        _window_kernel,
        grid=(b, KV_HEADS),
        in_specs=[smem, smem] + in_specs,
        out_specs=out_spec,
        out_shape=jax.ShapeDtypeStruct((b, s, MIX_W), jnp.bfloat16),
        scratch_shapes=[pltpu.VMEM((s, 2 * HEAD_DIM), jnp.bfloat16)],
        compiler_params=_params("parallel", "parallel"),
        name="window_sink_attn",
    )(slopes, sinks, proj, proj, proj, proj)


def _rope_tables(s):
    t = np.arange(s)
    quarter = HEAD_DIM // 4
    inv = jnp.asarray(ROPE_THETA, jnp.float32) ** (-jnp.arange(quarter, dtype=jnp.float32) / quarter)
    ang_r = jnp.asarray(t // GRID_W, jnp.float32)[:, None] * inv[None, :]
    ang_c = jnp.asarray(t % GRID_W, jnp.float32)[:, None] * inv[None, :]
    cos = jnp.concatenate([jnp.cos(ang_r), jnp.cos(ang_r), jnp.cos(ang_c), jnp.cos(ang_c)], axis=-1)
    sin = jnp.concatenate([-jnp.sin(ang_r), jnp.sin(ang_r), -jnp.sin(ang_c), jnp.sin(ang_c)], axis=-1)
    return cos, sin


def _norm_rope(x, gain, cos, sin):
    y = x * lax.rsqrt(jnp.mean(x * x, axis=-1, keepdims=True) + EPS) * gain
    quarter = HEAD_DIM // 4
    lane = lax.broadcasted_iota(jnp.int32, y.shape, 1)
    partner = jnp.where((lane // quarter) % 2 == 0,
                        pltpu.roll(y, HEAD_DIM - quarter, 1),
                        pltpu.roll(y, quarter, 1))
    return y * cos + partner * sin


D_ROWS_PER_DOT = 256


def _axial_kernel(q_ref, k_ref, v_ref, g_ref, cos_ref, sin_ref, gq_ref, gk_ref, o_ref,
                  kp, v1, q4, s_buf, m_buf, p_buf):
    s_len = q_ref.shape[0]
    n_blk = s_len // QBLK
    row_groups = [slice(r, r + D_ROWS_PER_DOT) for r in range(0, GROUP * QBLK, D_ROWS_PER_DOT)]
    kp[...] = _norm_rope(k_ref[...].astype(jnp.float32), gk_ref[...], cos_ref[...], sin_ref[...]).astype(kp.dtype)
    v1[:, :HEAD_DIM] = v_ref[...]
    v1[:, HEAD_DIM:] = jnp.ones((s_len, HEAD_DIM), v1.dtype)

    def block_rows(n):
        return pl.ds(pl.multiple_of(n * QBLK, QBLK), QBLK)

    def step(slot, n_prep, n_scores, n_probs, n_out):
        other = 1 - slot
        if n_prep is not None:
            rows = block_rows(n_prep)
            cos = cos_ref[rows, :]
            sin = sin_ref[rows, :]
            for gi in range(GROUP):
                q = _norm_rope(q_ref[rows, gi * HEAD_DIM:(gi + 1) * HEAD_DIM].astype(jnp.float32), gq_ref[...],
                               cos, sin)
                q4[other, gi * QBLK:(gi + 1) * QBLK, :] = (q * (SCALE * LOG2E)).astype(q4.dtype)
        if n_scores is not None:
            for hs in row_groups:
                s = _dot_nt(q4[slot, hs, :], kp[...])
                s_buf[slot, hs, :] = s
                m_buf[slot, hs, :] = jnp.broadcast_to(jnp.max(s, axis=-1, keepdims=True), (D_ROWS_PER_DOT, HEAD_DIM))
        if n_probs is not None:
            for hs in row_groups:
                x = s_buf[other, hs, :] - jnp.tile(m_buf[other, hs, :], (1, s_len // HEAD_DIM))
                p_buf[other, hs, :] = jnp.exp2(x.astype(p_buf.dtype))
        if n_out is not None:
            rows = block_rows(n_out)
            for hs in row_groups:
                acc = jnp.dot(p_buf[slot, hs, :], v1[...], preferred_element_type=jnp.float32)
                o = acc[:, :HEAD_DIM] / acc[:, HEAD_DIM:]
                for gi in range(hs.start // QBLK, hs.stop // QBLK):
                    lanes = slice(gi * HEAD_DIM, (gi + 1) * HEAD_DIM)
                    gate = _silu(g_ref[rows, lanes].astype(jnp.float32))
                    o_ref[rows, lanes] = (o[gi * QBLK - hs.start:(gi + 1) * QBLK - hs.start, :] * gate).astype(o_ref.dtype)

    step(1, 0, None, None, None)
    step(0, 1, 0, None, None)
    step(1, 2, 1, 0, None)

    def pair(i, carry):
        n = 2 * i
        step(0, n + 3, n + 2, n + 1, n)
        step(1, jnp.minimum(n + 4, n_blk - 1), n + 3, n + 2, n + 1)
        return carry

    lax.fori_loop(0, n_blk // 2 - 1, pair, 0)
    step(0, None, None, n_blk - 1, n_blk - 2)
    step(1, None, None, None, n_blk - 1)


def _axial_rope_attention(proj, cos, sin, gq, gk, q_off, k_off, v_off, g_off):
    b, s, _ = proj.shape
    in_specs, out_spec = _gqa_specs(s, q_off, k_off, v_off, g_off)
    table = pl.BlockSpec((s, HEAD_DIM), lambda bi, hi: (0, 0))
    gain = pl.BlockSpec((1, HEAD_DIM), lambda bi, hi: (0, 0))
    stacked = GROUP * QBLK
    return pl.pallas_call(
        _axial_kernel,
        grid=(b, KV_HEADS),
        in_specs=in_specs + [table, table, gain, gain],
        out_specs=out_spec,
        out_shape=jax.ShapeDtypeStruct((b, s, MIX_W), jnp.bfloat16),
        scratch_shapes=[pltpu.VMEM((s, HEAD_DIM), jnp.bfloat16),
                        pltpu.VMEM((s, 2 * HEAD_DIM), jnp.bfloat16),
                        pltpu.VMEM((2, stacked, HEAD_DIM), jnp.bfloat16),
                        pltpu.VMEM((2, stacked, s), jnp.float32),
                        pltpu.VMEM((2, stacked, HEAD_DIM), jnp.float32),
                        pltpu.VMEM((2, stacked, s), jnp.bfloat16)],
        compiler_params=_params("parallel", "parallel"),
        name="axial_rope_attn",
    )(proj, proj, proj, proj, cos, sin, gq.reshape(1, HEAD_DIM), gk.reshape(1, HEAD_DIM))


def _alibi_slopes(n):
    return 2.0 ** (-8.0 * jnp.arange(1, n + 1, dtype=jnp.float32) / n)


def _q_column_scale(widths, q_sections):
    scale = np.ones(sum(widths), np.float32)
    for sec in q_sections:
        start = sum(widths[:sec])
        scale[start:start + widths[sec]] = SCALE * LOG2E
    return jnp.asarray(scale)


def kernel(x, norm_g, final_g, w_in_even, w_out_even, rpb_b, w_in_odd, w_out_odd, sinks_c, qnorm_d, knorm_d):
    b, s, d = x.shape
    bf16 = jnp.bfloat16
    slopes = _alibi_slopes(N_HEADS)
    cos, sin = _rope_tables(s)
    blk = lambda width: width // HEAD_DIM

    h = x.reshape(b * s, d)
    y = _rms_norm_rows(h, norm_g[0], bf16)
    for layer in range(DEPTH):
        i = layer // 2
        final = layer == DEPTH - 1
        g_next = final_g if final else norm_g[layer + 1]
        if layer % 2 == 0:
            widths = [MIX_W] * 8
            offs = [blk(MIX_W) * j for j in range(8)]
            proj = _in_proj(y, w_in_even, i, _q_column_scale(widths, (0, 4))).reshape(b, s, -1)
            mix1, mix2 = _even_mixers(proj, slopes, _na_bias_table(rpb_b[i], s // GRID_W), offs)
            w_out = w_out_even
        else:
            widths = [MIX_W, KV_W, KV_W, MIX_W] * 2
            offs = [blk(int(sum(widths[:j]))) for j in range(8)]
            proj = _in_proj(y, w_in_odd, i, _q_column_scale(widths, (0,))).reshape(b, s, -1)
            mix1 = _windowed_sink_attention(proj, slopes, sinks_c[i], *offs[:4])
            mix2 = _axial_rope_attention(proj, cos, sin, qnorm_d[i], knorm_d[i], *offs[4:])
            w_out = w_out_odd
        res = _out_proj(mix1.reshape(b * s, MIX_W), mix2.reshape(b * s, MIX_W), w_out, i, h, g_next, final)
        if final:
            return res.reshape(b, s, d)
        h, y = res
```

```python
import functools

import jax
import jax.numpy as jnp
import numpy as np
from jax import lax
from jax.experimental import pallas as pl
from jax.experimental.pallas import tpu as pltpu

D_MODEL = 2048
DEPTH = 4
HEAD_DIM = 128
GRID_W = 64
EPS = 1e-6
N_HEADS = 8
KV_HEADS = 2
GROUP = N_HEADS // KV_HEADS
MIX_W = N_HEADS * HEAD_DIM
KV_W = KV_HEADS * HEAD_DIM
A_PATTERNS = ((128, 1), (512, 4), (2048, 16))
A_SIDE = 64
NA_ROWS = 8
NA_COLS = 16
NA_ROWS_MAX = 8
C_WINDOW = 128
QBLK = 128
ROPE_THETA = 10000.0
SCALE = HEAD_DIM ** -0.5
LOG2E = 1.4426950408889634
MASKED = -1e30
FAR = 1e9
AHEAD = 6

VMEM_LIMIT = 56 * 1024 * 1024
IN_PROJ_ROWS = 2048
IN_PROJ_COLS = 1024
OUT_PROJ_ROWS = 512


def _params(*sem):
    return pltpu.CompilerParams(dimension_semantics=sem, vmem_limit_bytes=VMEM_LIMIT)


def _silu(x):
    half = 0.5 * x
    return half + half * jnp.tanh(half)


def _dot_nt(a, b):
    return lax.dot_general(a, b, (((1,), (1,)), ((), ())), preferred_element_type=jnp.float32)


def _run_pipelined(items):
    pending = [scores() for scores, _ in items[:AHEAD]]
    for idx, (_, finish) in enumerate(items):
        if idx + AHEAD < len(items):
            pending.append(items[idx + AHEAD][0]())
        finish(pending.pop(0))


def _interleave(a, b):
    keyed = [((i + 0.5) / len(a), 0, x) for i, x in enumerate(a)] + [((i + 0.5) / len(b), 1, x) for i, x in enumerate(b)]
    return [x for _, _, x in sorted(keyed, key=lambda t: t[:2])]


def _norm_kernel(h_ref, g_ref, y_ref):
    x = h_ref[...]
    y = x * lax.rsqrt(jnp.mean(x * x, axis=-1, keepdims=True) + EPS)
    y_ref[...] = (y * g_ref[...]).astype(y_ref.dtype)


def _rms_norm_rows(h, g, out_dtype, tm=1024):
    m, d = h.shape
    return pl.pallas_call(
        _norm_kernel,
        grid=(m // tm,),
        in_specs=[pl.BlockSpec((tm, d), lambda i: (i, 0)), pl.BlockSpec((1, d), lambda i: (0, 0))],
        out_specs=pl.BlockSpec((tm, d), lambda i: (i, 0)),
        out_shape=jax.ShapeDtypeStruct((m, d), out_dtype),
        compiler_params=_params("parallel"),
        name="rms_norm",
    )(h, g.reshape(1, d))


def _in_proj_kernel(x_ref, w_ref, scale_ref, o_ref, wb_ref):
    @pl.when(pl.program_id(1) == 0)
    def _():
        wb_ref[...] = (w_ref[...] * scale_ref[...]).astype(wb_ref.dtype)

    o_ref[...] = jnp.dot(x_ref[...], wb_ref[...], preferred_element_type=jnp.float32).astype(o_ref.dtype)


def _in_proj(y, w_stack, layer, col_scale, tm=IN_PROJ_ROWS, tn=IN_PROJ_COLS):
    m, k = y.shape
    n = w_stack.shape[2]
    return pl.pallas_call(
        _in_proj_kernel,
        grid=(n // tn, m // tm),
        in_specs=[pl.BlockSpec((tm, k), lambda j, i: (i, 0)),
                  pl.BlockSpec((None, k, tn), lambda j, i: (layer, 0, j)),
                  pl.BlockSpec((1, tn), lambda j, i: (0, j))],
        out_specs=pl.BlockSpec((tm, tn), lambda j, i: (i, j)),
        out_shape=jax.ShapeDtypeStruct((m, n), jnp.bfloat16),
        scratch_shapes=[pltpu.VMEM((k, tn), jnp.bfloat16)],
        compiler_params=_params("arbitrary", "arbitrary"),
        name="in_proj",
    )(y, w_stack, col_scale.reshape(1, n))


def _out_proj_kernel(m1_ref, m2_ref, w_ref, h_ref, g_ref, *refs, final):
    out_refs, wb_ref = refs[:-1], refs[-1]
    half = m1_ref.shape[1]

    @pl.when(pl.program_id(0) == 0)
    def _():
        wb_ref[...] = w_ref[...].astype(wb_ref.dtype)

    acc = jnp.dot(m1_ref[...], wb_ref[:half, :], preferred_element_type=jnp.float32)
    acc = acc + jnp.dot(m2_ref[...], wb_ref[half:, :], preferred_element_type=jnp.float32)
    h = h_ref[...] + acc
    y = h * lax.rsqrt(jnp.mean(h * h, axis=-1, keepdims=True) + EPS) * g_ref[...]
    if final:
        out_refs[0][...] = y
    else:
        out_refs[0][...] = h
        out_refs[1][...] = y.astype(out_refs[1].dtype)


def _out_proj(mix1, mix2, w_stack, layer, h, g_next, final, tm=OUT_PROJ_ROWS):
    m, d = h.shape
    half = mix1.shape[1]
    row = lambda i: (i, 0)
    if final:
        out_shape = jax.ShapeDtypeStruct((m, d), jnp.float32)
        out_specs = pl.BlockSpec((tm, d), row)
    else:
        out_shape = (jax.ShapeDtypeStruct((m, d), jnp.float32), jax.ShapeDtypeStruct((m, d), jnp.bfloat16))
        out_specs = (pl.BlockSpec((tm, d), row), pl.BlockSpec((tm, d), row))
    return pl.pallas_call(
        functools.partial(_out_proj_kernel, final=final),
        grid=(m // tm,),
        in_specs=[pl.BlockSpec((tm, half), row), pl.BlockSpec((tm, half), row),
                  pl.BlockSpec((None, 2 * half, d), lambda i: (layer, 0, 0), pipeline_mode=pl.Buffered(1)),
                  pl.BlockSpec((tm, d), row), pl.BlockSpec((1, d), lambda i: (0, 0))],
        out_specs=out_specs,
        out_shape=out_shape,
        scratch_shapes=[pltpu.VMEM((2 * half, d), jnp.bfloat16)],
        compiler_params=_params("arbitrary"),
        name="out_proj_final" if final else "out_proj",
    )(mix1, mix2, w_stack, h, g_next.reshape(1, d))


A_REGROUP = 4


def _dilated_items(slope, q_ref, k_ref, v_ref, xf, x4, qp, kp, vp, o_s, lse_s):
    assert tuple(d for _, d in A_PATTERNS) == (1, A_REGROUP, A_REGROUP ** 2)
    s_len = q_ref.shape[0]
    n4 = s_len // A_REGROUP
    n16 = n4 // A_REGROUP

    for a, (src, dst) in enumerate(((q_ref, qp), (k_ref, kp), (v_ref, vp))):
        xf[a] = src[...].astype(jnp.float32)
        for c4 in range(A_REGROUP):
            x = xf[a, pl.ds(c4, n4, stride=A_REGROUP), :]
            x4[a, c4 * n4:(c4 + 1) * n4, :] = x
            dst[0, c4 * n4:(c4 + 1) * n4, :] = x.astype(dst.dtype)
        for c4 in range(A_REGROUP):
            for c2 in range(A_REGROUP):
                c16 = A_REGROUP * c2 + c4
                x = x4[a, pl.ds(c4 * n4 + c2, n16, stride=A_REGROUP), :]
                dst[1, c16 * n16:(c16 + 1) * n16, :] = x.astype(dst.dtype)
    srcs = [(q_ref, k_ref, v_ref), (qp.at[0], kp.at[0], vp.at[0]), (qp.at[1], kp.at[1], vp.at[1])]

    tiles = []
    bias = {}
    for p_idx, (_, dil) in enumerate(A_PATTERNS):
        n = s_len // dil
        kw = min(2 * QBLK, n)
        rel0 = (lax.broadcasted_iota(jnp.int32, (QBLK, kw), 1)
                - lax.broadcasted_iota(jnp.int32, (QBLK, kw), 0))
        for t in range(s_len // QBLK):
            c, i0 = divmod(t * QBLK, n)
            w0 = min(max(i0 - A_SIDE, 0), n - kw)
            if (p_idx, w0 - i0) not in bias:
                dist = jnp.abs(rel0 + (w0 - i0)).astype(jnp.float32)
                bias[(p_idx, w0 - i0)] = (slope * (-dil * LOG2E)) * jnp.where(dist <= float(A_SIDE), dist, FAR)
            if p_idx < 2:
                out_rows = pl.ds(c * n + i0, QBLK)
            else:
                c2, c4 = divmod(c, A_REGROUP)
                out_rows = pl.ds(c4 * n4 + c2 + A_REGROUP * i0, QBLK, stride=A_REGROUP)
            tiles.append((p_idx, slice(c * n + i0, c * n + i0 + QBLK), slice(c * n + w0, c * n + w0 + kw),
                          w0 - i0, out_rows))

    def item(p_idx, q_rows, k_rows, off, out_rows):
        q_src, k_src, v_src = srcs[p_idx]

        def scores():
            return _dot_nt(q_src[q_rows, :], k_src[k_rows, :])

        def finish(s):
            v_win = v_src[k_rows, :]
            logits = bias[(p_idx, off)] + s
            m = jnp.max(logits, axis=-1, keepdims=True)
            p = jnp.exp2(logits - m)
            l = jnp.sum(p, axis=-1, keepdims=True)
            o_s[p_idx, out_rows, :] = jnp.dot(p.astype(v_win.dtype), v_win, preferred_element_type=jnp.float32) / l
            lse_s[p_idx, out_rows, :] = jnp.broadcast_to(m + jnp.log2(l), (QBLK, HEAD_DIM))

        return scores, finish

    return [item(*t) for t in tiles]


def _dilated_merge(g_ref, o_ref, o_s, lse_s, o_nat):
    n4 = o_ref.shape[0] // A_REGROUP
    for c4 in range(A_REGROUP):
        blk = slice(c4 * n4, (c4 + 1) * n4)
        nat = pl.ds(c4, n4, stride=A_REGROUP)
        lses = (lse_s[0, nat, :], lse_s[1, blk, :], lse_s[2, blk, :])
        outs = (o_s[0, nat, :], o_s[1, blk, :], o_s[2, blk, :])
        top = jnp.maximum(jnp.maximum(lses[0], lses[1]), lses[2])
        ws = [jnp.exp2(ls - top) for ls in lses]
        num = ws[0] * outs[0] + ws[1] * outs[1] + ws[2] * outs[2]
        o_nat[nat, :] = num / (ws[0] + ws[1] + ws[2])
    o_ref[...] = (o_nat[...] * _silu(g_ref[...].astype(jnp.float32))).astype(o_ref.dtype)


NA_VARIANTS = 8


def _na_variant_row(v, rows):
    return v if v <= 4 else rows - NA_VARIANTS + v


def _na_row_start(r, rows):
    kh = min(NA_ROWS, rows)
    return min(max(r - kh // 2, 0), rows - kh)


def _na_bias_kernel(rpb_ref, o_ref, pair_ref, *, rows):
    kh = min(NA_ROWS, rows)
    n_drow = 2 * NA_ROWS_MAX - 1
    n_dcol = 2 * NA_COLS - 1
    h = pl.program_id(0)
    shape = (GRID_W, 2 * GRID_W)
    lane = lax.broadcasted_iota(jnp.int32, shape, 1)
    c = lax.broadcasted_iota(jnp.int32, shape, 0)
    kc = lane % GRID_W
    second = lane >= GRID_W
    cs = jnp.clip(c - NA_COLS // 2, 0, GRID_W - NA_COLS)
    valid = (kc >= cs) & (kc < cs + NA_COLS)
    dcol = kc - c + (NA_COLS - 1)
    for d in range(n_drow - 1):
        acc = jnp.zeros(shape, jnp.float32)
        for j in range(n_dcol):
            lo = rpb_ref[(h * n_drow + d) * n_dcol + j]
            hi = rpb_ref[(h * n_drow + d + 1) * n_dcol + j]
            acc = jnp.where(dcol == j, jnp.where(second, hi, lo), acc)
        pair_ref[d] = jnp.where(valid, acc * LOG2E, MASKED)
    for v in range(NA_VARIANTS):
        r = _na_variant_row(v, rows)
        d0 = _na_row_start(r, rows) - r + (NA_ROWS_MAX - 1)
        for i in range(0, kh, 2):
            o_ref[v, :, i * GRID_W:(i + 2) * GRID_W] = pair_ref[d0 + i]


def _na_bias_table(rpb, rows):
    n_heads = rpb.shape[0]
    kh = min(NA_ROWS, rows)
    return pl.pallas_call(
        functools.partial(_na_bias_kernel, rows=rows),
        grid=(n_heads,),
        in_specs=[pl.BlockSpec(memory_space=pltpu.SMEM)],
        out_specs=pl.BlockSpec((None, NA_VARIANTS, GRID_W, kh * GRID_W), lambda hi: (hi, 0, 0, 0)),
        out_shape=jax.ShapeDtypeStruct((n_heads, NA_VARIANTS, GRID_W, kh * GRID_W), jnp.float32),
        scratch_shapes=[pltpu.VMEM((2 * NA_ROWS_MAX - 2, GRID_W, 2 * GRID_W), jnp.float32)],
        compiler_params=_params("parallel"),
        name="na_bias",
    )(rpb.reshape(-1))


def _na_items(q_ref, k_ref, v_ref, g_ref, bias_ref, o_ref):
    s_len = q_ref.shape[0]
    rows = s_len // GRID_W
    win = min(NA_ROWS, rows) * GRID_W

    def item(r):
        k_rows = slice(_na_row_start(r, rows) * GRID_W, _na_row_start(r, rows) * GRID_W + win)
        variant = r if r < 4 else (r - (rows - NA_VARIANTS) if r > rows - 4 else 4)
        sl = slice(r * GRID_W, (r + 1) * GRID_W)

        def scores():
            return _dot_nt(q_ref[sl, :], k_ref[k_rows, :])

        def finish(s):
            v_win = v_ref[k_rows, :]
            logits = bias_ref[variant] + s
            m = jnp.max(logits, axis=-1, keepdims=True)
            p = jnp.exp2(logits - m)
            den = jnp.sum(p, axis=-1, keepdims=True)
            o = jnp.dot(p.astype(v_win.dtype), v_win, preferred_element_type=jnp.float32) / den
            o_ref[sl, :] = (o * _silu(g_ref[sl, :].astype(jnp.float32))).astype(o_ref.dtype)

        return scores, finish

    return [item(r) for r in range(rows)]


def _even_mixers_kernel(slopes_ref, qa_ref, ka_ref, va_ref, ga_ref, qb_ref, kb_ref, vb_ref, gb_ref, bias_ref,
                        oa_ref, ob_ref, xf, x4, qp, kp, vp, o_s, lse_s, o_nat):
    a_items = _dilated_items(slopes_ref[pl.program_id(0)], qa_ref, ka_ref, va_ref, xf, x4, qp, kp, vp, o_s, lse_s)
    b_items = _na_items(qb_ref, kb_ref, vb_ref, gb_ref, bias_ref, ob_ref)
    _run_pipelined(_interleave(a_items, b_items))
    _dilated_merge(ga_ref, oa_ref, o_s, lse_s, o_nat)


def _even_mixers(proj, slopes, bias, offs):
    b, s, _ = proj.shape
    col = lambda off: pl.BlockSpec((None, s, HEAD_DIM), lambda hi, bi: (bi, 0, off + hi))
    out_spec = pl.BlockSpec((None, s, HEAD_DIM), lambda hi, bi: (bi, 0, hi))
    out_shape = jax.ShapeDtypeStruct((b, s, MIX_W), jnp.bfloat16)
    n_pat = len(A_PATTERNS)
    f32_rows = lambda n: pltpu.VMEM((n, s, HEAD_DIM), jnp.float32)
    return pl.pallas_call(
        _even_mixers_kernel,
        grid=(N_HEADS, b),
        in_specs=[pl.BlockSpec(memory_space=pltpu.SMEM)] + [col(off) for off in offs]
        + [pl.BlockSpec((None,) + bias.shape[1:], lambda hi, bi: (hi, 0, 0, 0))],
        out_specs=(out_spec, out_spec),
        out_shape=(out_shape, out_shape),
        scratch_shapes=[f32_rows(3), f32_rows(3)]
        + [pltpu.VMEM((n_pat - 1, s, HEAD_DIM), jnp.bfloat16)] * 3
        + [f32_rows(n_pat), f32_rows(n_pat), pltpu.VMEM((s, HEAD_DIM), jnp.float32)],
        compiler_params=_params("parallel", "parallel"),
        name="even_mixers",
    )(slopes, *([proj] * 8), bias)


def _window_kernel(slopes_ref, sinks_ref, q_ref, k_ref, v_ref, g_ref, o_ref, v1):
    s_len = q_ref.shape[0]
    v1[:, :HEAD_DIM] = v_ref[...]
    v1[:, HEAD_DIM:] = jnp.ones((s_len, HEAD_DIM), v1.dtype)
    hk = pl.program_id(1)
    kw = 3 * QBLK
    rel0 = (lax.broadcasted_iota(jnp.int32, (QBLK, kw), 1) - lax.broadcasted_iota(jnp.int32, (QBLK, kw), 0))
    slopes = [slopes_ref[hk * GROUP + gi] for gi in range(GROUP)]
    sinks = [sinks_ref[hk * GROUP + gi] for gi in range(GROUP)]
    sinks = [sink * LOG2E for sink in sinks]
    bias = {}
    tiles = []
    for n in range(s_len // QBLK):
        q0 = n * QBLK
        w0 = min(max(q0 - QBLK, 0), s_len - kw)
        if (w0 - q0, 0) not in bias:
            dist = jnp.abs(rel0 + (w0 - q0)).astype(jnp.float32)
            dist = jnp.where(dist <= float(C_WINDOW), dist, FAR)
            for gi in range(GROUP):
                bias[(w0 - q0, gi)] = (slopes[gi] * -LOG2E) * dist
        tiles += [(slice(q0, q0 + QBLK), gi, slice(w0, w0 + kw), w0 - q0) for gi in range(GROUP)]

    def item(q_rows, gi, k_rows, off):
        lanes = slice(gi * HEAD_DIM, (gi + 1) * HEAD_DIM)

        def scores():
            return _dot_nt(q_ref[q_rows, lanes], k_ref[k_rows, :])

        def finish(s):
            v_win = v1[k_rows, :]
            logits = bias[(off, gi)] + s
            m = jnp.maximum(jnp.max(logits, axis=-1, keepdims=True), sinks[gi])
            p = jnp.exp2((logits - m).astype(v_win.dtype))
            acc = jnp.dot(p, v_win, preferred_element_type=jnp.float32)
            o = acc[:, :HEAD_DIM] / (acc[:, HEAD_DIM:] + jnp.exp2(sinks[gi] - m))
            gate = _silu(g_ref[q_rows, lanes].astype(jnp.float32))
            o_ref[q_rows, lanes] = (o * gate).astype(o_ref.dtype)

        return scores, finish

    _run_pipelined([item(*t) for t in tiles])


def _gqa_specs(s, q_off, k_off, v_off, g_off):
    gw = GROUP * HEAD_DIM
    wide = lambda off: pl.BlockSpec((None, s, gw), lambda bi, hi: (bi, 0, off // GROUP + hi))
    col = lambda off: pl.BlockSpec((None, s, HEAD_DIM), lambda bi, hi: (bi, 0, off + hi))
    in_specs = [wide(q_off), col(k_off), col(v_off), wide(g_off)]
    out_spec = pl.BlockSpec((None, s, gw), lambda bi, hi: (bi, 0, hi))
    return in_specs, out_spec


def _windowed_sink_attention(proj, slopes, sinks, q_off, k_off, v_off, g_off):
    b, s, _ = proj.shape
    in_specs, out_spec = _gqa_specs(s, q_off, k_off, v_off, g_off)
    smem = pl.BlockSpec(memory_space=pltpu.SMEM)
    return pl.pallas_call(
        _window_kernel,
        grid=(b, KV_HEADS),
        in_specs=[smem, smem] + in_specs,
        out_specs=out_spec,
        out_shape=jax.ShapeDtypeStruct((b, s, MIX_W), jnp.bfloat16),
        scratch_shapes=[pltpu.VMEM((s, 2 * HEAD_DIM), jnp.bfloat16)],
        compiler_params=_params("parallel", "parallel"),
        name="window_sink_attn",
    )(slopes, sinks, proj, proj, proj, proj)


def _rope_tables(s):
    t = np.arange(s)
    quarter = HEAD_DIM // 4
    inv = jnp.asarray(ROPE_THETA, jnp.float32) ** (-jnp.arange(quarter, dtype=jnp.float32) / quarter)
    ang_r = jnp.asarray(t // GRID_W, jnp.float32)[:, None] * inv[None, :]
    ang_c = jnp.asarray(t % GRID_W, jnp.float32)[:, None] * inv[None, :]
    cos = jnp.concatenate([jnp.cos(ang_r), jnp.cos(ang_r), jnp.cos(ang_c), jnp.cos(ang_c)], axis=-1)
    sin = jnp.concatenate([-jnp.sin(ang_r), jnp.sin(ang_r), -jnp.sin(ang_c), jnp.sin(ang_c)], axis=-1)
    return cos, sin


def _norm_rope(x, gain, cos, sin):
    y = x * lax.rsqrt(jnp.mean(x * x, axis=-1, keepdims=True) + EPS) * gain
    quarter = HEAD_DIM // 4
    lane = lax.broadcasted_iota(jnp.int32, y.shape, 1)
    partner = jnp.where((lane // quarter) % 2 == 0,
                        pltpu.roll(y, HEAD_DIM - quarter, 1),
                        pltpu.roll(y, quarter, 1))
    return y * cos + partner * sin


D_ROWS_PER_DOT = 256


def _axial_kernel(q_ref, k_ref, v_ref, g_ref, cos_ref, sin_ref, gq_ref, gk_ref, o_ref,
                  kp, v1, q4, s_buf, m_buf, p_buf):
    s_len = q_ref.shape[0]
    n_blk = s_len // QBLK
    row_groups = [slice(r, r + D_ROWS_PER_DOT) for r in range(0, GROUP * QBLK, D_ROWS_PER_DOT)]
    kp[...] = _norm_rope(k_ref[...].astype(jnp.float32), gk_ref[...], cos_ref[...], sin_ref[...]).astype(kp.dtype)
    v1[:, :HEAD_DIM] = v_ref[...]
    v1[:, HEAD_DIM:] = jnp.ones((s_len, HEAD_DIM), v1.dtype)

    def block_rows(n):
        return pl.ds(pl.multiple_of(n * QBLK, QBLK), QBLK)

    def step(slot, n_prep, n_scores, n_probs, n_out):
        other = 1 - slot
        if n_prep is not None:
            rows = block_rows(n_prep)
            cos = cos_ref[rows, :]
            sin = sin_ref[rows, :]
            for gi in range(GROUP):
                q = _norm_rope(q_ref[rows, gi * HEAD_DIM:(gi + 1) * HEAD_DIM].astype(jnp.float32), gq_ref[...],
                               cos, sin)
                q4[other, gi * QBLK:(gi + 1) * QBLK, :] = (q * (SCALE * LOG2E)).astype(q4.dtype)
        if n_scores is not None:
            for hs in row_groups:
                s = _dot_nt(q4[slot, hs, :], kp[...])
                s_buf[slot, hs, :] = s
                m_buf[slot, hs, :] = jnp.broadcast_to(jnp.max(s, axis=-1, keepdims=True), (D_ROWS_PER_DOT, HEAD_DIM))
        if n_probs is not None:
            for hs in row_groups:
                x = s_buf[other, hs, :] - jnp.tile(m_buf[other, hs, :], (1, s_len // HEAD_DIM))
                p_buf[other, hs, :] = jnp.exp2(x.astype(p_buf.dtype))
        if n_out is not None:
            rows = block_rows(n_out)
            for hs in row_groups:
                acc = jnp.dot(p_buf[slot, hs, :], v1[...], preferred_element_type=jnp.float32)
                o = acc[:, :HEAD_DIM] / acc[:, HEAD_DIM:]
                for gi in range(hs.start // QBLK, hs.stop // QBLK):
                    lanes = slice(gi * HEAD_DIM, (gi + 1) * HEAD_DIM)
                    gate = _silu(g_ref[rows, lanes].astype(jnp.float32))
                    o_ref[rows, lanes] = (o[gi * QBLK - hs.start:(gi + 1) * QBLK - hs.start, :] * gate).astype(o_ref.dtype)

    step(1, 0, None, None, None)
    step(0, 1, 0, None, None)
    step(1, 2, 1, 0, None)

    def pair(i, carry):
        n = 2 * i
        step(0, n + 3, n + 2, n + 1, n)
        step(1, jnp.minimum(n + 4, n_blk - 1), n + 3, n + 2, n + 1)
        return carry

    lax.fori_loop(0, n_blk // 2 - 1, pair, 0)
    step(0, None, None, n_blk - 1, n_blk - 2)
    step(1, None, None, None, n_blk - 1)


def _axial_rope_attention(proj, cos, sin, gq, gk, q_off, k_off, v_off, g_off):
    b, s, _ = proj.shape
    in_specs, out_spec = _gqa_specs(s, q_off, k_off, v_off, g_off)
    table = pl.BlockSpec((s, HEAD_DIM), lambda bi, hi: (0, 0))
    gain = pl.BlockSpec((1, HEAD_DIM), lambda bi, hi: (0, 0))
    stacked = GROUP * QBLK
    return pl.pallas_call(
        _axial_kernel,
        grid=(b, KV_HEADS),
        in_specs=in_specs + [table, table, gain, gain],
        out_specs=out_spec,
        out_shape=jax.ShapeDtypeStruct((b, s, MIX_W), jnp.bfloat16),
        scratch_shapes=[pltpu.VMEM((s, HEAD_DIM), jnp.bfloat16),
                        pltpu.VMEM((s, 2 * HEAD_DIM), jnp.bfloat16),
                        pltpu.VMEM((2, stacked, HEAD_DIM), jnp.bfloat16),
                        pltpu.VMEM((2, stacked, s), jnp.float32),
                        pltpu.VMEM((2, stacked, HEAD_DIM), jnp.float32),
                        pltpu.VMEM((2, stacked, s), jnp.bfloat16)],
        compiler_params=_params("parallel", "parallel"),
        name="axial_rope_attn",
    )(proj, proj, proj, proj, cos, sin, gq.reshape(1, HEAD_DIM), gk.reshape(1, HEAD_DIM))


def _alibi_slopes(n):
    return 2.0 ** (-8.0 * jnp.arange(1, n + 1, dtype=jnp.float32) / n)


def _q_column_scale(widths, q_sections):
    scale = np.ones(sum(widths), np.float32)
    for sec in q_sections:
        start = sum(widths[:sec])
        scale[start:start + widths[sec]] = SCALE * LOG2E
    return jnp.asarray(scale)


def kernel(x, norm_g, final_g, w_in_even, w_out_even, rpb_b, w_in_odd, w_out_odd, sinks_c, qnorm_d, knorm_d):
    b, s, d = x.shape
    bf16 = jnp.bfloat16
    slopes = _alibi_slopes(N_HEADS)
    cos, sin = _rope_tables(s)
    blk = lambda width: width // HEAD_DIM

    h = x.reshape(b * s, d)
    y = _rms_norm_rows(h, norm_g[0], bf16)
    for layer in range(DEPTH):
        i = layer // 2
        final = layer == DEPTH - 1
        g_next = final_g if final else norm_g[layer + 1]
        if layer % 2 == 0:
            widths = [MIX_W] * 8
            offs = [blk(MIX_W) * j for j in range(8)]
            proj = _in_proj(y, w_in_even, i, _q_column_scale(widths, (0, 4))).reshape(b, s, -1)
            mix1, mix2 = _even_mixers(proj, slopes, _na_bias_table(rpb_b[i], s // GRID_W), offs)
            w_out = w_out_even
        else:
            widths = [MIX_W, KV_W, KV_W, MIX_W] * 2
            offs = [blk(int(sum(widths[:j]))) for j in range(8)]
            proj = _in_proj(y, w_in_odd, i, _q_column_scale(widths, (0,))).reshape(b, s, -1)
            mix1 = _windowed_sink_attention(proj, slopes, sinks_c[i], *offs[:4])
            mix2 = _axial_rope_attention(proj, cos, sin, qnorm_d[i], knorm_d[i], *offs[4:])
            w_out = w_out_odd
        res = _out_proj(mix1.reshape(b * s, MIX_W), mix2.reshape(b * s, MIX_W), w_out, i, h, g_next, final)
        if final:
            return res.reshape(b, s, d)
        h, y = res
```

```python
import functools

import jax
import jax.numpy as jnp
import numpy as np
from jax import lax
from jax.experimental import pallas as pl
from jax.experimental.pallas import tpu as pltpu

D_MODEL = 2048
DEPTH = 4
HEAD_DIM = 128
GRID_W = 64
EPS = 1e-6
N_HEADS = 8
KV_HEADS = 2
GROUP = N_HEADS // KV_HEADS
MIX_W = N_HEADS * HEAD_DIM
KV_W = KV_HEADS * HEAD_DIM
A_PATTERNS = ((128, 1), (512, 4), (2048, 16))
A_SIDE = 64
NA_ROWS = 8
NA_COLS = 16
NA_ROWS_MAX = 8
C_WINDOW = 128
QBLK = 128
ROPE_THETA = 10000.0
SCALE = HEAD_DIM ** -0.5
LOG2E = 1.4426950408889634
MASKED = -1e30
FAR = 1e9
AHEAD = 6

VMEM_LIMIT = 56 * 1024 * 1024
IN_PROJ_ROWS = 2048
IN_PROJ_COLS = 1024
OUT_PROJ_ROWS = 512


def _params(*sem):
    return pltpu.CompilerParams(dimension_semantics=sem, vmem_limit_bytes=VMEM_LIMIT)


def _silu(x):
    half = 0.5 * x
    return half + half * jnp.tanh(half)


def _dot_nt(a, b):
    return lax.dot_general(a, b, (((1,), (1,)), ((), ())), preferred_element_type=jnp.float32)


def _run_pipelined(items):
    pending = [scores() for scores, _ in items[:AHEAD]]
    for idx, (_, finish) in enumerate(items):
        if idx + AHEAD < len(items):
            pending.append(items[idx + AHEAD][0]())
        finish(pending.pop(0))


def _interleave(a, b):
    keyed = [((i + 0.5) / len(a), 0, x) for i, x in enumerate(a)] + [((i + 0.5) / len(b), 1, x) for i, x in enumerate(b)]
    return [x for _, _, x in sorted(keyed, key=lambda t: t[:2])]


def _norm_kernel(h_ref, g_ref, y_ref):
    x = h_ref[...]
    y = x * lax.rsqrt(jnp.mean(x * x, axis=-1, keepdims=True) + EPS)
    y_ref[...] = (y * g_ref[...]).astype(y_ref.dtype)


def _rms_norm_rows(h, g, out_dtype, tm=1024):
    m, d = h.shape
    return pl.pallas_call(
        _norm_kernel,
        grid=(m // tm,),
        in_specs=[pl.BlockSpec((tm, d), lambda i: (i, 0)), pl.BlockSpec((1, d), lambda i: (0, 0))],
        out_specs=pl.BlockSpec((tm, d), lambda i: (i, 0)),
        out_shape=jax.ShapeDtypeStruct((m, d), out_dtype),
        compiler_params=_params("parallel"),
        name="rms_norm",
    )(h, g.reshape(1, d))


def _in_proj_kernel(x_ref, w_ref, scale_ref, o_ref, wb_ref):
    @pl.when(pl.program_id(1) == 0)
    def _():
        wb_ref[...] = (w_ref[...] * scale_ref[...]).astype(wb_ref.dtype)

    o_ref[...] = jnp.dot(x_ref[...], wb_ref[...], preferred_element_type=jnp.float32).astype(o_ref.dtype)


def _in_proj(y, w_stack, layer, col_scale, tm=IN_PROJ_ROWS, tn=IN_PROJ_COLS):
    m, k = y.shape
    n = w_stack.shape[2]
    return pl.pallas_call(
        _in_proj_kernel,
        grid=(n // tn, m // tm),
        in_specs=[pl.BlockSpec((tm, k), lambda j, i: (i, 0)),
                  pl.BlockSpec((None, k, tn), lambda j, i: (layer, 0, j)),
                  pl.BlockSpec((1, tn), lambda j, i: (0, j))],
        out_specs=pl.BlockSpec((tm, tn), lambda j, i: (i, j)),
        out_shape=jax.ShapeDtypeStruct((m, n), jnp.bfloat16),
        scratch_shapes=[pltpu.VMEM((k, tn), jnp.bfloat16)],
        compiler_params=_params("arbitrary", "arbitrary"),
        name="in_proj",
    )(y, w_stack, col_scale.reshape(1, n))


def _out_proj_kernel(m1_ref, m2_ref, w_ref, h_ref, g_ref, *refs, final):
    out_refs, wb_ref = refs[:-1], refs[-1]
    half = m1_ref.shape[1]

    @pl.when(pl.program_id(0) == 0)
    def _():
        wb_ref[...] = w_ref[...].astype(wb_ref.dtype)

    acc = jnp.dot(m1_ref[...], wb_ref[:half, :], preferred_element_type=jnp.float32)
    acc = acc + jnp.dot(m2_ref[...], wb_ref[half:, :], preferred_element_type=jnp.float32)
    h = h_ref[...] + acc
    y = h * lax.rsqrt(jnp.mean(h * h, axis=-1, keepdims=True) + EPS) * g_ref[...]
    if final:
        out_refs[0][...] = y
    else:
        out_refs[0][...] = h
        out_refs[1][...] = y.astype(out_refs[1].dtype)


def _out_proj(mix1, mix2, w_stack, layer, h, g_next, final, tm=OUT_PROJ_ROWS):
    m, d = h.shape
    half = mix1.shape[1]
    row = lambda i: (i, 0)
    if final:
        out_shape = jax.ShapeDtypeStruct((m, d), jnp.float32)
        out_specs = pl.BlockSpec((tm, d), row)
    else:
        out_shape = (jax.ShapeDtypeStruct((m, d), jnp.float32), jax.ShapeDtypeStruct((m, d), jnp.bfloat16))
        out_specs = (pl.BlockSpec((tm, d), row), pl.BlockSpec((tm, d), row))
    return pl.pallas_call(
        functools.partial(_out_proj_kernel, final=final),
        grid=(m // tm,),
        in_specs=[pl.BlockSpec((tm, half), row), pl.BlockSpec((tm, half), row),
                  pl.BlockSpec((None, 2 * half, d), lambda i: (layer, 0, 0), pipeline_mode=pl.Buffered(1)),
                  pl.BlockSpec((tm, d), row), pl.BlockSpec((1, d), lambda i: (0, 0))],
        out_specs=out_specs,
        out_shape=out_shape,
        scratch_shapes=[pltpu.VMEM((2 * half, d), jnp.bfloat16)],
        compiler_params=_params("arbitrary"),
        name="out_proj_final" if final else "out_proj",
    )(mix1, mix2, w_stack, h, g_next.reshape(1, d))


A_REGROUP = 4


def _dilated_items(slope, q_ref, k_ref, v_ref, xf, x4, qp, kp, vp, o_s, lse_s):
    assert tuple(d for _, d in A_PATTERNS) == (1, A_REGROUP, A_REGROUP ** 2)
    s_len = q_ref.shape[0]
    n4 = s_len // A_REGROUP
    n16 = n4 // A_REGROUP

    for a, (src, dst) in enumerate(((q_ref, qp), (k_ref, kp), (v_ref, vp))):
        xf[a] = src[...].astype(jnp.float32)
        for c4 in range(A_REGROUP):
            x = xf[a, pl.ds(c4, n4, stride=A_REGROUP), :]
            x4[a, c4 * n4:(c4 + 1) * n4, :] = x
            dst[0, c4 * n4:(c4 + 1) * n4, :] = x.astype(dst.dtype)
        for c4 in range(A_REGROUP):
            for c2 in range(A_REGROUP):
                c16 = A_REGROUP * c2 + c4
                x = x4[a, pl.ds(c4 * n4 + c2, n16, stride=A_REGROUP), :]
                dst[1, c16 * n16:(c16 + 1) * n16, :] = x.astype(dst.dtype)
    srcs = [(q_ref, k_ref, v_ref), (qp.at[0], kp.at[0], vp.at[0]), (qp.at[1], kp.at[1], vp.at[1])]

    tiles = []
    bias = {}
    for p_idx, (_, dil) in enumerate(A_PATTERNS):
        n = s_len // dil
        kw = min(2 * QBLK, n)
        rel0 = (lax.broadcasted_iota(jnp.int32, (QBLK, kw), 1)
                - lax.broadcasted_iota(jnp.int32, (QBLK, kw), 0))
        for t in range(s_len // QBLK):
            c, i0 = divmod(t * QBLK, n)
            w0 = min(max(i0 - A_SIDE, 0), n - kw)
            if (p_idx, w0 - i0) not in bias:
                dist = jnp.abs(rel0 + (w0 - i0)).astype(jnp.float32)
                bias[(p_idx, w0 - i0)] = (slope * (-dil * LOG2E)) * jnp.where(dist <= float(A_SIDE), dist, FAR)
            if p_idx < 2:
                out_rows = pl.ds(c * n + i0, QBLK)
            else:
                c2, c4 = divmod(c, A_REGROUP)
                out_rows = pl.ds(c4 * n4 + c2 + A_REGROUP * i0, QBLK, stride=A_REGROUP)
            tiles.append((p_idx, slice(c * n + i0, c * n + i0 + QBLK), slice(c * n + w0, c * n + w0 + kw),
                          w0 - i0, out_rows))

    def item(p_idx, q_rows, k_rows, off, out_rows):
        q_src, k_src, v_src = srcs[p_idx]

        def scores():
            return _dot_nt(q_src[q_rows, :], k_src[k_rows, :])

        def finish(s):
            v_win = v_src[k_rows, :]
            logits = bias[(p_idx, off)] + s
            m = jnp.max(logits, axis=-1, keepdims=True)
            p = jnp.exp2(logits - m)
            l = jnp.sum(p, axis=-1, keepdims=True)
            o_s[p_idx, out_rows, :] = jnp.dot(p.astype(v_win.dtype), v_win, preferred_element_type=jnp.float32) / l
            lse_s[p_idx, out_rows, :] = jnp.broadcast_to(m + jnp.log2(l), (QBLK, HEAD_DIM))

        return scores, finish

    return [item(*t) for t in tiles]


def _dilated_merge(g_ref, o_ref, o_s, lse_s, o_nat):
    n4 = o_ref.shape[0] // A_REGROUP
    for c4 in range(A_REGROUP):
        blk = slice(c4 * n4, (c4 + 1) * n4)
        nat = pl.ds(c4, n4, stride=A_REGROUP)
        lses = (lse_s[0, nat, :], lse_s[1, blk, :], lse_s[2, blk, :])
        outs = (o_s[0, nat, :], o_s[1, blk, :], o_s[2, blk, :])
        top = jnp.maximum(jnp.maximum(lses[0], lses[1]), lses[2])
        ws = [jnp.exp2(ls - top) for ls in lses]
        num = ws[0] * outs[0] + ws[1] * outs[1] + ws[2] * outs[2]
        o_nat[nat, :] = num / (ws[0] + ws[1] + ws[2])
    o_ref[...] = (o_nat[...] * _silu(g_ref[...].astype(jnp.float32))).astype(o_ref.dtype)


NA_VARIANTS = 8


def _na_variant_row(v, rows):
    return v if v <= 4 else rows - NA_VARIANTS + v


def _na_row_start(r, rows):
    kh = min(NA_ROWS, rows)
    return min(max(r - kh // 2, 0), rows - kh)


def _na_bias_kernel(rpb_ref, o_ref, pair_ref, *, rows):
    kh = min(NA_ROWS, rows)
    n_drow = 2 * NA_ROWS_MAX - 1
    n_dcol = 2 * NA_COLS - 1
    h = pl.program_id(0)
    shape = (GRID_W, 2 * GRID_W)
    lane = lax.broadcasted_iota(jnp.int32, shape, 1)
    c = lax.broadcasted_iota(jnp.int32, shape, 0)
    kc = lane % GRID_W
    second = lane >= GRID_W
    cs = jnp.clip(c - NA_COLS // 2, 0, GRID_W - NA_COLS)
    valid = (kc >= cs) & (kc < cs + NA_COLS)
    dcol = kc - c + (NA_COLS - 1)
    for d in range(n_drow - 1):
        acc = jnp.zeros(shape, jnp.float32)
        for j in range(n_dcol):
            lo = rpb_ref[(h * n_drow + d) * n_dcol + j]
            hi = rpb_ref[(h * n_drow + d + 1) * n_dcol + j]
            acc = jnp.where(dcol == j, jnp.where(second, hi, lo), acc)
        pair_ref[d] = jnp.where(valid, acc * LOG2E, MASKED)
    for v in range(NA_VARIANTS):
        r = _na_variant_row(v, rows)
        d0 = _na_row_start(r, rows) - r + (NA_ROWS_MAX - 1)
        for i in range(0, kh, 2):
            o_ref[v, :, i * GRID_W:(i + 2) * GRID_W] = pair_ref[d0 + i]


def _na_bias_table(rpb, rows):
    n_heads = rpb.shape[0]
    kh = min(NA_ROWS, rows)
    return pl.pallas_call(
        functools.partial(_na_bias_kernel, rows=rows),
        grid=(n_heads,),
        in_specs=[pl.BlockSpec(memory_space=pltpu.SMEM)],
        out_specs=pl.BlockSpec((None, NA_VARIANTS, GRID_W, kh * GRID_W), lambda hi: (hi, 0, 0, 0)),
        out_shape=jax.ShapeDtypeStruct((n_heads, NA_VARIANTS, GRID_W, kh * GRID_W), jnp.float32),
        scratch_shapes=[pltpu.VMEM((2 * NA_ROWS_MAX - 2, GRID_W, 2 * GRID_W), jnp.float32)],
        compiler_params=_params("parallel"),
        name="na_bias",
    )(rpb.reshape(-1))


def _na_items(q_ref, k_ref, v_ref, g_ref, bias_ref, o_ref):
    s_len = q_ref.shape[0]
    rows = s_len // GRID_W
    win = min(NA_ROWS, rows) * GRID_W

    def item(r):
        k_rows = slice(_na_row_start(r, rows) * GRID_W, _na_row_start(r, rows) * GRID_W + win)
        variant = r if r < 4 else (r - (rows - NA_VARIANTS) if r > rows - 4 else 4)
        sl = slice(r * GRID_W, (r + 1) * GRID_W)

        def scores():
            return _dot_nt(q_ref[sl, :], k_ref[k_rows, :])

        def finish(s):
            v_win = v_ref[k_rows, :]
            logits = bias_ref[variant] + s
            m = jnp.max(logits, axis=-1, keepdims=True)
            p = jnp.exp2(logits - m)
            den = jnp.sum(p, axis=-1, keepdims=True)
            o = jnp.dot(p.astype(v_win.dtype), v_win, preferred_element_type=jnp.float32) / den
            o_ref[sl, :] = (o * _silu(g_ref[sl, :].astype(jnp.float32))).astype(o_ref.dtype)

        return scores, finish

    return [item(r) for r in range(rows)]


def _even_mixers_kernel(slopes_ref, qa_ref, ka_ref, va_ref, ga_ref, qb_ref, kb_ref, vb_ref, gb_ref, bias_ref,
                        oa_ref, ob_ref, xf, x4, qp, kp, vp, o_s, lse_s, o_nat):
    a_items = _dilated_items(slopes_ref[pl.program_id(0)], qa_ref, ka_ref, va_ref, xf, x4, qp, kp, vp, o_s, lse_s)
    b_items = _na_items(qb_ref, kb_ref, vb_ref, gb_ref, bias_ref, ob_ref)
    _run_pipelined(_interleave(a_items, b_items))
    _dilated_merge(ga_ref, oa_ref, o_s, lse_s, o_nat)


def _even_mixers(proj, slopes, bias, offs):
    b, s, _ = proj.shape
    col = lambda off: pl.BlockSpec((None, s, HEAD_DIM), lambda hi, bi: (bi, 0, off + hi))
    out_spec = pl.BlockSpec((None, s, HEAD_DIM), lambda hi, bi: (bi, 0, hi))
    out_shape = jax.ShapeDtypeStruct((b, s, MIX_W), jnp.bfloat16)
    n_pat = len(A_PATTERNS)
    f32_rows = lambda n: pltpu.VMEM((n, s, HEAD_DIM), jnp.float32)
    return pl.pallas_call(
        _even_mixers_kernel,
        grid=(N_HEADS, b),
        in_specs=[pl.BlockSpec(memory_space=pltpu.SMEM)] + [col(off) for off in offs]
        + [pl.BlockSpec((None,) + bias.shape[1:], lambda hi, bi: (hi, 0, 0, 0))],
        out_specs=(out_spec, out_spec),
        out_shape=(out_shape, out_shape),
        scratch_shapes=[f32_rows(3), f32_rows(3)]
        + [pltpu.VMEM((n_pat - 1, s, HEAD_DIM), jnp.bfloat16)] * 3
        + [f32_rows(n_pat), f32_rows(n_pat), pltpu.VMEM((s, HEAD_DIM), jnp.float32)],
        compiler_params=_params("parallel", "parallel"),
        name="even_mixers",
    )(slopes, *([proj] * 8), bias)


def _window_kernel(slopes_ref, sinks_ref, q_ref, k_ref, v_ref, g_ref, o_ref, v1, kt):
    s_len = q_ref.shape[0]
    v1[:, :HEAD_DIM] = v_ref[...]
    v1[:, HEAD_DIM:] = jnp.ones((s_len, HEAD_DIM), v1.dtype)
    kt[...] = k_ref[...].astype(jnp.float32).T.astype(kt.dtype)
    hk = pl.program_id(1)
    kw = 3 * QBLK
    rel0 = (lax.broadcasted_iota(jnp.int32, (QBLK, kw), 1) - lax.broadcasted_iota(jnp.int32, (QBLK, kw), 0))
    slopes = [slopes_ref[hk * GROUP + gi] for gi in range(GROUP)]
    sinks = [sinks_ref[hk * GROUP + gi] for gi in range(GROUP)]
    sinks = [sink * LOG2E for sink in sinks]
    bias = {}
    tiles = []
    for n in range(s_len // QBLK):
        q0 = n * QBLK
        w0 = min(max(q0 - QBLK, 0), s_len - kw)
        if (w0 - q0, 0) not in bias:
            dist = jnp.abs(rel0 + (w0 - q0)).astype(jnp.float32)
            dist = jnp.where(dist <= float(C_WINDOW), dist, FAR)
            for gi in range(GROUP):
                bias[(w0 - q0, gi)] = (slopes[gi] * -LOG2E) * dist
        tiles += [(slice(q0, q0 + QBLK), gi, slice(w0, w0 + kw), w0 - q0) for gi in range(GROUP)]

    def item(q_rows, gi, k_rows, off):
        lanes = slice(gi * HEAD_DIM, (gi + 1) * HEAD_DIM)

        def scores():
            return jnp.dot(q_ref[q_rows, lanes], kt[:, k_rows], preferred_element_type=jnp.float32)

        def finish(s):
            v_win = v1[k_rows, :]
            logits = bias[(off, gi)] + s
            m = jnp.maximum(jnp.max(logits, axis=-1, keepdims=True), sinks[gi])
            p = jnp.exp2((logits - m).astype(v_win.dtype))
            acc = jnp.dot(p, v_win, preferred_element_type=jnp.float32)
            o = acc[:, :HEAD_DIM] / (acc[:, HEAD_DIM:] + jnp.exp2(sinks[gi] - m))
            gate = _silu(g_ref[q_rows, lanes].astype(jnp.float32))
            o_ref[q_rows, lanes] = (o * gate).astype(o_ref.dtype)

        return scores, finish

    _run_pipelined([item(*t) for t in tiles])


def _gqa_specs(s, q_off, k_off, v_off, g_off):
    gw = GROUP * HEAD_DIM
    wide = lambda off: pl.BlockSpec((None, s, gw), lambda bi, hi: (bi, 0, off // GROUP + hi))
    col = lambda off: pl.BlockSpec((None, s, HEAD_DIM), lambda bi, hi: (bi, 0, off + hi))
    in_specs = [wide(q_off), col(k_off), col(v_off), wide(g_off)]
    out_spec = pl.BlockSpec((None, s, gw), lambda bi, hi: (bi, 0, hi))
    return in_specs, out_spec


def _windowed_sink_attention(proj, slopes, sinks, q_off, k_off, v_off, g_off):
    b, s, _ = proj.shape
    in_specs, out_spec = _gqa_specs(s, q_off, k_off, v_off, g_off)
    smem = pl.BlockSpec(memory_space=pltpu.SMEM)
    return pl.pallas_call(
        _window_kernel,
        grid=(b, KV_HEADS),
        in_specs=[smem, smem] + in_specs,
        out_specs=out_spec,
        out_shape=jax.ShapeDtypeStruct((b, s, MIX_W), jnp.bfloat16),
        scratch_shapes=[pltpu.VMEM((s, 2 * HEAD_DIM), jnp.bfloat16),
                        pltpu.VMEM((HEAD_DIM, s), jnp.bfloat16)],
        compiler_params=_params("parallel", "parallel"),
        name="window_sink_attn",
    )(slopes, sinks, proj, proj, proj, proj)


def _rope_tables(s):
    t = np.arange(s)
    quarter = HEAD_DIM // 4
    inv = jnp.asarray(ROPE_THETA, jnp.float32) ** (-jnp.arange(quarter, dtype=jnp.float32) / quarter)
    ang_r = jnp.asarray(t // GRID_W, jnp.float32)[:, None] * inv[None, :]
    ang_c = jnp.asarray(t % GRID_W, jnp.float32)[:, None] * inv[None, :]
    cos = jnp.concatenate([jnp.cos(ang_r), jnp.cos(ang_r), jnp.cos(ang_c), jnp.cos(ang_c)], axis=-1)
    sin = jnp.concatenate([-jnp.sin(ang_r), jnp.sin(ang_r), -jnp.sin(ang_c), jnp.sin(ang_c)], axis=-1)
    return cos, sin


def _norm_rope(x, gain, cos, sin):
    y = x * lax.rsqrt(jnp.mean(x * x, axis=-1, keepdims=True) + EPS) * gain
    quarter = HEAD_DIM // 4
    lane = lax.broadcasted_iota(jnp.int32, y.shape, 1)
    partner = jnp.where((lane // quarter) % 2 == 0,
                        pltpu.roll(y, HEAD_DIM - quarter, 1),
                        pltpu.roll(y, quarter, 1))
    return y * cos + partner * sin


D_ROWS_PER_DOT = 256


def _axial_kernel(q_ref, k_ref, v_ref, g_ref, cos_ref, sin_ref, gq_ref, gk_ref, o_ref,
                  kp, v1, q4, s_buf, m_buf, p_buf):
    s_len = q_ref.shape[0]
    n_blk = s_len // QBLK
    row_groups = [slice(r, r + D_ROWS_PER_DOT) for r in range(0, GROUP * QBLK, D_ROWS_PER_DOT)]
    kp[...] = _norm_rope(k_ref[...].astype(jnp.float32), gk_ref[...], cos_ref[...], sin_ref[...]).T.astype(kp.dtype)
    v1[:, :HEAD_DIM] = v_ref[...]
    v1[:, HEAD_DIM:] = jnp.ones((s_len, HEAD_DIM), v1.dtype)

    def block_rows(n):
        return pl.ds(pl.multiple_of(n * QBLK, QBLK), QBLK)

    def step(slot, n_prep, n_scores, n_probs, n_out):
        other = 1 - slot
        if n_prep is not None:
            rows = block_rows(n_prep)
            cos = cos_ref[rows, :]
            sin = sin_ref[rows, :]
            for gi in range(GROUP):
                q = _norm_rope(q_ref[rows, gi * HEAD_DIM:(gi + 1) * HEAD_DIM].astype(jnp.float32), gq_ref[...],
                               cos, sin)
                q4[other, gi * QBLK:(gi + 1) * QBLK, :] = (q * (SCALE * LOG2E)).astype(q4.dtype)
        if n_scores is not None:
            for hs in row_groups:
                s = jnp.dot(q4[slot, hs, :], kp[...], preferred_element_type=jnp.float32)
                s_buf[slot, hs, :] = s
                m_buf[slot, hs, :] = jnp.broadcast_to(jnp.max(s, axis=-1, keepdims=True), (D_ROWS_PER_DOT, HEAD_DIM))
        if n_probs is not None:
            for hs in row_groups:
                x = s_buf[other, hs, :] - jnp.tile(m_buf[other, hs, :], (1, s_len // HEAD_DIM))
                p_buf[other, hs, :] = jnp.exp2(x.astype(p_buf.dtype))
        if n_out is not None:
            rows = block_rows(n_out)
            for hs in row_groups:
                acc = jnp.dot(p_buf[slot, hs, :], v1[...], preferred_element_type=jnp.float32)
                o = acc[:, :HEAD_DIM] / acc[:, HEAD_DIM:]
                for gi in range(hs.start // QBLK, hs.stop // QBLK):
                    lanes = slice(gi * HEAD_DIM, (gi + 1) * HEAD_DIM)
                    gate = _silu(g_ref[rows, lanes].astype(jnp.float32))
                    o_ref[rows, lanes] = (o[gi * QBLK - hs.start:(gi + 1) * QBLK - hs.start, :] * gate).astype(o_ref.dtype)

    step(1, 0, None, None, None)
    step(0, 1, 0, None, None)
    step(1, 2, 1, 0, None)

    def pair(i, carry):
        n = 2 * i
        step(0, n + 3, n + 2, n + 1, n)
        step(1, jnp.minimum(n + 4, n_blk - 1), n + 3, n + 2, n + 1)
        return carry

    lax.fori_loop(0, n_blk // 2 - 1, pair, 0)
    step(0, None, None, n_blk - 1, n_blk - 2)
    step(1, None, None, None, n_blk - 1)


def _axial_rope_attention(proj, cos, sin, gq, gk, q_off, k_off, v_off, g_off):
    b, s, _ = proj.shape
    in_specs, out_spec = _gqa_specs(s, q_off, k_off, v_off, g_off)
    table = pl.BlockSpec((s, HEAD_DIM), lambda bi, hi: (0, 0))
    gain = pl.BlockSpec((1, HEAD_DIM), lambda bi, hi: (0, 0))
    stacked = GROUP * QBLK
    return pl.pallas_call(
        _axial_kernel,
        grid=(b, KV_HEADS),
        in_specs=in_specs + [table, table, gain, gain],
        out_specs=out_spec,
        out_shape=jax.ShapeDtypeStruct((b, s, MIX_W), jnp.bfloat16),
        scratch_shapes=[pltpu.VMEM((HEAD_DIM, s), jnp.bfloat16),
                        pltpu.VMEM((s, 2 * HEAD_DIM), jnp.bfloat16),
                        pltpu.VMEM((2, stacked, HEAD_DIM), jnp.bfloat16),
                        pltpu.VMEM((2, stacked, s), jnp.float32),
                        pltpu.VMEM((2, stacked, HEAD_DIM), jnp.float32),
                        pltpu.VMEM((2, stacked, s), jnp.bfloat16)],
        compiler_params=_params("parallel", "parallel"),
        name="axial_rope_attn",
    )(proj, proj, proj, proj, cos, sin, gq.reshape(1, HEAD_DIM), gk.reshape(1, HEAD_DIM))


def _alibi_slopes(n):
    return 2.0 ** (-8.0 * jnp.arange(1, n + 1, dtype=jnp.float32) / n)


def _q_column_scale(widths, q_sections):
    scale = np.ones(sum(widths), np.float32)
    for sec in q_sections:
        start = sum(widths[:sec])
        scale[start:start + widths[sec]] = SCALE * LOG2E
    return jnp.asarray(scale)


def kernel(x, norm_g, final_g, w_in_even, w_out_even, rpb_b, w_in_odd, w_out_odd, sinks_c, qnorm_d, knorm_d):
    b, s, d = x.shape
    bf16 = jnp.bfloat16
    slopes = _alibi_slopes(N_HEADS)
    cos, sin = _rope_tables(s)
    blk = lambda width: width // HEAD_DIM

    h = x.reshape(b * s, d)
    y = _rms_norm_rows(h, norm_g[0], bf16)
    for layer in range(DEPTH):
        i = layer // 2
        final = layer == DEPTH - 1
        g_next = final_g if final else norm_g[layer + 1]
        if layer % 2 == 0:
            widths = [MIX_W] * 8
            offs = [blk(MIX_W) * j for j in range(8)]
            proj = _in_proj(y, w_in_even, i, _q_column_scale(widths, (0, 4))).reshape(b, s, -1)
            mix1, mix2 = _even_mixers(proj, slopes, _na_bias_table(rpb_b[i], s // GRID_W), offs)
            w_out = w_out_even
        else:
            widths = [MIX_W, KV_W, KV_W, MIX_W] * 2
            offs = [blk(int(sum(widths[:j]))) for j in range(8)]
            proj = _in_proj(y, w_in_odd, i, _q_column_scale(widths, (0,))).reshape(b, s, -1)
            mix1 = _windowed_sink_attention(proj, slopes, sinks_c[i], *offs[:4])
            mix2 = _axial_rope_attention(proj, cos, sin, qnorm_d[i], knorm_d[i], *offs[4:])
            w_out = w_out_odd
        res = _out_proj(mix1.reshape(b * s, MIX_W), mix2.reshape(b * s, MIX_W), w_out, i, h, g_next, final)
        if final:
            return res.reshape(b, s, d)
        h, y = res
```

```python
import functools

import jax
import jax.numpy as jnp
import numpy as np
from jax import lax
from jax.experimental import pallas as pl
from jax.experimental.pallas import tpu as pltpu

DEPTH = 4
HEAD_DIM = 128
GRID_W = 64
EPS = 1e-6
N_HEADS = 8
KV_HEADS = 2
GROUP = N_HEADS // KV_HEADS
MIX_W = N_HEADS * HEAD_DIM
KV_W = KV_HEADS * HEAD_DIM
A_PATTERNS = ((128, 1), (512, 4), (2048, 16))
A_SIDE = 64
NA_ROWS = 8
NA_COLS = 16
NA_ROWS_MAX = 8
C_WINDOW = 128
QBLK = 128
ROPE_THETA = 10000.0
SCALE = HEAD_DIM ** -0.5
LOG2E = 1.4426950408889634
MASKED = -1e30
FAR = 1e9
EVEN_AHEAD = 5
WINDOW_AHEAD = 6

VMEM_LIMIT = 56 * 1024 * 1024
IN_PROJ_ROWS = 2048
IN_PROJ_COLS = 1024
OUT_PROJ_ROWS = 512


def _params(*sem):
    return pltpu.CompilerParams(dimension_semantics=sem, vmem_limit_bytes=VMEM_LIMIT)


def _silu(x):
    half = 0.5 * x
    return half + half * jnp.tanh(half)


def _dot_nt(a, b):
    return lax.dot_general(a, b, (((1,), (1,)), ((), ())), preferred_element_type=jnp.float32)


def _run_pipelined(items, ahead):
    pending = [scores() for scores, _ in items[:ahead]]
    for idx, (_, finish) in enumerate(items):
        if idx + ahead < len(items):
            pending.append(items[idx + ahead][0]())
        finish(pending.pop(0))


def _interleave(a, b):
    keyed = [((i + 0.5) / len(a), 0, x) for i, x in enumerate(a)] + [((i + 0.5) / len(b), 1, x) for i, x in enumerate(b)]
    return [x for _, _, x in sorted(keyed, key=lambda t: t[:2])]


def _norm_kernel(h_ref, g_ref, y_ref):
    x = h_ref[...]
    y = x * lax.rsqrt(jnp.mean(x * x, axis=-1, keepdims=True) + EPS)
    y_ref[...] = (y * g_ref[...]).astype(y_ref.dtype)


def _rms_norm_rows(h, g, out_dtype, tm=1024):
    m, d = h.shape
    return pl.pallas_call(
        _norm_kernel,
        grid=(m // tm,),
        in_specs=[pl.BlockSpec((tm, d), lambda i: (i, 0)), pl.BlockSpec((1, d), lambda i: (0, 0))],
        out_specs=pl.BlockSpec((tm, d), lambda i: (i, 0)),
        out_shape=jax.ShapeDtypeStruct((m, d), out_dtype),
        compiler_params=_params("parallel"),
        name="rms_norm",
    )(h, g.reshape(1, d))


def _in_proj_kernel(x_ref, w_ref, scale_ref, o_ref, wb_ref):
    @pl.when(pl.program_id(1) == 0)
    def _():
        wb_ref[...] = (w_ref[...] * scale_ref[...]).astype(wb_ref.dtype)

    o_ref[...] = jnp.dot(x_ref[...], wb_ref[...], preferred_element_type=jnp.float32).astype(o_ref.dtype)


def _in_proj(y, w_stack, layer, col_scale, tm=IN_PROJ_ROWS, tn=IN_PROJ_COLS):
    m, k = y.shape
    n = w_stack.shape[2]
    return pl.pallas_call(
        _in_proj_kernel,
        grid=(n // tn, m // tm),
        in_specs=[pl.BlockSpec((tm, k), lambda j, i: (i, 0)),
                  pl.BlockSpec((None, k, tn), lambda j, i: (layer, 0, j)),
                  pl.BlockSpec((1, tn), lambda j, i: (0, j))],
        out_specs=pl.BlockSpec((tm, tn), lambda j, i: (i, j)),
        out_shape=jax.ShapeDtypeStruct((m, n), jnp.bfloat16),
        scratch_shapes=[pltpu.VMEM((k, tn), jnp.bfloat16)],
        compiler_params=_params("arbitrary", "arbitrary"),
        name="in_proj",
    )(y, w_stack, col_scale.reshape(1, n))


def _out_proj_kernel(m1_ref, m2_ref, w_ref, h_ref, g_ref, *refs, final):
    out_refs, wb_ref = refs[:-1], refs[-1]
    half = m1_ref.shape[1]

    @pl.when(pl.program_id(0) == 0)
    def _():
        wb_ref[...] = w_ref[...].astype(wb_ref.dtype)

    acc = jnp.dot(m1_ref[...], wb_ref[:half, :], preferred_element_type=jnp.float32)
    acc = acc + jnp.dot(m2_ref[...], wb_ref[half:, :], preferred_element_type=jnp.float32)
    h = h_ref[...] + acc
    y = h * lax.rsqrt(jnp.mean(h * h, axis=-1, keepdims=True) + EPS) * g_ref[...]
    if final:
        out_refs[0][...] = y
    else:
        out_refs[0][...] = h
        out_refs[1][...] = y.astype(out_refs[1].dtype)


def _out_proj(mix1, mix2, w_stack, layer, h, g_next, final, tm=OUT_PROJ_ROWS):
    m, d = h.shape
    half = mix1.shape[1]
    row = lambda i: (i, 0)
    if final:
        out_shape = jax.ShapeDtypeStruct((m, d), jnp.float32)
        out_specs = pl.BlockSpec((tm, d), row)
    else:
        out_shape = (jax.ShapeDtypeStruct((m, d), jnp.float32), jax.ShapeDtypeStruct((m, d), jnp.bfloat16))
        out_specs = (pl.BlockSpec((tm, d), row), pl.BlockSpec((tm, d), row))
    return pl.pallas_call(
        functools.partial(_out_proj_kernel, final=final),
        grid=(m // tm,),
        in_specs=[pl.BlockSpec((tm, half), row), pl.BlockSpec((tm, half), row),
                  pl.BlockSpec((None, 2 * half, d), lambda i: (layer, 0, 0), pipeline_mode=pl.Buffered(1)),
                  pl.BlockSpec((tm, d), row), pl.BlockSpec((1, d), lambda i: (0, 0))],
        out_specs=out_specs,
        out_shape=out_shape,
        scratch_shapes=[pltpu.VMEM((2 * half, d), jnp.bfloat16)],
        compiler_params=_params("arbitrary"),
        name="out_proj_final" if final else "out_proj",
    )(mix1, mix2, w_stack, h, g_next.reshape(1, d))


A_REGROUP = 4


def _dilated_items(slope, q_ref, k_ref, v_ref, xf, x4, qp, kp, vp, o_s, lse_s):
    assert tuple(d for _, d in A_PATTERNS) == (1, A_REGROUP, A_REGROUP ** 2)
    s_len = q_ref.shape[0]
    n4 = s_len // A_REGROUP
    n16 = n4 // A_REGROUP

    for a, (src, dst) in enumerate(((q_ref, qp), (k_ref, kp), (v_ref, vp))):
        xf[a] = src[...].astype(jnp.float32)
        for c4 in range(A_REGROUP):
            x = xf[a, pl.ds(c4, n4, stride=A_REGROUP), :]
            x4[a, c4 * n4:(c4 + 1) * n4, :] = x
            dst[0, c4 * n4:(c4 + 1) * n4, :] = x.astype(dst.dtype)
        for c4 in range(A_REGROUP):
            for c2 in range(A_REGROUP):
                c16 = A_REGROUP * c2 + c4
                x = x4[a, pl.ds(c4 * n4 + c2, n16, stride=A_REGROUP), :]
                dst[1, c16 * n16:(c16 + 1) * n16, :] = x.astype(dst.dtype)
    srcs = [(q_ref, k_ref, v_ref), (qp.at[0], kp.at[0], vp.at[0]), (qp.at[1], kp.at[1], vp.at[1])]

    tiles = []
    bias = {}
    for p_idx, (_, dil) in enumerate(A_PATTERNS):
        n = s_len // dil
        kw = min(2 * QBLK, n)
        rel0 = (lax.broadcasted_iota(jnp.int32, (QBLK, kw), 1)
                - lax.broadcasted_iota(jnp.int32, (QBLK, kw), 0))
        for t in range(s_len // QBLK):
            c, i0 = divmod(t * QBLK, n)
            w0 = min(max(i0 - A_SIDE, 0), n - kw)
            if (p_idx, w0 - i0) not in bias:
                dist = jnp.abs(rel0 + (w0 - i0)).astype(jnp.float32)
                bias[(p_idx, w0 - i0)] = (slope * (-dil * LOG2E)) * jnp.where(dist <= float(A_SIDE), dist, FAR)
            if p_idx < 2:
                out_rows = pl.ds(c * n + i0, QBLK)
            else:
                c2, c4 = divmod(c, A_REGROUP)
                out_rows = pl.ds(c4 * n4 + c2 + A_REGROUP * i0, QBLK, stride=A_REGROUP)
            tiles.append((p_idx, slice(c * n + i0, c * n + i0 + QBLK), slice(c * n + w0, c * n + w0 + kw),
                          w0 - i0, out_rows))

    def item(p_idx, q_rows, k_rows, off, out_rows):
        q_src, k_src, v_src = srcs[p_idx]

        def scores():
            return _dot_nt(q_src[q_rows, :], k_src[k_rows, :])

        def finish(s):
            v_win = v_src[k_rows, :]
            logits = bias[(p_idx, off)] + s
            m = jnp.max(logits, axis=-1, keepdims=True)
            p = jnp.exp2(logits - m)
            l = jnp.sum(p, axis=-1, keepdims=True)
            o_s[p_idx, out_rows, :] = jnp.dot(p.astype(v_win.dtype), v_win, preferred_element_type=jnp.float32) / l
            lse_s[p_idx, out_rows, :] = jnp.broadcast_to(m + jnp.log2(l), (QBLK, HEAD_DIM))

        return scores, finish

    return [item(*t) for t in tiles]


def _dilated_merge(g_ref, o_ref, o_s, lse_s, o_nat):
    n4 = o_ref.shape[0] // A_REGROUP
    for c4 in range(A_REGROUP):
        blk = slice(c4 * n4, (c4 + 1) * n4)
        nat = pl.ds(c4, n4, stride=A_REGROUP)
        lses = (lse_s[0, nat, :], lse_s[1, blk, :], lse_s[2, blk, :])
        outs = (o_s[0, nat, :], o_s[1, blk, :], o_s[2, blk, :])
        top = jnp.maximum(jnp.maximum(lses[0], lses[1]), lses[2])
        ws = [jnp.exp2(ls - top) for ls in lses]
        num = ws[0] * outs[0] + ws[1] * outs[1] + ws[2] * outs[2]
        o_nat[nat, :] = num / (ws[0] + ws[1] + ws[2])
    o_ref[...] = (o_nat[...] * _silu(g_ref[...].astype(jnp.float32))).astype(o_ref.dtype)


NA_VARIANTS = 8


def _na_variant_row(v, rows):
    return v if v <= 4 else rows - NA_VARIANTS + v


def _na_row_start(r, rows):
    kh = min(NA_ROWS, rows)
    return min(max(r - kh // 2, 0), rows - kh)


def _na_bias_kernel(rpb_ref, o_ref, pair_ref, *, rows):
    kh = min(NA_ROWS, rows)
    n_drow = 2 * NA_ROWS_MAX - 1
    n_dcol = 2 * NA_COLS - 1
    h = pl.program_id(0)
    shape = (GRID_W, 2 * GRID_W)
    lane = lax.broadcasted_iota(jnp.int32, shape, 1)
    c = lax.broadcasted_iota(jnp.int32, shape, 0)
    kc = lane % GRID_W
    second = lane >= GRID_W
    cs = jnp.clip(c - NA_COLS // 2, 0, GRID_W - NA_COLS)
    valid = (kc >= cs) & (kc < cs + NA_COLS)
    dcol = kc - c + (NA_COLS - 1)
    for d in range(n_drow - 1):
        acc = jnp.zeros(shape, jnp.float32)
        for j in range(n_dcol):
            lo = rpb_ref[(h * n_drow + d) * n_dcol + j]
            hi = rpb_ref[(h * n_drow + d + 1) * n_dcol + j]
            acc = jnp.where(dcol == j, jnp.where(second, hi, lo), acc)
        pair_ref[d] = jnp.where(valid, acc * LOG2E, MASKED)
    for v in range(NA_VARIANTS):
        r = _na_variant_row(v, rows)
        d0 = _na_row_start(r, rows) - r + (NA_ROWS_MAX - 1)
        for i in range(0, kh, 2):
            o_ref[v, :, i * GRID_W:(i + 2) * GRID_W] = pair_ref[d0 + i]


def _na_bias_table(rpb, rows):
    n_heads = rpb.shape[0]
    kh = min(NA_ROWS, rows)
    return pl.pallas_call(
        functools.partial(_na_bias_kernel, rows=rows),
        grid=(n_heads,),
        in_specs=[pl.BlockSpec(memory_space=pltpu.SMEM)],
        out_specs=pl.BlockSpec((None, NA_VARIANTS, GRID_W, kh * GRID_W), lambda hi: (hi, 0, 0, 0)),
        out_shape=jax.ShapeDtypeStruct((n_heads, NA_VARIANTS, GRID_W, kh * GRID_W), jnp.float32),
        scratch_shapes=[pltpu.VMEM((2 * NA_ROWS_MAX - 2, GRID_W, 2 * GRID_W), jnp.float32)],
        compiler_params=_params("parallel"),
        name="na_bias",
    )(rpb.reshape(-1))


def _na_items(q_ref, k_ref, v_ref, g_ref, bias_ref, o_ref):
    s_len = q_ref.shape[0]
    rows = s_len // GRID_W
    win = min(NA_ROWS, rows) * GRID_W

    def item(r):
        k_rows = slice(_na_row_start(r, rows) * GRID_W, _na_row_start(r, rows) * GRID_W + win)
        variant = r if r < 4 else (r - (rows - NA_VARIANTS) if r > rows - 4 else 4)
        sl = slice(r * GRID_W, (r + 1) * GRID_W)

        def scores():
            return _dot_nt(q_ref[sl, :], k_ref[k_rows, :])

        def finish(s):
            v_win = v_ref[k_rows, :]
            logits = bias_ref[variant] + s
            m = jnp.max(logits, axis=-1, keepdims=True)
            p = jnp.exp2(logits - m)
            den = jnp.sum(p, axis=-1, keepdims=True)
            o = jnp.dot(p.astype(v_win.dtype), v_win, preferred_element_type=jnp.float32) / den
            o_ref[sl, :] = (o * _silu(g_ref[sl, :].astype(jnp.float32))).astype(o_ref.dtype)

        return scores, finish

    return [item(r) for r in range(rows)]


def _even_mixers_kernel(slopes_ref, qa_ref, ka_ref, va_ref, ga_ref, qb_ref, kb_ref, vb_ref, gb_ref, bias_ref,
                        oa_ref, ob_ref, xf, x4, qp, kp, vp, o_s, lse_s, o_nat):
    a_items = _dilated_items(slopes_ref[pl.program_id(0)], qa_ref, ka_ref, va_ref, xf, x4, qp, kp, vp, o_s, lse_s)
    b_items = _na_items(qb_ref, kb_ref, vb_ref, gb_ref, bias_ref, ob_ref)
    _run_pipelined(_interleave(a_items, b_items), EVEN_AHEAD)
    _dilated_merge(ga_ref, oa_ref, o_s, lse_s, o_nat)


def _even_mixers(proj, slopes, bias, offs):
    b, s, _ = proj.shape
    col = lambda off: pl.BlockSpec((None, s, HEAD_DIM), lambda hi, bi: (bi, 0, off + hi))
    out_spec = pl.BlockSpec((None, s, HEAD_DIM), lambda hi, bi: (bi, 0, hi))
    out_shape = jax.ShapeDtypeStruct((b, s, MIX_W), jnp.bfloat16)
    n_pat = len(A_PATTERNS)
    f32_rows = lambda n: pltpu.VMEM((n, s, HEAD_DIM), jnp.float32)
    return pl.pallas_call(
        _even_mixers_kernel,
        grid=(N_HEADS, b),
        in_specs=[pl.BlockSpec(memory_space=pltpu.SMEM)] + [col(off) for off in offs]
        + [pl.BlockSpec((None,) + bias.shape[1:], lambda hi, bi: (hi, 0, 0, 0))],
        out_specs=(out_spec, out_spec),
        out_shape=(out_shape, out_shape),
        scratch_shapes=[f32_rows(3), f32_rows(3)]
        + [pltpu.VMEM((n_pat - 1, s, HEAD_DIM), jnp.bfloat16)] * 3
        + [f32_rows(n_pat), f32_rows(n_pat), pltpu.VMEM((s, HEAD_DIM), jnp.float32)],
        compiler_params=_params("parallel", "parallel"),
        name="even_mixers",
    )(slopes, *([proj] * 8), bias)


def _window_kernel(slopes_ref, sinks_ref, q_ref, k_ref, v_ref, g_ref, o_ref, v1, kt):
    s_len = q_ref.shape[0]
    v1[:, :HEAD_DIM] = v_ref[...]
    v1[:, HEAD_DIM:] = jnp.ones((s_len, HEAD_DIM), v1.dtype)
    kt[...] = k_ref[...].astype(jnp.float32).T.astype(kt.dtype)
    hk = pl.program_id(1)
    kw = 3 * QBLK
    rel0 = (lax.broadcasted_iota(jnp.int32, (QBLK, kw), 1) - lax.broadcasted_iota(jnp.int32, (QBLK, kw), 0))
    slopes = [slopes_ref[hk * GROUP + gi] for gi in range(GROUP)]
    sinks = [sinks_ref[hk * GROUP + gi] for gi in range(GROUP)]
    sinks = [sink * LOG2E for sink in sinks]
    bias = {}
    tiles = []
    for n in range(s_len // QBLK):
        q0 = n * QBLK
        w0 = min(max(q0 - QBLK, 0), s_len - kw)
        if (w0 - q0, 0) not in bias:
            dist = jnp.abs(rel0 + (w0 - q0)).astype(jnp.float32)
            dist = jnp.where(dist <= float(C_WINDOW), dist, FAR)
            for gi in range(GROUP):
                bias[(w0 - q0, gi)] = (slopes[gi] * -LOG2E) * dist
        tiles += [(slice(q0, q0 + QBLK), gi, slice(w0, w0 + kw), w0 - q0) for gi in range(GROUP)]

    def item(q_rows, gi, k_rows, off):
        lanes = slice(gi * HEAD_DIM, (gi + 1) * HEAD_DIM)

        def scores():
            return jnp.dot(q_ref[q_rows, lanes], kt[:, k_rows], preferred_element_type=jnp.float32)

        def finish(s):
            v_win = v1[k_rows, :]
            logits = bias[(off, gi)] + s
            m = jnp.maximum(jnp.max(logits, axis=-1, keepdims=True), sinks[gi])
            p = jnp.exp2((logits - m).astype(v_win.dtype))
            acc = jnp.dot(p, v_win, preferred_element_type=jnp.float32)
            o = acc[:, :HEAD_DIM] / (acc[:, HEAD_DIM:] + jnp.exp2(sinks[gi] - m))
            gate = _silu(g_ref[q_rows, lanes].astype(jnp.float32))
            o_ref[q_rows, lanes] = (o * gate).astype(o_ref.dtype)

        return scores, finish

    _run_pipelined([item(*t) for t in tiles], WINDOW_AHEAD)


def _gqa_specs(s, q_off, k_off, v_off, g_off):
    gw = GROUP * HEAD_DIM
    wide = lambda off: pl.BlockSpec((None, s, gw), lambda bi, hi: (bi, 0, off // GROUP + hi))
    col = lambda off: pl.BlockSpec((None, s, HEAD_DIM), lambda bi, hi: (bi, 0, off + hi))
    in_specs = [wide(q_off), col(k_off), col(v_off), wide(g_off)]
    out_spec = pl.BlockSpec((None, s, gw), lambda bi, hi: (bi, 0, hi))
    return in_specs, out_spec


def _windowed_sink_attention(proj, slopes, sinks, q_off, k_off, v_off, g_off):
    b, s, _ = proj.shape
    in_specs, out_spec = _gqa_specs(s, q_off, k_off, v_off, g_off)
    smem = pl.BlockSpec(memory_space=pltpu.SMEM)
    return pl.pallas_call(
        _window_kernel,
        grid=(b, KV_HEADS),
        in_specs=[smem, smem] + in_specs,
        out_specs=out_spec,
        out_shape=jax.ShapeDtypeStruct((b, s, MIX_W), jnp.bfloat16),
        scratch_shapes=[pltpu.VMEM((s, 2 * HEAD_DIM), jnp.bfloat16),
                        pltpu.VMEM((HEAD_DIM, s), jnp.bfloat16)],
        compiler_params=_params("parallel", "parallel"),
        name="window_sink_attn",
    )(slopes, sinks, proj, proj, proj, proj)


def _rope_tables(s):
    t = np.arange(s)
    quarter = HEAD_DIM // 4
    inv = jnp.asarray(ROPE_THETA, jnp.float32) ** (-jnp.arange(quarter, dtype=jnp.float32) / quarter)
    ang_r = jnp.asarray(t // GRID_W, jnp.float32)[:, None] * inv[None, :]
    ang_c = jnp.asarray(t % GRID_W, jnp.float32)[:, None] * inv[None, :]
    cos = jnp.concatenate([jnp.cos(ang_r), jnp.cos(ang_r), jnp.cos(ang_c), jnp.cos(ang_c)], axis=-1)
    sin = jnp.concatenate([-jnp.sin(ang_r), jnp.sin(ang_r), -jnp.sin(ang_c), jnp.sin(ang_c)], axis=-1)
    return cos, sin


def _norm_rope(x, gain, cos, sin):
    y = x * lax.rsqrt(jnp.mean(x * x, axis=-1, keepdims=True) + EPS) * gain
    quarter = HEAD_DIM // 4
    lane = lax.broadcasted_iota(jnp.int32, y.shape, 1)
    partner = jnp.where((lane // quarter) % 2 == 0,
                        pltpu.roll(y, HEAD_DIM - quarter, 1),
                        pltpu.roll(y, quarter, 1))
    return y * cos + partner * sin


D_ROWS_PER_DOT = 256


def _axial_kernel(q_ref, k_ref, v_ref, g_ref, cos_ref, sin_ref, gq_ref, gk_ref, o_ref,
                  kp, v1, q4, s_buf, m_buf, p_buf):
    s_len = q_ref.shape[0]
    n_blk = s_len // QBLK
    row_groups = [slice(r, r + D_ROWS_PER_DOT) for r in range(0, GROUP * QBLK, D_ROWS_PER_DOT)]
    kp[...] = _norm_rope(k_ref[...].astype(jnp.float32), gk_ref[...], cos_ref[...], sin_ref[...]).T.astype(kp.dtype)
    v1[:, :HEAD_DIM] = v_ref[...]
    v1[:, HEAD_DIM:] = jnp.ones((s_len, HEAD_DIM), v1.dtype)

    def block_rows(n):
        return pl.ds(pl.multiple_of(n * QBLK, QBLK), QBLK)

    def step(slot, n_prep, n_scores, n_probs, n_out):
        other = 1 - slot
        if n_prep is not None:
            rows = block_rows(n_prep)
            cos = cos_ref[rows, :]
            sin = sin_ref[rows, :]
            for gi in range(GROUP):
                q = _norm_rope(q_ref[rows, gi * HEAD_DIM:(gi + 1) * HEAD_DIM].astype(jnp.float32), gq_ref[...],
                               cos, sin)
                q4[other, gi * QBLK:(gi + 1) * QBLK, :] = (q * (SCALE * LOG2E)).astype(q4.dtype)
        if n_scores is not None:
            for hs in row_groups:
                s = jnp.dot(q4[slot, hs, :], kp[...], preferred_element_type=jnp.float32)
                s_buf[slot, hs, :] = s
                m_buf[slot, hs, :] = jnp.broadcast_to(jnp.max(s, axis=-1, keepdims=True), (D_ROWS_PER_DOT, HEAD_DIM))
        if n_probs is not None:
            for hs in row_groups:
                x = s_buf[other, hs, :] - jnp.tile(m_buf[other, hs, :], (1, s_len // HEAD_DIM))
                p_buf[other, hs, :] = jnp.exp2(x.astype(p_buf.dtype))
        if n_out is not None:
            rows = block_rows(n_out)
            for hs in row_groups:
                acc = jnp.dot(p_buf[slot, hs, :], v1[...], preferred_element_type=jnp.float32)
                o = acc[:, :HEAD_DIM] / acc[:, HEAD_DIM:]
                for gi in range(hs.start // QBLK, hs.stop // QBLK):
                    lanes = slice(gi * HEAD_DIM, (gi + 1) * HEAD_DIM)
                    gate = _silu(g_ref[rows, lanes].astype(jnp.float32))
                    o_ref[rows, lanes] = (o[gi * QBLK - hs.start:(gi + 1) * QBLK - hs.start, :] * gate).astype(o_ref.dtype)

    step(1, 0, None, None, None)
    step(0, 1, 0, None, None)
    step(1, 2, 1, 0, None)

    def pair(i, carry):
        n = 2 * i
        step(0, n + 3, n + 2, n + 1, n)
        step(1, jnp.minimum(n + 4, n_blk - 1), n + 3, n + 2, n + 1)
        return carry

    lax.fori_loop(0, n_blk // 2 - 1, pair, 0)
    step(0, None, None, n_blk - 1, n_blk - 2)
    step(1, None, None, None, n_blk - 1)


def _axial_rope_attention(proj, cos, sin, gq, gk, q_off, k_off, v_off, g_off):
    b, s, _ = proj.shape
    in_specs, out_spec = _gqa_specs(s, q_off, k_off, v_off, g_off)
    table = pl.BlockSpec((s, HEAD_DIM), lambda bi, hi: (0, 0))
    gain = pl.BlockSpec((1, HEAD_DIM), lambda bi, hi: (0, 0))
    stacked = GROUP * QBLK
    return pl.pallas_call(
        _axial_kernel,
        grid=(b, KV_HEADS),
        in_specs=in_specs + [table, table, gain, gain],
        out_specs=out_spec,
        out_shape=jax.ShapeDtypeStruct((b, s, MIX_W), jnp.bfloat16),
        scratch_shapes=[pltpu.VMEM((HEAD_DIM, s), jnp.bfloat16),
                        pltpu.VMEM((s, 2 * HEAD_DIM), jnp.bfloat16),
                        pltpu.VMEM((2, stacked, HEAD_DIM), jnp.bfloat16),
                        pltpu.VMEM((2, stacked, s), jnp.float32),
                        pltpu.VMEM((2, stacked, HEAD_DIM), jnp.float32),
                        pltpu.VMEM((2, stacked, s), jnp.bfloat16)],
        compiler_params=_params("parallel", "parallel"),
        name="axial_rope_attn",
    )(proj, proj, proj, proj, cos, sin, gq.reshape(1, HEAD_DIM), gk.reshape(1, HEAD_DIM))


def _alibi_slopes(n):
    return 2.0 ** (-8.0 * jnp.arange(1, n + 1, dtype=jnp.float32) / n)


def _q_column_scale(widths, q_sections):
    scale = np.ones(sum(widths), np.float32)
    for sec in q_sections:
        start = sum(widths[:sec])
        scale[start:start + widths[sec]] = SCALE * LOG2E
    return jnp.asarray(scale)


def kernel(x, norm_g, final_g, w_in_even, w_out_even, rpb_b, w_in_odd, w_out_odd, sinks_c, qnorm_d, knorm_d):
    b, s, d = x.shape
    bf16 = jnp.bfloat16
    slopes = _alibi_slopes(N_HEADS)
    cos, sin = _rope_tables(s)
    blk = lambda width: width // HEAD_DIM

    h = x.reshape(b * s, d)
    y = _rms_norm_rows(h, norm_g[0], bf16)
    for layer in range(DEPTH):
        i = layer // 2
        final = layer == DEPTH - 1
        g_next = final_g if final else norm_g[layer + 1]
        if layer % 2 == 0:
            widths = [MIX_W] * 8
            offs = [blk(MIX_W) * j for j in range(8)]
            proj = _in_proj(y, w_in_even, i, _q_column_scale(widths, (0, 4))).reshape(b, s, -1)
            mix1, mix2 = _even_mixers(proj, slopes, _na_bias_table(rpb_b[i], s // GRID_W), offs)
            w_out = w_out_even
        else:
            widths = [MIX_W, KV_W, KV_W, MIX_W] * 2
            offs = [blk(int(sum(widths[:j]))) for j in range(8)]
            proj = _in_proj(y, w_in_odd, i, _q_column_scale(widths, (0,))).reshape(b, s, -1)
            mix1 = _windowed_sink_attention(proj, slopes, sinks_c[i], *offs[:4])
            mix2 = _axial_rope_attention(proj, cos, sin, qnorm_d[i], knorm_d[i], *offs[4:])
            w_out = w_out_odd
        res = _out_proj(mix1.reshape(b * s, MIX_W), mix2.reshape(b * s, MIX_W), w_out, i, h, g_next, final)
        if final:
            return res.reshape(b, s, d)
        h, y = res
```

```python
import functools

import jax
import jax.numpy as jnp
import numpy as np
from jax import lax
from jax.experimental import pallas as pl
from jax.experimental.pallas import tpu as pltpu

DEPTH = 4
HEAD_DIM = 128
GRID_W = 64
EPS = 1e-6
N_HEADS = 8
KV_HEADS = 2
GROUP = N_HEADS // KV_HEADS
MIX_W = N_HEADS * HEAD_DIM
KV_W = KV_HEADS * HEAD_DIM
A_PATTERNS = ((128, 1), (512, 4), (2048, 16))
A_SIDE = 64
NA_ROWS = 8
NA_COLS = 16
NA_ROWS_MAX = 8
C_WINDOW = 128
QBLK = 128
ROPE_THETA = 10000.0
SCALE = HEAD_DIM ** -0.5
LOG2E = 1.4426950408889634
MASKED = -1e30
FAR = 1e9
EVEN_AHEAD = 5
WINDOW_AHEAD = 6

VMEM_LIMIT = 56 * 1024 * 1024
IN_PROJ_ROWS = 2048
IN_PROJ_COLS = 1024
OUT_PROJ_ROWS = 512


def _params(*sem):
    return pltpu.CompilerParams(dimension_semantics=sem, vmem_limit_bytes=VMEM_LIMIT)


def _silu(x):
    half = 0.5 * x
    return half + half * jnp.tanh(half)


def _dot_nt(a, b):
    return lax.dot_general(a, b, (((1,), (1,)), ((), ())), preferred_element_type=jnp.float32)


def _run_pipelined(items, ahead):
    pending = [scores() for scores, _ in items[:ahead]]
    for idx, (_, finish) in enumerate(items):
        if idx + ahead < len(items):
            pending.append(items[idx + ahead][0]())
        finish(pending.pop(0))


def _interleave(a, b):
    keyed = [((i + 0.5) / len(a), 0, x) for i, x in enumerate(a)] + [((i + 0.5) / len(b), 1, x) for i, x in enumerate(b)]
    return [x for _, _, x in sorted(keyed, key=lambda t: t[:2])]


def _norm_kernel(h_ref, g_ref, y_ref):
    x = h_ref[...]
    y = x * lax.rsqrt(jnp.mean(x * x, axis=-1, keepdims=True) + EPS)
    y_ref[...] = (y * g_ref[...]).astype(y_ref.dtype)


def _rms_norm_rows(h, g, out_dtype, tm=1024):
    m, d = h.shape
    return pl.pallas_call(
        _norm_kernel,
        grid=(m // tm,),
        in_specs=[pl.BlockSpec((tm, d), lambda i: (i, 0)), pl.BlockSpec((1, d), lambda i: (0, 0))],
        out_specs=pl.BlockSpec((tm, d), lambda i: (i, 0)),
        out_shape=jax.ShapeDtypeStruct((m, d), out_dtype),
        compiler_params=_params("parallel"),
        name="rms_norm",
    )(h, g.reshape(1, d))


def _in_proj_kernel(x_ref, w_ref, scale_ref, o_ref, wb_ref):
    @pl.when(pl.program_id(1) == 0)
    def _():
        wb_ref[...] = (w_ref[...] * scale_ref[...]).astype(wb_ref.dtype)

    o_ref[...] = jnp.dot(x_ref[...], wb_ref[...], preferred_element_type=jnp.float32).astype(o_ref.dtype)


def _in_proj(y, w_stack, layer, col_scale, tm=IN_PROJ_ROWS, tn=IN_PROJ_COLS):
    m, k = y.shape
    n = w_stack.shape[2]
    return pl.pallas_call(
        _in_proj_kernel,
        grid=(n // tn, m // tm),
        in_specs=[pl.BlockSpec((tm, k), lambda j, i: (i, 0)),
                  pl.BlockSpec((None, k, tn), lambda j, i: (layer, 0, j)),
                  pl.BlockSpec((1, tn), lambda j, i: (0, j))],
        out_specs=pl.BlockSpec((tm, tn), lambda j, i: (i, j)),
        out_shape=jax.ShapeDtypeStruct((m, n), jnp.bfloat16),
        scratch_shapes=[pltpu.VMEM((k, tn), jnp.bfloat16)],
        compiler_params=_params("arbitrary", "arbitrary"),
        name="in_proj",
    )(y, w_stack, col_scale.reshape(1, n))


H_RING = 3


def _out_proj_kernel(m1_ref, m2_ref, w_ref, h_hbm, g_ref, *refs, final):
    out_refs, (wb_ref, h_ring, h_sems) = refs[:-3], refs[-3:]
    half = m1_ref.shape[1]
    tm = m1_ref.shape[0]
    step = pl.program_id(0)
    n_steps = pl.num_programs(0)

    def h_copy(s):
        slot = lax.rem(s, H_RING)
        rows = pl.ds(pl.multiple_of(s * tm, tm), tm)
        return pltpu.make_async_copy(h_hbm.at[rows, :], h_ring.at[slot], h_sems.at[slot])

    @pl.when(step == 0)
    def _():
        for s in range(H_RING - 1):
            h_copy(s).start()
        wb_ref[...] = w_ref[...].astype(wb_ref.dtype)

    @pl.when(step + (H_RING - 1) < n_steps)
    def _():
        h_copy(step + (H_RING - 1)).start()

    acc = jnp.dot(m1_ref[...], wb_ref[:half, :], preferred_element_type=jnp.float32)
    acc = acc + jnp.dot(m2_ref[...], wb_ref[half:, :], preferred_element_type=jnp.float32)
    h_copy(step).wait()
    h = h_ring[lax.rem(step, H_RING)] + acc
    y = h * lax.rsqrt(jnp.mean(h * h, axis=-1, keepdims=True) + EPS) * g_ref[...]
    if final:
        out_refs[0][...] = y
    else:
        out_refs[0][...] = h
        out_refs[1][...] = y.astype(out_refs[1].dtype)


def _out_proj(mix1, mix2, w_stack, layer, h, g_next, final, tm=OUT_PROJ_ROWS):
    m, d = h.shape
    half = mix1.shape[1]
    assert m // tm >= H_RING - 1
    row = lambda i: (i, 0)
    if final:
        out_shape = jax.ShapeDtypeStruct((m, d), jnp.float32)
        out_specs = pl.BlockSpec((tm, d), row)
    else:
        out_shape = (jax.ShapeDtypeStruct((m, d), jnp.float32), jax.ShapeDtypeStruct((m, d), jnp.bfloat16))
        out_specs = (pl.BlockSpec((tm, d), row), pl.BlockSpec((tm, d), row))
    return pl.pallas_call(
        functools.partial(_out_proj_kernel, final=final),
        grid=(m // tm,),
        in_specs=[pl.BlockSpec((tm, half), row), pl.BlockSpec((tm, half), row),
                  pl.BlockSpec((None, 2 * half, d), lambda i: (layer, 0, 0), pipeline_mode=pl.Buffered(1)),
                  pl.BlockSpec(memory_space=pl.ANY), pl.BlockSpec((1, d), lambda i: (0, 0))],
        out_specs=out_specs,
        out_shape=out_shape,
        scratch_shapes=[pltpu.VMEM((2 * half, d), jnp.bfloat16),
                        pltpu.VMEM((H_RING, tm, d), jnp.float32), pltpu.SemaphoreType.DMA((H_RING,))],
        compiler_params=_params("arbitrary"),
        name="out_proj_final" if final else "out_proj",
    )(mix1, mix2, w_stack, h, g_next.reshape(1, d))


A_REGROUP = 4


def _dilated_items(slope, q_ref, k_ref, v_ref, xf, x4, qp, kp, vp, o_s, lse_s):
    assert tuple(d for _, d in A_PATTERNS) == (1, A_REGROUP, A_REGROUP ** 2)
    s_len = q_ref.shape[0]
    n4 = s_len // A_REGROUP
    n16 = n4 // A_REGROUP

    for a, (src, dst) in enumerate(((q_ref, qp), (k_ref, kp), (v_ref, vp))):
        xf[a] = src[...].astype(jnp.float32)
        for c4 in range(A_REGROUP):
            x = xf[a, pl.ds(c4, n4, stride=A_REGROUP), :]
            x4[a, c4 * n4:(c4 + 1) * n4, :] = x
            dst[0, c4 * n4:(c4 + 1) * n4, :] = x.astype(dst.dtype)
        for c4 in range(A_REGROUP):
            for c2 in range(A_REGROUP):
                c16 = A_REGROUP * c2 + c4
                x = x4[a, pl.ds(c4 * n4 + c2, n16, stride=A_REGROUP), :]
                dst[1, c16 * n16:(c16 + 1) * n16, :] = x.astype(dst.dtype)
    srcs = [(q_ref, k_ref, v_ref), (qp.at[0], kp.at[0], vp.at[0]), (qp.at[1], kp.at[1], vp.at[1])]

    tiles = []
    bias = {}
    for p_idx, (_, dil) in enumerate(A_PATTERNS):
        n = s_len // dil
        kw = min(2 * QBLK, n)
        rel0 = (lax.broadcasted_iota(jnp.int32, (QBLK, kw), 1)
                - lax.broadcasted_iota(jnp.int32, (QBLK, kw), 0))
        for t in range(s_len // QBLK):
            c, i0 = divmod(t * QBLK, n)
            w0 = min(max(i0 - A_SIDE, 0), n - kw)
            if (p_idx, w0 - i0) not in bias:
                dist = jnp.abs(rel0 + (w0 - i0)).astype(jnp.float32)
                bias[(p_idx, w0 - i0)] = (slope * (-dil * LOG2E)) * jnp.where(dist <= float(A_SIDE), dist, FAR)
            if p_idx < 2:
                out_rows = pl.ds(c * n + i0, QBLK)
            else:
                c2, c4 = divmod(c, A_REGROUP)
                out_rows = pl.ds(c4 * n4 + c2 + A_REGROUP * i0, QBLK, stride=A_REGROUP)
            tiles.append((p_idx, slice(c * n + i0, c * n + i0 + QBLK), slice(c * n + w0, c * n + w0 + kw),
                          w0 - i0, out_rows))

    def item(p_idx, q_rows, k_rows, off, out_rows):
        q_src, k_src, v_src = srcs[p_idx]

        def scores():
            return _dot_nt(q_src[q_rows, :], k_src[k_rows, :])

        def finish(s):
            v_win = v_src[k_rows, :]
            logits = bias[(p_idx, off)] + s
            m = jnp.max(logits, axis=-1, keepdims=True)
            p = jnp.exp2(logits - m)
            l = jnp.sum(p, axis=-1, keepdims=True)
            o_s[p_idx, out_rows, :] = jnp.dot(p.astype(v_win.dtype), v_win, preferred_element_type=jnp.float32) / l
            lse_s[p_idx, out_rows, :] = jnp.broadcast_to(m + jnp.log2(l), (QBLK, HEAD_DIM))

        return scores, finish

    return [item(*t) for t in tiles]


def _dilated_merge(g_ref, o_ref, o_s, lse_s, o_nat):
    n4 = o_ref.shape[0] // A_REGROUP
    for c4 in range(A_REGROUP):
        blk = slice(c4 * n4, (c4 + 1) * n4)
        nat = pl.ds(c4, n4, stride=A_REGROUP)
        lses = (lse_s[0, nat, :], lse_s[1, blk, :], lse_s[2, blk, :])
        outs = (o_s[0, nat, :], o_s[1, blk, :], o_s[2, blk, :])
        top = jnp.maximum(jnp.maximum(lses[0], lses[1]), lses[2])
        ws = [jnp.exp2(ls - top) for ls in lses]
        num = ws[0] * outs[0] + ws[1] * outs[1] + ws[2] * outs[2]
        o_nat[nat, :] = num / (ws[0] + ws[1] + ws[2])
    o_ref[...] = (o_nat[...] * _silu(g_ref[...].astype(jnp.float32))).astype(o_ref.dtype)


NA_VARIANTS = 8


def _na_variant_row(v, rows):
    return v if v <= 4 else rows - NA_VARIANTS + v


def _na_row_start(r, rows):
    kh = min(NA_ROWS, rows)
    return min(max(r - kh // 2, 0), rows - kh)


def _na_bias_kernel(rpb_ref, o_ref, pair_ref, *, rows):
    kh = min(NA_ROWS, rows)
    n_drow = 2 * NA_ROWS_MAX - 1
    n_dcol = 2 * NA_COLS - 1
    h = pl.program_id(0)
    shape = (GRID_W, 2 * GRID_W)
    lane = lax.broadcasted_iota(jnp.int32, shape, 1)
    c = lax.broadcasted_iota(jnp.int32, shape, 0)
    kc = lane % GRID_W
    second = lane >= GRID_W
    cs = jnp.clip(c - NA_COLS // 2, 0, GRID_W - NA_COLS)
    valid = (kc >= cs) & (kc < cs + NA_COLS)
    dcol = kc - c + (NA_COLS - 1)
    for d in range(n_drow - 1):
        acc = jnp.zeros(shape, jnp.float32)
        for j in range(n_dcol):
            lo = rpb_ref[(h * n_drow + d) * n_dcol + j]
            hi = rpb_ref[(h * n_drow + d + 1) * n_dcol + j]
            acc = jnp.where(dcol == j, jnp.where(second, hi, lo), acc)
        pair_ref[d] = jnp.where(valid, acc * LOG2E, MASKED)
    for v in range(NA_VARIANTS):
        r = _na_variant_row(v, rows)
        d0 = _na_row_start(r, rows) - r + (NA_ROWS_MAX - 1)
        for i in range(0, kh, 2):
            o_ref[v, :, i * GRID_W:(i + 2) * GRID_W] = pair_ref[d0 + i]


def _na_bias_table(rpb, rows):
    n_heads = rpb.shape[0]
    kh = min(NA_ROWS, rows)
    return pl.pallas_call(
        functools.partial(_na_bias_kernel, rows=rows),
        grid=(n_heads,),
        in_specs=[pl.BlockSpec(memory_space=pltpu.SMEM)],
        out_specs=pl.BlockSpec((None, NA_VARIANTS, GRID_W, kh * GRID_W), lambda hi: (hi, 0, 0, 0)),
        out_shape=jax.ShapeDtypeStruct((n_heads, NA_VARIANTS, GRID_W, kh * GRID_W), jnp.float32),
        scratch_shapes=[pltpu.VMEM((2 * NA_ROWS_MAX - 2, GRID_W, 2 * GRID_W), jnp.float32)],
        compiler_params=_params("parallel"),
        name="na_bias",
    )(rpb.reshape(-1))


def _na_items(q_ref, k_ref, v_ref, g_ref, bias_ref, o_ref):
    s_len = q_ref.shape[0]
    rows = s_len // GRID_W
    win = min(NA_ROWS, rows) * GRID_W

    def item(r):
        k_rows = slice(_na_row_start(r, rows) * GRID_W, _na_row_start(r, rows) * GRID_W + win)
        variant = r if r < 4 else (r - (rows - NA_VARIANTS) if r > rows - 4 else 4)
        sl = slice(r * GRID_W, (r + 1) * GRID_W)

        def scores():
            return _dot_nt(q_ref[sl, :], k_ref[k_rows, :])

        def finish(s):
            v_win = v_ref[k_rows, :]
            logits = bias_ref[variant] + s
            m = jnp.max(logits, axis=-1, keepdims=True)
            p = jnp.exp2(logits - m)
            den = jnp.sum(p, axis=-1, keepdims=True)
            o = jnp.dot(p.astype(v_win.dtype), v_win, preferred_element_type=jnp.float32) / den
            o_ref[sl, :] = (o * _silu(g_ref[sl, :].astype(jnp.float32))).astype(o_ref.dtype)

        return scores, finish

    return [item(r) for r in range(rows)]


def _even_mixers_kernel(slopes_ref, qa_ref, ka_ref, va_ref, ga_ref, qb_ref, kb_ref, vb_ref, gb_ref, bias_ref,
                        oa_ref, ob_ref, xf, x4, qp, kp, vp, o_s, lse_s, o_nat):
    a_items = _dilated_items(slopes_ref[pl.program_id(0)], qa_ref, ka_ref, va_ref, xf, x4, qp, kp, vp, o_s, lse_s)
    b_items = _na_items(qb_ref, kb_ref, vb_ref, gb_ref, bias_ref, ob_ref)
    _run_pipelined(_interleave(a_items, b_items), EVEN_AHEAD)
    _dilated_merge(ga_ref, oa_ref, o_s, lse_s, o_nat)


def _even_mixers(proj, slopes, bias, offs):
    b, s, _ = proj.shape
    col = lambda off: pl.BlockSpec((None, s, HEAD_DIM), lambda hi, bi: (bi, 0, off + hi))
    out_spec = pl.BlockSpec((None, s, HEAD_DIM), lambda hi, bi: (bi, 0, hi))
    out_shape = jax.ShapeDtypeStruct((b, s, MIX_W), jnp.bfloat16)
    n_pat = len(A_PATTERNS)
    f32_rows = lambda n: pltpu.VMEM((n, s, HEAD_DIM), jnp.float32)
    return pl.pallas_call(
        _even_mixers_kernel,
        grid=(N_HEADS, b),
        in_specs=[pl.BlockSpec(memory_space=pltpu.SMEM)] + [col(off) for off in offs]
        + [pl.BlockSpec((None,) + bias.shape[1:], lambda hi, bi: (hi, 0, 0, 0))],
        out_specs=(out_spec, out_spec),
        out_shape=(out_shape, out_shape),
        scratch_shapes=[f32_rows(3), f32_rows(3)]
        + [pltpu.VMEM((n_pat - 1, s, HEAD_DIM), jnp.bfloat16)] * 3
        + [f32_rows(n_pat), f32_rows(n_pat), pltpu.VMEM((s, HEAD_DIM), jnp.float32)],
        compiler_params=_params("parallel", "parallel"),
        name="even_mixers",
    )(slopes, *([proj] * 8), bias)


def _window_kernel(slopes_ref, sinks_ref, q_ref, k_ref, v_ref, g_ref, o_ref, v1, kt):
    s_len = q_ref.shape[0]
    v1[:, :HEAD_DIM] = v_ref[...]
    v1[:, HEAD_DIM:] = jnp.ones((s_len, HEAD_DIM), v1.dtype)
    kt[...] = k_ref[...].astype(jnp.float32).T.astype(kt.dtype)
    hk = pl.program_id(1)
    kw = 3 * QBLK
    rel0 = (lax.broadcasted_iota(jnp.int32, (QBLK, kw), 1) - lax.broadcasted_iota(jnp.int32, (QBLK, kw), 0))
    slopes = [slopes_ref[hk * GROUP + gi] for gi in range(GROUP)]
    sinks = [sinks_ref[hk * GROUP + gi] for gi in range(GROUP)]
    sinks = [sink * LOG2E for sink in sinks]
    bias = {}
    tiles = []
    for n in range(s_len // QBLK):
        q0 = n * QBLK
        w0 = min(max(q0 - QBLK, 0), s_len - kw)
        if (w0 - q0, 0) not in bias:
            dist = jnp.abs(rel0 + (w0 - q0)).astype(jnp.float32)
            dist = jnp.where(dist <= float(C_WINDOW), dist, FAR)
            for gi in range(GROUP):
                bias[(w0 - q0, gi)] = (slopes[gi] * -LOG2E) * dist
        tiles += [(slice(q0, q0 + QBLK), gi, slice(w0, w0 + kw), w0 - q0) for gi in range(GROUP)]

    def item(q_rows, gi, k_rows, off):
        lanes = slice(gi * HEAD_DIM, (gi + 1) * HEAD_DIM)

        def scores():
            return jnp.dot(q_ref[q_rows, lanes], kt[:, k_rows], preferred_element_type=jnp.float32)

        def finish(s):
            v_win = v1[k_rows, :]
            logits = bias[(off, gi)] + s
            m = jnp.maximum(jnp.max(logits, axis=-1, keepdims=True), sinks[gi])
            p = jnp.exp2((logits - m).astype(v_win.dtype))
            acc = jnp.dot(p, v_win, preferred_element_type=jnp.float32)
            o = acc[:, :HEAD_DIM] / (acc[:, HEAD_DIM:] + jnp.exp2(sinks[gi] - m))
            gate = _silu(g_ref[q_rows, lanes].astype(jnp.float32))
            o_ref[q_rows, lanes] = (o * gate).astype(o_ref.dtype)

        return scores, finish

    _run_pipelined([item(*t) for t in tiles], WINDOW_AHEAD)


def _gqa_specs(s, q_off, k_off, v_off, g_off):
    gw = GROUP * HEAD_DIM
    wide = lambda off: pl.BlockSpec((None, s, gw), lambda bi, hi: (bi, 0, off // GROUP + hi))
    col = lambda off: pl.BlockSpec((None, s, HEAD_DIM), lambda bi, hi: (bi, 0, off + hi))
    in_specs = [wide(q_off), col(k_off), col(v_off), wide(g_off)]
    out_spec = pl.BlockSpec((None, s, gw), lambda bi, hi: (bi, 0, hi))
    return in_specs, out_spec


def _windowed_sink_attention(proj, slopes, sinks, q_off, k_off, v_off, g_off):
    b, s, _ = proj.shape
    in_specs, out_spec = _gqa_specs(s, q_off, k_off, v_off, g_off)
    smem = pl.BlockSpec(memory_space=pltpu.SMEM)
    return pl.pallas_call(
        _window_kernel,
        grid=(b, KV_HEADS),
        in_specs=[smem, smem] + in_specs,
        out_specs=out_spec,
        out_shape=jax.ShapeDtypeStruct((b, s, MIX_W), jnp.bfloat16),
        scratch_shapes=[pltpu.VMEM((s, 2 * HEAD_DIM), jnp.bfloat16),
                        pltpu.VMEM((HEAD_DIM, s), jnp.bfloat16)],
        compiler_params=_params("parallel", "parallel"),
        name="window_sink_attn",
    )(slopes, sinks, proj, proj, proj, proj)


def _rope_tables(s):
    t = np.arange(s)
    quarter = HEAD_DIM // 4
    inv = jnp.asarray(ROPE_THETA, jnp.float32) ** (-jnp.arange(quarter, dtype=jnp.float32) / quarter)
    ang_r = jnp.asarray(t // GRID_W, jnp.float32)[:, None] * inv[None, :]
    ang_c = jnp.asarray(t % GRID_W, jnp.float32)[:, None] * inv[None, :]
    cos = jnp.concatenate([jnp.cos(ang_r), jnp.cos(ang_r), jnp.cos(ang_c), jnp.cos(ang_c)], axis=-1)
    sin = jnp.concatenate([-jnp.sin(ang_r), jnp.sin(ang_r), -jnp.sin(ang_c), jnp.sin(ang_c)], axis=-1)
    return cos, sin


def _norm_rope(x, gain, cos, sin):
    y = x * lax.rsqrt(jnp.mean(x * x, axis=-1, keepdims=True) + EPS) * gain
    quarter = HEAD_DIM // 4
    lane = lax.broadcasted_iota(jnp.int32, y.shape, 1)
    partner = jnp.where((lane // quarter) % 2 == 0,
                        pltpu.roll(y, HEAD_DIM - quarter, 1),
                        pltpu.roll(y, quarter, 1))
    return y * cos + partner * sin


D_ROWS_PER_DOT = 256


def _axial_kernel(q_ref, k_ref, v_ref, g_ref, cos_ref, sin_ref, gq_ref, gk_ref, o_ref,
                  kp, v1, q4, s_buf, m_buf, p_buf):
    s_len = q_ref.shape[0]
    n_blk = s_len // QBLK
    row_groups = [slice(r, r + D_ROWS_PER_DOT) for r in range(0, GROUP * QBLK, D_ROWS_PER_DOT)]
    kp[...] = _norm_rope(k_ref[...].astype(jnp.float32), gk_ref[...], cos_ref[...], sin_ref[...]).T.astype(kp.dtype)
    v1[:, :HEAD_DIM] = v_ref[...]
    v1[:, HEAD_DIM:] = jnp.ones((s_len, HEAD_DIM), v1.dtype)

    def block_rows(n):
        return pl.ds(pl.multiple_of(n * QBLK, QBLK), QBLK)

    def step(slot, n_prep, n_scores, n_probs, n_out):
        other = 1 - slot
        if n_prep is not None:
            rows = block_rows(n_prep)
            cos = cos_ref[rows, :]
            sin = sin_ref[rows, :]
            for gi in range(GROUP):
                q = _norm_rope(q_ref[rows, gi * HEAD_DIM:(gi + 1) * HEAD_DIM].astype(jnp.float32), gq_ref[...],
                               cos, sin)
                q4[other, gi * QBLK:(gi + 1) * QBLK, :] = (q * (SCALE * LOG2E)).astype(q4.dtype)
        if n_scores is not None:
            for hs in row_groups:
                s = jnp.dot(q4[slot, hs, :], kp[...], preferred_element_type=jnp.float32)
                s_buf[slot, hs, :] = s
                m_buf[slot, hs, :] = jnp.broadcast_to(jnp.max(s, axis=-1, keepdims=True), (D_ROWS_PER_DOT, HEAD_DIM))
        if n_probs is not None:
            for hs in row_groups:
                x = s_buf[other, hs, :] - jnp.tile(m_buf[other, hs, :], (1, s_len // HEAD_DIM))
                p_buf[other, hs, :] = jnp.exp2(x.astype(p_buf.dtype))
        if n_out is not None:
            rows = block_rows(n_out)
            for hs in row_groups:
                acc = jnp.dot(p_buf[slot, hs, :], v1[...], preferred_element_type=jnp.float32)
                o = acc[:, :HEAD_DIM] / acc[:, HEAD_DIM:]
                for gi in range(hs.start // QBLK, hs.stop // QBLK):
                    lanes = slice(gi * HEAD_DIM, (gi + 1) * HEAD_DIM)
                    gate = _silu(g_ref[rows, lanes].astype(jnp.float32))
                    o_ref[rows, lanes] = (o[gi * QBLK - hs.start:(gi + 1) * QBLK - hs.start, :] * gate).astype(o_ref.dtype)

    step(1, 0, None, None, None)
    step(0, 1, 0, None, None)
    step(1, 2, 1, 0, None)

    def pair(i, carry):
        n = 2 * i
        step(0, n + 3, n + 2, n + 1, n)
        step(1, jnp.minimum(n + 4, n_blk - 1), n + 3, n + 2, n + 1)
        return carry

    lax.fori_loop(0, n_blk // 2 - 1, pair, 0)
    step(0, None, None, n_blk - 1, n_blk - 2)
    step(1, None, None, None, n_blk - 1)


def _axial_rope_attention(proj, cos, sin, gq, gk, q_off, k_off, v_off, g_off):
    b, s, _ = proj.shape
    in_specs, out_spec = _gqa_specs(s, q_off, k_off, v_off, g_off)
    table = pl.BlockSpec((s, HEAD_DIM), lambda bi, hi: (0, 0))
    gain = pl.BlockSpec((1, HEAD_DIM), lambda bi, hi: (0, 0))
    stacked = GROUP * QBLK
    return pl.pallas_call(
        _axial_kernel,
        grid=(b, KV_HEADS),
        in_specs=in_specs + [table, table, gain, gain],
        out_specs=out_spec,
        out_shape=jax.ShapeDtypeStruct((b, s, MIX_W), jnp.bfloat16),
        scratch_shapes=[pltpu.VMEM((HEAD_DIM, s), jnp.bfloat16),
                        pltpu.VMEM((s, 2 * HEAD_DIM), jnp.bfloat16),
                        pltpu.VMEM((2, stacked, HEAD_DIM), jnp.bfloat16),
                        pltpu.VMEM((2, stacked, s), jnp.float32),
                        pltpu.VMEM((2, stacked, HEAD_DIM), jnp.float32),
                        pltpu.VMEM((2, stacked, s), jnp.bfloat16)],
        compiler_params=_params("parallel", "parallel"),
        name="axial_rope_attn",
    )(proj, proj, proj, proj, cos, sin, gq.reshape(1, HEAD_DIM), gk.reshape(1, HEAD_DIM))


def _alibi_slopes(n):
    return 2.0 ** (-8.0 * jnp.arange(1, n + 1, dtype=jnp.float32) / n)


def _q_column_scale(widths, q_sections):
    scale = np.ones(sum(widths), np.float32)
    for sec in q_sections:
        start = sum(widths[:sec])
        scale[start:start + widths[sec]] = SCALE * LOG2E
    return jnp.asarray(scale)


def kernel(x, norm_g, final_g, w_in_even, w_out_even, rpb_b, w_in_odd, w_out_odd, sinks_c, qnorm_d, knorm_d):
    b, s, d = x.shape
    bf16 = jnp.bfloat16
    slopes = _alibi_slopes(N_HEADS)
    cos, sin = _rope_tables(s)
    blk = lambda width: width // HEAD_DIM

    h = x.reshape(b * s, d)
    y = _rms_norm_rows(h, norm_g[0], bf16)
    for layer in range(DEPTH):
        i = layer // 2
        final = layer == DEPTH - 1
        g_next = final_g if final else norm_g[layer + 1]
        if layer % 2 == 0:
            widths = [MIX_W] * 8
            offs = [blk(MIX_W) * j for j in range(8)]
            proj = _in_proj(y, w_in_even, i, _q_column_scale(widths, (0, 4))).reshape(b, s, -1)
            mix1, mix2 = _even_mixers(proj, slopes, _na_bias_table(rpb_b[i], s // GRID_W), offs)
            w_out = w_out_even
        else:
            widths = [MIX_W, KV_W, KV_W, MIX_W] * 2
            offs = [blk(int(sum(widths[:j]))) for j in range(8)]
            proj = _in_proj(y, w_in_odd, i, _q_column_scale(widths, (0,))).reshape(b, s, -1)
            mix1 = _windowed_sink_attention(proj, slopes, sinks_c[i], *offs[:4])
            mix2 = _axial_rope_attention(proj, cos, sin, qnorm_d[i], knorm_d[i], *offs[4:])
            w_out = w_out_odd
        res = _out_proj(mix1.reshape(b * s, MIX_W), mix2.reshape(b * s, MIX_W), w_out, i, h, g_next, final)
        if final:
            return res.reshape(b, s, d)
        h, y = res
```

```python
import functools

import jax
import jax.numpy as jnp
import numpy as np
from jax import lax
from jax.experimental import pallas as pl
from jax.experimental.pallas import tpu as pltpu

DEPTH = 4
HEAD_DIM = 128
GRID_W = 64
EPS = 1e-6
N_HEADS = 8
KV_HEADS = 2
GROUP = N_HEADS // KV_HEADS
MIX_W = N_HEADS * HEAD_DIM
KV_W = KV_HEADS * HEAD_DIM
A_PATTERNS = ((128, 1), (512, 4), (2048, 16))
A_SIDE = 64
NA_ROWS = 8
NA_COLS = 16
NA_ROWS_MAX = 8
C_WINDOW = 128
QBLK = 128
ROPE_THETA = 10000.0
SCALE = HEAD_DIM ** -0.5
LOG2E = 1.4426950408889634
MASKED = -1e30
FAR = 1e9
EVEN_AHEAD = 5
WINDOW_AHEAD = 6

VMEM_LIMIT = 56 * 1024 * 1024
IN_PROJ_ROWS = 2048
IN_PROJ_COLS = 1024
OUT_PROJ_ROWS = 512


def _params(*sem):
    return pltpu.CompilerParams(dimension_semantics=sem, vmem_limit_bytes=VMEM_LIMIT)


def _silu(x):
    half = 0.5 * x
    return half + half * jnp.tanh(half)


def _dot_nt(a, b):
    return lax.dot_general(a, b, (((1,), (1,)), ((), ())), preferred_element_type=jnp.float32)


def _run_pipelined(items, ahead):
    pending = [scores() for scores, _ in items[:ahead]]
    for idx, (_, finish) in enumerate(items):
        if idx + ahead < len(items):
            pending.append(items[idx + ahead][0]())
        finish(pending.pop(0))


def _interleave(a, b):
    keyed = [((i + 0.5) / len(a), 0, x) for i, x in enumerate(a)] + [((i + 0.5) / len(b), 1, x) for i, x in enumerate(b)]
    return [x for _, _, x in sorted(keyed, key=lambda t: t[:2])]


def _norm_kernel(h_ref, g_ref, y_ref):
    x = h_ref[...]
    y = x * lax.rsqrt(jnp.mean(x * x, axis=-1, keepdims=True) + EPS)
    y_ref[...] = (y * g_ref[...]).astype(y_ref.dtype)


def _rms_norm_rows(h, g, out_dtype, tm=1024):
    m, d = h.shape
    return pl.pallas_call(
        _norm_kernel,
        grid=(m // tm,),
        in_specs=[pl.BlockSpec((tm, d), lambda i: (i, 0)), pl.BlockSpec((1, d), lambda i: (0, 0))],
        out_specs=pl.BlockSpec((tm, d), lambda i: (i, 0)),
        out_shape=jax.ShapeDtypeStruct((m, d), out_dtype),
        compiler_params=_params("parallel"),
        name="rms_norm",
    )(h, g.reshape(1, d))


def _in_proj_kernel(x_ref, w_ref, scale_ref, o_ref, wb_ref):
    @pl.when(pl.program_id(1) == 0)
    def _():
        wb_ref[...] = (w_ref[...] * scale_ref[...]).astype(wb_ref.dtype)

    o_ref[...] = jnp.dot(x_ref[...], wb_ref[...], preferred_element_type=jnp.float32).astype(o_ref.dtype)


def _in_proj(y, w_stack, layer, col_scale, tm=IN_PROJ_ROWS, tn=IN_PROJ_COLS):
    m, k = y.shape
    n = w_stack.shape[2]
    return pl.pallas_call(
        _in_proj_kernel,
        grid=(n // tn, m // tm),
        in_specs=[pl.BlockSpec((tm, k), lambda j, i: (i, 0)),
                  pl.BlockSpec((None, k, tn), lambda j, i: (layer, 0, j)),
                  pl.BlockSpec((1, tn), lambda j, i: (0, j))],
        out_specs=pl.BlockSpec((tm, tn), lambda j, i: (i, j)),
        out_shape=jax.ShapeDtypeStruct((m, n), jnp.bfloat16),
        scratch_shapes=[pltpu.VMEM((k, tn), jnp.bfloat16)],
        compiler_params=_params("arbitrary", "arbitrary"),
        name="in_proj",
    )(y, w_stack, col_scale.reshape(1, n))


def _out_proj_kernel(m1_ref, m2_ref, w_ref, h_ref, g_ref, *refs, final):
    out_refs, wb_ref = refs[:-1], refs[-1]
    half = m1_ref.shape[1]

    @pl.when(pl.program_id(0) == 0)
    def _():
        wb_ref[...] = w_ref[...].astype(wb_ref.dtype)

    acc = jnp.dot(m1_ref[...], wb_ref[:half, :], preferred_element_type=jnp.float32)
    acc = acc + jnp.dot(m2_ref[...], wb_ref[half:, :], preferred_element_type=jnp.float32)
    h = h_ref[...] + acc
    y = h * lax.rsqrt(jnp.mean(h * h, axis=-1, keepdims=True) + EPS) * g_ref[...]
    if final:
        out_refs[0][...] = y
    else:
        out_refs[0][...] = h
        out_refs[1][...] = y.astype(out_refs[1].dtype)


def _out_proj(mix1, mix2, w_stack, layer, h, g_next, final, tm=OUT_PROJ_ROWS):
    m, d = h.shape
    half = mix1.shape[1]
    row = lambda i: (i, 0)
    if final:
        out_shape = jax.ShapeDtypeStruct((m, d), jnp.float32)
        out_specs = pl.BlockSpec((tm, d), row)
    else:
        out_shape = (jax.ShapeDtypeStruct((m, d), jnp.float32), jax.ShapeDtypeStruct((m, d), jnp.bfloat16))
        out_specs = (pl.BlockSpec((tm, d), row), pl.BlockSpec((tm, d), row))
    return pl.pallas_call(
        functools.partial(_out_proj_kernel, final=final),
        grid=(m // tm,),
        in_specs=[pl.BlockSpec((tm, half), row), pl.BlockSpec((tm, half), row),
                  pl.BlockSpec((None, 2 * half, d), lambda i: (layer, 0, 0), pipeline_mode=pl.Buffered(1)),
                  pl.BlockSpec((tm, d), row), pl.BlockSpec((1, d), lambda i: (0, 0))],
        out_specs=out_specs,
        out_shape=out_shape,
        scratch_shapes=[pltpu.VMEM((2 * half, d), jnp.bfloat16)],
        compiler_params=_params("arbitrary"),
        name="out_proj_final" if final else "out_proj",
    )(mix1, mix2, w_stack, h, g_next.reshape(1, d))


A_REGROUP = 4


def _dilated_items(slope, q_ref, k_ref, v_ref, xf, x4, qp, kp, vp, o_s, lse_s):
    assert tuple(d for _, d in A_PATTERNS) == (1, A_REGROUP, A_REGROUP ** 2)
    s_len = q_ref.shape[0]
    n4 = s_len // A_REGROUP
    n16 = n4 // A_REGROUP

    for a, (src, dst) in enumerate(((q_ref, qp), (k_ref, kp), (v_ref, vp))):
        xf[a] = src[...].astype(jnp.float32)
        for c4 in range(A_REGROUP):
            x = xf[a, pl.ds(c4, n4, stride=A_REGROUP), :]
            x4[a, c4 * n4:(c4 + 1) * n4, :] = x
            dst[0, c4 * n4:(c4 + 1) * n4, :] = x.astype(dst.dtype)
        for c4 in range(A_REGROUP):
            for c2 in range(A_REGROUP):
                c16 = A_REGROUP * c2 + c4
                x = x4[a, pl.ds(c4 * n4 + c2, n16, stride=A_REGROUP), :]
                dst[1, c16 * n16:(c16 + 1) * n16, :] = x.astype(dst.dtype)
    srcs = [(q_ref, k_ref, v_ref), (qp.at[0], kp.at[0], vp.at[0]), (qp.at[1], kp.at[1], vp.at[1])]

    tiles = []
    bias = {}
    for p_idx, (_, dil) in enumerate(A_PATTERNS):
        n = s_len // dil
        kw = min(2 * QBLK, n)
        rel0 = (lax.broadcasted_iota(jnp.int32, (QBLK, kw), 1)
                - lax.broadcasted_iota(jnp.int32, (QBLK, kw), 0))
        for t in range(s_len // QBLK):
            c, i0 = divmod(t * QBLK, n)
            w0 = min(max(i0 - A_SIDE, 0), n - kw)
            if (p_idx, w0 - i0) not in bias:
                dist = jnp.abs(rel0 + (w0 - i0)).astype(jnp.float32)
                bias[(p_idx, w0 - i0)] = (slope * (-dil * LOG2E)) * jnp.where(dist <= float(A_SIDE), dist, FAR)
            if p_idx < 2:
                out_rows = pl.ds(c * n + i0, QBLK)
            else:
                c2, c4 = divmod(c, A_REGROUP)
                out_rows = pl.ds(c4 * n4 + c2 + A_REGROUP * i0, QBLK, stride=A_REGROUP)
            tiles.append((p_idx, slice(c * n + i0, c * n + i0 + QBLK), slice(c * n + w0, c * n + w0 + kw),
                          w0 - i0, out_rows))

    def item(p_idx, q_rows, k_rows, off, out_rows):
        q_src, k_src, v_src = srcs[p_idx]

        def scores():
            return _dot_nt(q_src[q_rows, :], k_src[k_rows, :])

        def finish(s):
            v_win = v_src[k_rows, :]
            logits = bias[(p_idx, off)] + s
            m = jnp.max(logits, axis=-1, keepdims=True)
            p = jnp.exp2(logits - m)
            l = jnp.sum(p, axis=-1, keepdims=True)
            o_s[p_idx, out_rows, :] = jnp.dot(p.astype(v_win.dtype), v_win, preferred_element_type=jnp.float32) / l
            lse_s[p_idx, out_rows, :] = jnp.broadcast_to(m + jnp.log2(l), (QBLK, HEAD_DIM))

        return scores, finish

    return [item(*t) for t in tiles]


def _dilated_merge(g_ref, o_ref, o_s, lse_s, o_nat):
    n4 = o_ref.shape[0] // A_REGROUP
    for c4 in range(A_REGROUP):
        blk = slice(c4 * n4, (c4 + 1) * n4)
        nat = pl.ds(c4, n4, stride=A_REGROUP)
        lses = (lse_s[0, nat, :], lse_s[1, blk, :], lse_s[2, blk, :])
        outs = (o_s[0, nat, :], o_s[1, blk, :], o_s[2, blk, :])
        top = jnp.maximum(jnp.maximum(lses[0], lses[1]), lses[2])
        ws = [jnp.exp2(ls - top) for ls in lses]
        num = ws[0] * outs[0] + ws[1] * outs[1] + ws[2] * outs[2]
        o_nat[nat, :] = num / (ws[0] + ws[1] + ws[2])
    o_ref[...] = (o_nat[...] * _silu(g_ref[...].astype(jnp.float32))).astype(o_ref.dtype)


NA_VARIANTS = 8


def _na_variant_row(v, rows):
    return v if v <= 4 else rows - NA_VARIANTS + v


def _na_row_start(r, rows):
    kh = min(NA_ROWS, rows)
    return min(max(r - kh // 2, 0), rows - kh)


def _na_bias_fill(rpb_ref, h, o_ref, pair_ref, rows):
    kh = min(NA_ROWS, rows)
    n_drow = 2 * NA_ROWS_MAX - 1
    n_dcol = 2 * NA_COLS - 1
    shape = (GRID_W, 2 * GRID_W)
    lane = lax.broadcasted_iota(jnp.int32, shape, 1)
    c = lax.broadcasted_iota(jnp.int32, shape, 0)
    kc = lane % GRID_W
    second = lane >= GRID_W
    cs = jnp.clip(c - NA_COLS // 2, 0, GRID_W - NA_COLS)
    valid = (kc >= cs) & (kc < cs + NA_COLS)
    dcol = kc - c + (NA_COLS - 1)
    for d in range(n_drow - 1):
        acc = jnp.zeros(shape, jnp.float32)
        for j in range(n_dcol):
            lo = rpb_ref[(h * n_drow + d) * n_dcol + j]
            hi = rpb_ref[(h * n_drow + d + 1) * n_dcol + j]
            acc = jnp.where(dcol == j, jnp.where(second, hi, lo), acc)
        pair_ref[d] = jnp.where(valid, acc * LOG2E, MASKED)
    for v in range(NA_VARIANTS):
        r = _na_variant_row(v, rows)
        d0 = _na_row_start(r, rows) - r + (NA_ROWS_MAX - 1)
        for i in range(0, kh, 2):
            o_ref[v, :, i * GRID_W:(i + 2) * GRID_W] = pair_ref[d0 + i]


def _na_items(q_ref, k_ref, v_ref, g_ref, bias_ref, o_ref):
    s_len = q_ref.shape[0]
    rows = s_len // GRID_W
    win = min(NA_ROWS, rows) * GRID_W

    def item(r):
        k_rows = slice(_na_row_start(r, rows) * GRID_W, _na_row_start(r, rows) * GRID_W + win)
        variant = r if r < 4 else (r - (rows - NA_VARIANTS) if r > rows - 4 else 4)
        sl = slice(r * GRID_W, (r + 1) * GRID_W)

        def scores():
            return _dot_nt(q_ref[sl, :], k_ref[k_rows, :])

        def finish(s):
            v_win = v_ref[k_rows, :]
            logits = bias_ref[variant] + s
            m = jnp.max(logits, axis=-1, keepdims=True)
            p = jnp.exp2(logits - m)
            den = jnp.sum(p, axis=-1, keepdims=True)
            o = jnp.dot(p.astype(v_win.dtype), v_win, preferred_element_type=jnp.float32) / den
            o_ref[sl, :] = (o * _silu(g_ref[sl, :].astype(jnp.float32))).astype(o_ref.dtype)

        return scores, finish

    return [item(r) for r in range(rows)]


def _even_mixers_kernel(slopes_ref, rpb_ref, qa_ref, ka_ref, va_ref, ga_ref, qb_ref, kb_ref, vb_ref, gb_ref,
                        oa_ref, ob_ref, xf, x4, qp, kp, vp, o_s, lse_s, o_nat, bias_ref, pair_ref):
    @pl.when(pl.program_id(1) == 0)
    def _():
        _na_bias_fill(rpb_ref, pl.program_id(0), bias_ref, pair_ref, qb_ref.shape[0] // GRID_W)

    a_items = _dilated_items(slopes_ref[pl.program_id(0)], qa_ref, ka_ref, va_ref, xf, x4, qp, kp, vp, o_s, lse_s)
    b_items = _na_items(qb_ref, kb_ref, vb_ref, gb_ref, bias_ref, ob_ref)
    _run_pipelined(_interleave(a_items, b_items), EVEN_AHEAD)
    _dilated_merge(ga_ref, oa_ref, o_s, lse_s, o_nat)


def _even_mixers(proj, slopes, rpb, offs):
    b, s, _ = proj.shape
    kh = min(NA_ROWS, s // GRID_W)
    smem = pl.BlockSpec(memory_space=pltpu.SMEM)
    col = lambda off: pl.BlockSpec((None, s, HEAD_DIM), lambda hi, bi: (bi, 0, off + hi))
    out_spec = pl.BlockSpec((None, s, HEAD_DIM), lambda hi, bi: (bi, 0, hi))
    out_shape = jax.ShapeDtypeStruct((b, s, MIX_W), jnp.bfloat16)
    n_pat = len(A_PATTERNS)
    f32_rows = lambda n: pltpu.VMEM((n, s, HEAD_DIM), jnp.float32)
    return pl.pallas_call(
        _even_mixers_kernel,
        grid=(N_HEADS, b),
        in_specs=[smem, smem] + [col(off) for off in offs],
        out_specs=(out_spec, out_spec),
        out_shape=(out_shape, out_shape),
        scratch_shapes=[f32_rows(3), f32_rows(3)]
        + [pltpu.VMEM((n_pat - 1, s, HEAD_DIM), jnp.bfloat16)] * 3
        + [f32_rows(n_pat), f32_rows(n_pat), pltpu.VMEM((s, HEAD_DIM), jnp.float32)]
        + [pltpu.VMEM((NA_VARIANTS, GRID_W, kh * GRID_W), jnp.float32),
           pltpu.VMEM((2 * NA_ROWS_MAX - 2, GRID_W, 2 * GRID_W), jnp.float32)],
        compiler_params=_params("parallel", "arbitrary"),
        name="even_mixers",
    )(slopes, rpb.reshape(-1), *([proj] * 8))


def _window_kernel(slopes_ref, sinks_ref, q_ref, k_ref, v_ref, g_ref, o_ref, v1, kt):
    s_len = q_ref.shape[0]
    v1[:, :HEAD_DIM] = v_ref[...]
    v1[:, HEAD_DIM:] = jnp.ones((s_len, HEAD_DIM), v1.dtype)
    kt[...] = k_ref[...].astype(jnp.float32).T.astype(kt.dtype)
    hk = pl.program_id(1)
    kw = 3 * QBLK
    rel0 = (lax.broadcasted_iota(jnp.int32, (QBLK, kw), 1) - lax.broadcasted_iota(jnp.int32, (QBLK, kw), 0))
    slopes = [slopes_ref[hk * GROUP + gi] for gi in range(GROUP)]
    sinks = [sinks_ref[hk * GROUP + gi] for gi in range(GROUP)]
    sinks = [sink * LOG2E for sink in sinks]
    bias = {}
    tiles = []
    for n in range(s_len // QBLK):
        q0 = n * QBLK
        w0 = min(max(q0 - QBLK, 0), s_len - kw)
        if (w0 - q0, 0) not in bias:
            dist = jnp.abs(rel0 + (w0 - q0)).astype(jnp.float32)
            dist = jnp.where(dist <= float(C_WINDOW), dist, FAR)
            for gi in range(GROUP):
                bias[(w0 - q0, gi)] = (slopes[gi] * -LOG2E) * dist
        tiles += [(slice(q0, q0 + QBLK), gi, slice(w0, w0 + kw), w0 - q0) for gi in range(GROUP)]

    def item(q_rows, gi, k_rows, off):
        lanes = slice(gi * HEAD_DIM, (gi + 1) * HEAD_DIM)

        def scores():
            return jnp.dot(q_ref[q_rows, lanes], kt[:, k_rows], preferred_element_type=jnp.float32)

        def finish(s):
            v_win = v1[k_rows, :]
            logits = bias[(off, gi)] + s
            m = jnp.maximum(jnp.max(logits, axis=-1, keepdims=True), sinks[gi])
            p = jnp.exp2((logits - m).astype(v_win.dtype))
            acc = jnp.dot(p, v_win, preferred_element_type=jnp.float32)
            o = acc[:, :HEAD_DIM] / (acc[:, HEAD_DIM:] + jnp.exp2(sinks[gi] - m))
            gate = _silu(g_ref[q_rows, lanes].astype(jnp.float32))
            o_ref[q_rows, lanes] = (o * gate).astype(o_ref.dtype)

        return scores, finish

    _run_pipelined([item(*t) for t in tiles], WINDOW_AHEAD)


def _gqa_specs(s, q_off, k_off, v_off, g_off):
    gw = GROUP * HEAD_DIM
    wide = lambda off: pl.BlockSpec((None, s, gw), lambda bi, hi: (bi, 0, off // GROUP + hi))
    col = lambda off: pl.BlockSpec((None, s, HEAD_DIM), lambda bi, hi: (bi, 0, off + hi))
    in_specs = [wide(q_off), col(k_off), col(v_off), wide(g_off)]
    out_spec = pl.BlockSpec((None, s, gw), lambda bi, hi: (bi, 0, hi))
    return in_specs, out_spec


def _windowed_sink_attention(proj, slopes, sinks, q_off, k_off, v_off, g_off):
    b, s, _ = proj.shape
    in_specs, out_spec = _gqa_specs(s, q_off, k_off, v_off, g_off)
    smem = pl.BlockSpec(memory_space=pltpu.SMEM)
    return pl.pallas_call(
        _window_kernel,
        grid=(b, KV_HEADS),
        in_specs=[smem, smem] + in_specs,
        out_specs=out_spec,
        out_shape=jax.ShapeDtypeStruct((b, s, MIX_W), jnp.bfloat16),
        scratch_shapes=[pltpu.VMEM((s, 2 * HEAD_DIM), jnp.bfloat16),
                        pltpu.VMEM((HEAD_DIM, s), jnp.bfloat16)],
        compiler_params=_params("parallel", "parallel"),
        name="window_sink_attn",
    )(slopes, sinks, proj, proj, proj, proj)


def _rope_tables(s):
    t = np.arange(s)
    quarter = HEAD_DIM // 4
    inv = jnp.asarray(ROPE_THETA, jnp.float32) ** (-jnp.arange(quarter, dtype=jnp.float32) / quarter)
    ang_r = jnp.asarray(t // GRID_W, jnp.float32)[:, None] * inv[None, :]
    ang_c = jnp.asarray(t % GRID_W, jnp.float32)[:, None] * inv[None, :]
    cos = jnp.concatenate([jnp.cos(ang_r), jnp.cos(ang_r), jnp.cos(ang_c), jnp.cos(ang_c)], axis=-1)
    sin = jnp.concatenate([-jnp.sin(ang_r), jnp.sin(ang_r), -jnp.sin(ang_c), jnp.sin(ang_c)], axis=-1)
    return cos, sin


def _norm_rope(x, gain, cos, sin):
    y = x * lax.rsqrt(jnp.mean(x * x, axis=-1, keepdims=True) + EPS) * gain
    quarter = HEAD_DIM // 4
    lane = lax.broadcasted_iota(jnp.int32, y.shape, 1)
    partner = jnp.where((lane // quarter) % 2 == 0,
                        pltpu.roll(y, HEAD_DIM - quarter, 1),
                        pltpu.roll(y, quarter, 1))
    return y * cos + partner * sin


D_ROWS_PER_DOT = 256


def _axial_kernel(q_ref, k_ref, v_ref, g_ref, cos_ref, sin_ref, gq_ref, gk_ref, o_ref,
                  kp, v1, q4, s_buf, m_buf, p_buf):
    s_len = q_ref.shape[0]
    n_blk = s_len // QBLK
    row_groups = [slice(r, r + D_ROWS_PER_DOT) for r in range(0, GROUP * QBLK, D_ROWS_PER_DOT)]
    kp[...] = _norm_rope(k_ref[...].astype(jnp.float32), gk_ref[...], cos_ref[...], sin_ref[...]).T.astype(kp.dtype)
    v1[:, :HEAD_DIM] = v_ref[...]
    v1[:, HEAD_DIM:] = jnp.ones((s_len, HEAD_DIM), v1.dtype)

    def block_rows(n):
        return pl.ds(pl.multiple_of(n * QBLK, QBLK), QBLK)

    def step(slot, n_prep, n_scores, n_probs, n_out):
        other = 1 - slot
        if n_prep is not None:
            rows = block_rows(n_prep)
            cos = cos_ref[rows, :]
            sin = sin_ref[rows, :]
            for gi in range(GROUP):
                q = _norm_rope(q_ref[rows, gi * HEAD_DIM:(gi + 1) * HEAD_DIM].astype(jnp.float32), gq_ref[...],
                               cos, sin)
                q4[other, gi * QBLK:(gi + 1) * QBLK, :] = (q * (SCALE * LOG2E)).astype(q4.dtype)
        if n_scores is not None:
            for hs in row_groups:
                s = jnp.dot(q4[slot, hs, :], kp[...], preferred_element_type=jnp.float32)
                s_buf[slot, hs, :] = s
                m_buf[slot, hs, :] = jnp.broadcast_to(jnp.max(s, axis=-1, keepdims=True), (D_ROWS_PER_DOT, HEAD_DIM))
        if n_probs is not None:
            for hs in row_groups:
                x = s_buf[other, hs, :] - jnp.tile(m_buf[other, hs, :], (1, s_len // HEAD_DIM))
                p_buf[other, hs, :] = jnp.exp2(x.astype(p_buf.dtype))
        if n_out is not None:
            rows = block_rows(n_out)
            for hs in row_groups:
                acc = jnp.dot(p_buf[slot, hs, :], v1[...], preferred_element_type=jnp.float32)
                o = acc[:, :HEAD_DIM] / acc[:, HEAD_DIM:]
                for gi in range(hs.start // QBLK, hs.stop // QBLK):
                    lanes = slice(gi * HEAD_DIM, (gi + 1) * HEAD_DIM)
                    gate = _silu(g_ref[rows, lanes].astype(jnp.float32))
                    o_ref[rows, lanes] = (o[gi * QBLK - hs.start:(gi + 1) * QBLK - hs.start, :] * gate).astype(o_ref.dtype)

    step(1, 0, None, None, None)
    step(0, 1, 0, None, None)
    step(1, 2, 1, 0, None)

    def pair(i, carry):
        n = 2 * i
        step(0, n + 3, n + 2, n + 1, n)
        step(1, jnp.minimum(n + 4, n_blk - 1), n + 3, n + 2, n + 1)
        return carry

    lax.fori_loop(0, n_blk // 2 - 1, pair, 0)
    step(0, None, None, n_blk - 1, n_blk - 2)
    step(1, None, None, None, n_blk - 1)


def _axial_rope_attention(proj, cos, sin, gq, gk, q_off, k_off, v_off, g_off):
    b, s, _ = proj.shape
    in_specs, out_spec = _gqa_specs(s, q_off, k_off, v_off, g_off)
    table = pl.BlockSpec((s, HEAD_DIM), lambda bi, hi: (0, 0))
    gain = pl.BlockSpec((1, HEAD_DIM), lambda bi, hi: (0, 0))
    stacked = GROUP * QBLK
    return pl.pallas_call(
        _axial_kernel,
        grid=(b, KV_HEADS),
        in_specs=in_specs + [table, table, gain, gain],
        out_specs=out_spec,
        out_shape=jax.ShapeDtypeStruct((b, s, MIX_W), jnp.bfloat16),
        scratch_shapes=[pltpu.VMEM((HEAD_DIM, s), jnp.bfloat16),
                        pltpu.VMEM((s, 2 * HEAD_DIM), jnp.bfloat16),
                        pltpu.VMEM((2, stacked, HEAD_DIM), jnp.bfloat16),
                        pltpu.VMEM((2, stacked, s), jnp.float32),
                        pltpu.VMEM((2, stacked, HEAD_DIM), jnp.float32),
                        pltpu.VMEM((2, stacked, s), jnp.bfloat16)],
        compiler_params=_params("parallel", "parallel"),
        name="axial_rope_attn",
    )(proj, proj, proj, proj, cos, sin, gq.reshape(1, HEAD_DIM), gk.reshape(1, HEAD_DIM))


def _alibi_slopes(n):
    return 2.0 ** (-8.0 * jnp.arange(1, n + 1, dtype=jnp.float32) / n)


def _q_column_scale(widths, q_sections):
    scale = np.ones(sum(widths), np.float32)
    for sec in q_sections:
        start = sum(widths[:sec])
        scale[start:start + widths[sec]] = SCALE * LOG2E
    return jnp.asarray(scale)


def kernel(x, norm_g, final_g, w_in_even, w_out_even, rpb_b, w_in_odd, w_out_odd, sinks_c, qnorm_d, knorm_d):
    b, s, d = x.shape
    bf16 = jnp.bfloat16
    slopes = _alibi_slopes(N_HEADS)
    cos, sin = _rope_tables(s)
    blk = lambda width: width // HEAD_DIM

    h = x.reshape(b * s, d)
    y = _rms_norm_rows(h, norm_g[0], bf16)
    for layer in range(DEPTH):
        i = layer // 2
        final = layer == DEPTH - 1
        g_next = final_g if final else norm_g[layer + 1]
        if layer % 2 == 0:
            widths = [MIX_W] * 8
            offs = [blk(MIX_W) * j for j in range(8)]
            proj = _in_proj(y, w_in_even, i, _q_column_scale(widths, (0, 4))).reshape(b, s, -1)
            mix1, mix2 = _even_mixers(proj, slopes, rpb_b[i], offs)
            w_out = w_out_even
        else:
            widths = [MIX_W, KV_W, KV_W, MIX_W] * 2
            offs = [blk(int(sum(widths[:j]))) for j in range(8)]
            proj = _in_proj(y, w_in_odd, i, _q_column_scale(widths, (0,))).reshape(b, s, -1)
            mix1 = _windowed_sink_attention(proj, slopes, sinks_c[i], *offs[:4])
            mix2 = _axial_rope_attention(proj, cos, sin, qnorm_d[i], knorm_d[i], *offs[4:])
            w_out = w_out_odd
        res = _out_proj(mix1.reshape(b * s, MIX_W), mix2.reshape(b * s, MIX_W), w_out, i, h, g_next, final)
        if final:
            return res.reshape(b, s, d)
        h, y = res
```

```python
import functools

import jax
import jax.numpy as jnp
import numpy as np
from jax import lax
from jax.experimental import pallas as pl
from jax.experimental.pallas import tpu as pltpu

DEPTH = 4
HEAD_DIM = 128
GRID_W = 64
EPS = 1e-6
N_HEADS = 8
KV_HEADS = 2
GROUP = N_HEADS // KV_HEADS
MIX_W = N_HEADS * HEAD_DIM
KV_W = KV_HEADS * HEAD_DIM
A_PATTERNS = ((128, 1), (512, 4), (2048, 16))
A_SIDE = 64
NA_ROWS = 8
NA_COLS = 16
NA_ROWS_MAX = 8
C_WINDOW = 128
QBLK = 128
ROPE_THETA = 10000.0
SCALE = HEAD_DIM ** -0.5
LOG2E = 1.4426950408889634
MASKED = -1e30
FAR = 1e9
EVEN_AHEAD = 5
WINDOW_AHEAD = 6

VMEM_LIMIT = 56 * 1024 * 1024
IN_PROJ_ROWS = 2048
IN_PROJ_COLS = 1024
OUT_PROJ_ROWS = 512


def _params(*sem):
    return pltpu.CompilerParams(dimension_semantics=sem, vmem_limit_bytes=VMEM_LIMIT)


def _silu(x):
    half = 0.5 * x
    return half + half * jnp.tanh(half)


def _dot_nt(a, b):
    return lax.dot_general(a, b, (((1,), (1,)), ((), ())), preferred_element_type=jnp.float32)


def _run_pipelined(items, ahead):
    pending = [scores() for scores, _ in items[:ahead]]
    for idx, (_, finish) in enumerate(items):
        if idx + ahead < len(items):
            pending.append(items[idx + ahead][0]())
        finish(pending.pop(0))


def _interleave(a, b):
    keyed = [((i + 0.5) / len(a), 0, x) for i, x in enumerate(a)] + [((i + 0.5) / len(b), 1, x) for i, x in enumerate(b)]
    return [x for _, _, x in sorted(keyed, key=lambda t: t[:2])]


def _norm_kernel(h_ref, g_ref, y_ref):
    x = h_ref[...]
    y = x * lax.rsqrt(jnp.mean(x * x, axis=-1, keepdims=True) + EPS)
    y_ref[...] = (y * g_ref[...]).astype(y_ref.dtype)


def _rms_norm_rows(h, g, out_dtype, tm=1024):
    m, d = h.shape
    return pl.pallas_call(
        _norm_kernel,
        grid=(m // tm,),
        in_specs=[pl.BlockSpec((tm, d), lambda i: (i, 0)), pl.BlockSpec((1, d), lambda i: (0, 0))],
        out_specs=pl.BlockSpec((tm, d), lambda i: (i, 0)),
        out_shape=jax.ShapeDtypeStruct((m, d), out_dtype),
        compiler_params=_params("parallel"),
        name="rms_norm",
    )(h, g.reshape(1, d))


def _in_proj_kernel(x_ref, w_ref, scale_ref, o_ref, wb_ref):
    @pl.when(pl.program_id(1) == 0)
    def _():
        wb_ref[...] = (w_ref[...] * scale_ref[...]).astype(wb_ref.dtype)

    o_ref[...] = jnp.dot(x_ref[...], wb_ref[...], preferred_element_type=jnp.float32).astype(o_ref.dtype)


def _in_proj(y, w_stack, layer, col_scale, tm=IN_PROJ_ROWS, tn=IN_PROJ_COLS):
    m, k = y.shape
    n = w_stack.shape[2]
    return pl.pallas_call(
        _in_proj_kernel,
        grid=(n // tn, m // tm),
        in_specs=[pl.BlockSpec((tm, k), lambda j, i: (i, 0)),
                  pl.BlockSpec((None, k, tn), lambda j, i: (layer, 0, j)),
                  pl.BlockSpec((1, tn), lambda j, i: (0, j))],
        out_specs=pl.BlockSpec((tm, tn), lambda j, i: (i, j)),
        out_shape=jax.ShapeDtypeStruct((m, n), jnp.bfloat16),
        scratch_shapes=[pltpu.VMEM((k, tn), jnp.bfloat16)],
        compiler_params=_params("arbitrary", "arbitrary"),
        name="in_proj",
    )(y, w_stack, col_scale.reshape(1, n))


def _out_proj_kernel(m1_ref, m2_ref, w_ref, h_ref, g_ref, *refs, final):
    out_refs, wb_ref = refs[:-1], refs[-1]

    @pl.when(pl.program_id(0) == 0)
    def _():
        wb_ref[...] = w_ref[...].astype(wb_ref.dtype)

    mix = jnp.concatenate([m1_ref[...], m2_ref[...]], axis=-1)
    acc = jnp.dot(mix, wb_ref[...], preferred_element_type=jnp.float32)
    h = h_ref[...] + acc
    y = h * lax.rsqrt(jnp.mean(h * h, axis=-1, keepdims=True) + EPS) * g_ref[...]
    if final:
        out_refs[0][...] = y
    else:
        out_refs[0][...] = h
        out_refs[1][...] = y.astype(out_refs[1].dtype)


def _out_proj(mix1, mix2, w_stack, layer, h, g_next, final, tm=OUT_PROJ_ROWS):
    m, d = h.shape
    half = mix1.shape[1]
    row = lambda i: (i, 0)
    if final:
        out_shape = jax.ShapeDtypeStruct((m, d), jnp.float32)
        out_specs = pl.BlockSpec((tm, d), row)
    else:
        out_shape = (jax.ShapeDtypeStruct((m, d), jnp.float32), jax.ShapeDtypeStruct((m, d), jnp.bfloat16))
        out_specs = (pl.BlockSpec((tm, d), row), pl.BlockSpec((tm, d), row))
    return pl.pallas_call(
        functools.partial(_out_proj_kernel, final=final),
        grid=(m // tm,),
        in_specs=[pl.BlockSpec((tm, half), row), pl.BlockSpec((tm, half), row),
                  pl.BlockSpec((None, 2 * half, d), lambda i: (layer, 0, 0), pipeline_mode=pl.Buffered(1)),
                  pl.BlockSpec((tm, d), row), pl.BlockSpec((1, d), lambda i: (0, 0))],
        out_specs=out_specs,
        out_shape=out_shape,
        scratch_shapes=[pltpu.VMEM((2 * half, d), jnp.bfloat16)],
        compiler_params=_params("arbitrary"),
        name="out_proj_final" if final else "out_proj",
    )(mix1, mix2, w_stack, h, g_next.reshape(1, d))


A_REGROUP = 4


def _dilated_items(slope, q_ref, k_ref, v_ref, xf, x4, qp, kp, vp, o_s, lse_s):
    assert tuple(d for _, d in A_PATTERNS) == (1, A_REGROUP, A_REGROUP ** 2)
    s_len = q_ref.shape[0]
    n4 = s_len // A_REGROUP
    n16 = n4 // A_REGROUP

    for a, (src, dst) in enumerate(((q_ref, qp), (k_ref, kp), (v_ref, vp))):
        xf[a] = src[...].astype(jnp.float32)
        for c4 in range(A_REGROUP):
            x = xf[a, pl.ds(c4, n4, stride=A_REGROUP), :]
            x4[a, c4 * n4:(c4 + 1) * n4, :] = x
            dst[0, c4 * n4:(c4 + 1) * n4, :] = x.astype(dst.dtype)
        for c4 in range(A_REGROUP):
            for c2 in range(A_REGROUP):
                c16 = A_REGROUP * c2 + c4
                x = x4[a, pl.ds(c4 * n4 + c2, n16, stride=A_REGROUP), :]
                dst[1, c16 * n16:(c16 + 1) * n16, :] = x.astype(dst.dtype)
    srcs = [(q_ref, k_ref, v_ref), (qp.at[0], kp.at[0], vp.at[0]), (qp.at[1], kp.at[1], vp.at[1])]

    tiles = []
    bias = {}
    for p_idx, (_, dil) in enumerate(A_PATTERNS):
        n = s_len // dil
        kw = min(2 * QBLK, n)
        rel0 = (lax.broadcasted_iota(jnp.int32, (QBLK, kw), 1)
                - lax.broadcasted_iota(jnp.int32, (QBLK, kw), 0))
        for t in range(s_len // QBLK):
            c, i0 = divmod(t * QBLK, n)
            w0 = min(max(i0 - A_SIDE, 0), n - kw)
            if (p_idx, w0 - i0) not in bias:
                dist = jnp.abs(rel0 + (w0 - i0)).astype(jnp.float32)
                bias[(p_idx, w0 - i0)] = (slope * (-dil * LOG2E)) * jnp.where(dist <= float(A_SIDE), dist, FAR)
            if p_idx < 2:
                out_rows = pl.ds(c * n + i0, QBLK)
            else:
                c2, c4 = divmod(c, A_REGROUP)
                out_rows = pl.ds(c4 * n4 + c2 + A_REGROUP * i0, QBLK, stride=A_REGROUP)
            tiles.append((p_idx, slice(c * n + i0, c * n + i0 + QBLK), slice(c * n + w0, c * n + w0 + kw),
                          w0 - i0, out_rows))

    def item(p_idx, q_rows, k_rows, off, out_rows):
        q_src, k_src, v_src = srcs[p_idx]

        def scores():
            return _dot_nt(q_src[q_rows, :], k_src[k_rows, :])

        def finish(s):
            v_win = v_src[k_rows, :]
            logits = bias[(p_idx, off)] + s
            m = jnp.max(logits, axis=-1, keepdims=True)
            p = jnp.exp2(logits - m)
            l = jnp.sum(p, axis=-1, keepdims=True)
            o_s[p_idx, out_rows, :] = jnp.dot(p.astype(v_win.dtype), v_win, preferred_element_type=jnp.float32) / l
            lse_s[p_idx, out_rows, :] = jnp.broadcast_to(m + jnp.log2(l), (QBLK, HEAD_DIM))

        return scores, finish

    return [item(*t) for t in tiles]


def _dilated_merge(g_ref, o_ref, o_s, lse_s, o_nat):
    n4 = o_ref.shape[0] // A_REGROUP
    for c4 in range(A_REGROUP):
        blk = slice(c4 * n4, (c4 + 1) * n4)
        nat = pl.ds(c4, n4, stride=A_REGROUP)
        lses = (lse_s[0, nat, :], lse_s[1, blk, :], lse_s[2, blk, :])
        outs = (o_s[0, nat, :], o_s[1, blk, :], o_s[2, blk, :])
        top = jnp.maximum(jnp.maximum(lses[0], lses[1]), lses[2])
        ws = [jnp.exp2(ls - top) for ls in lses]
        num = ws[0] * outs[0] + ws[1] * outs[1] + ws[2] * outs[2]
        o_nat[nat, :] = num / (ws[0] + ws[1] + ws[2])
    o_ref[...] = (o_nat[...] * _silu(g_ref[...].astype(jnp.float32))).astype(o_ref.dtype)


NA_VARIANTS = 8


def _na_variant_row(v, rows):
    return v if v <= 4 else rows - NA_VARIANTS + v


def _na_row_start(r, rows):
    kh = min(NA_ROWS, rows)
    return min(max(r - kh // 2, 0), rows - kh)


def _na_bias_fill(rpb_ref, h, o_ref, pair_ref, rows):
    kh = min(NA_ROWS, rows)
    n_drow = 2 * NA_ROWS_MAX - 1
    n_dcol = 2 * NA_COLS - 1
    shape = (GRID_W, 2 * GRID_W)
    lane = lax.broadcasted_iota(jnp.int32, shape, 1)
    c = lax.broadcasted_iota(jnp.int32, shape, 0)
    kc = lane % GRID_W
    second = lane >= GRID_W
    cs = jnp.clip(c - NA_COLS // 2, 0, GRID_W - NA_COLS)
    valid = (kc >= cs) & (kc < cs + NA_COLS)
    dcol = kc - c + (NA_COLS - 1)
    for d in range(n_drow - 1):
        acc = jnp.zeros(shape, jnp.float32)
        for j in range(n_dcol):
            lo = rpb_ref[(h * n_drow + d) * n_dcol + j]
            hi = rpb_ref[(h * n_drow + d + 1) * n_dcol + j]
            acc = jnp.where(dcol == j, jnp.where(second, hi, lo), acc)
        pair_ref[d] = jnp.where(valid, acc * LOG2E, MASKED)
    for v in range(NA_VARIANTS):
        r = _na_variant_row(v, rows)
        d0 = _na_row_start(r, rows) - r + (NA_ROWS_MAX - 1)
        for i in range(0, kh, 2):
            o_ref[v, :, i * GRID_W:(i + 2) * GRID_W] = pair_ref[d0 + i]


def _na_items(q_ref, k_ref, v_ref, g_ref, bias_ref, o_ref):
    s_len = q_ref.shape[0]
    rows = s_len // GRID_W
    win = min(NA_ROWS, rows) * GRID_W

    def item(r):
        k_rows = slice(_na_row_start(r, rows) * GRID_W, _na_row_start(r, rows) * GRID_W + win)
        variant = r if r < 4 else (r - (rows - NA_VARIANTS) if r > rows - 4 else 4)
        sl = slice(r * GRID_W, (r + 1) * GRID_W)

        def scores():
            return _dot_nt(q_ref[sl, :], k_ref[k_rows, :])

        def finish(s):
            v_win = v_ref[k_rows, :]
            logits = bias_ref[variant] + s
            m = jnp.max(logits, axis=-1, keepdims=True)
            p = jnp.exp2(logits - m)
            den = jnp.sum(p, axis=-1, keepdims=True)
            o = jnp.dot(p.astype(v_win.dtype), v_win, preferred_element_type=jnp.float32) / den
            o_ref[sl, :] = (o * _silu(g_ref[sl, :].astype(jnp.float32))).astype(o_ref.dtype)

        return scores, finish

    return [item(r) for r in range(rows)]


def _even_mixers_kernel(slopes_ref, rpb_ref, qa_ref, ka_ref, va_ref, ga_ref, qb_ref, kb_ref, vb_ref, gb_ref,
                        oa_ref, ob_ref, xf, x4, qp, kp, vp, o_s, lse_s, o_nat, bias_ref, pair_ref):
    @pl.when(pl.program_id(1) == 0)
    def _():
        _na_bias_fill(rpb_ref, pl.program_id(0), bias_ref, pair_ref, qb_ref.shape[0] // GRID_W)

    a_items = _dilated_items(slopes_ref[pl.program_id(0)], qa_ref, ka_ref, va_ref, xf, x4, qp, kp, vp, o_s, lse_s)
    b_items = _na_items(qb_ref, kb_ref, vb_ref, gb_ref, bias_ref, ob_ref)
    _run_pipelined(_interleave(a_items, b_items), EVEN_AHEAD)
    _dilated_merge(ga_ref, oa_ref, o_s, lse_s, o_nat)


def _even_mixers(proj, slopes, rpb, offs):
    b, s, _ = proj.shape
    kh = min(NA_ROWS, s // GRID_W)
    smem = pl.BlockSpec(memory_space=pltpu.SMEM)
    col = lambda off: pl.BlockSpec((None, s, HEAD_DIM), lambda hi, bi: (bi, 0, off + hi))
    out_spec = pl.BlockSpec((None, s, HEAD_DIM), lambda hi, bi: (bi, 0, hi))
    out_shape = jax.ShapeDtypeStruct((b, s, MIX_W), jnp.bfloat16)
    n_pat = len(A_PATTERNS)
    f32_rows = lambda n: pltpu.VMEM((n, s, HEAD_DIM), jnp.float32)
    return pl.pallas_call(
        _even_mixers_kernel,
        grid=(N_HEADS, b),
        in_specs=[smem, smem] + [col(off) for off in offs],
        out_specs=(out_spec, out_spec),
        out_shape=(out_shape, out_shape),
        scratch_shapes=[f32_rows(3), f32_rows(3)]
        + [pltpu.VMEM((n_pat - 1, s, HEAD_DIM), jnp.bfloat16)] * 3
        + [f32_rows(n_pat), f32_rows(n_pat), pltpu.VMEM((s, HEAD_DIM), jnp.float32)]
        + [pltpu.VMEM((NA_VARIANTS, GRID_W, kh * GRID_W), jnp.float32),
           pltpu.VMEM((2 * NA_ROWS_MAX - 2, GRID_W, 2 * GRID_W), jnp.float32)],
        compiler_params=_params("parallel", "arbitrary"),
        name="even_mixers",
    )(slopes, rpb.reshape(-1), *([proj] * 8))


def _window_kernel(slopes_ref, sinks_ref, q_ref, k_ref, v_ref, g_ref, o_ref, v1, kt):
    s_len = q_ref.shape[0]
    v1[:, :HEAD_DIM] = v_ref[...]
    v1[:, HEAD_DIM:] = jnp.ones((s_len, HEAD_DIM), v1.dtype)
    kt[...] = k_ref[...].astype(jnp.float32).T.astype(kt.dtype)
    hk = pl.program_id(1)
    kw = 3 * QBLK
    rel0 = (lax.broadcasted_iota(jnp.int32, (QBLK, kw), 1) - lax.broadcasted_iota(jnp.int32, (QBLK, kw), 0))
    slopes = [slopes_ref[hk * GROUP + gi] for gi in range(GROUP)]
    sinks = [sinks_ref[hk * GROUP + gi] for gi in range(GROUP)]
    sinks = [sink * LOG2E for sink in sinks]
    bias = {}
    tiles = []
    for n in range(s_len // QBLK):
        q0 = n * QBLK
        w0 = min(max(q0 - QBLK, 0), s_len - kw)
        if (w0 - q0, 0) not in bias:
            dist = jnp.abs(rel0 + (w0 - q0)).astype(jnp.float32)
            dist = jnp.where(dist <= float(C_WINDOW), dist, FAR)
            for gi in range(GROUP):
                bias[(w0 - q0, gi)] = (slopes[gi] * -LOG2E) * dist
        tiles += [(slice(q0, q0 + QBLK), gi, slice(w0, w0 + kw), w0 - q0) for gi in range(GROUP)]

    def item(q_rows, gi, k_rows, off):
        lanes = slice(gi * HEAD_DIM, (gi + 1) * HEAD_DIM)

        def scores():
            return jnp.dot(q_ref[q_rows, lanes], kt[:, k_rows], preferred_element_type=jnp.float32)

        def finish(s):
            v_win = v1[k_rows, :]
            logits = bias[(off, gi)] + s
            m = jnp.maximum(jnp.max(logits, axis=-1, keepdims=True), sinks[gi])
            p = jnp.exp2((logits - m).astype(v_win.dtype))
            acc = jnp.dot(p, v_win, preferred_element_type=jnp.float32)
            o = acc[:, :HEAD_DIM] / (acc[:, HEAD_DIM:] + jnp.exp2(sinks[gi] - m))
            gate = _silu(g_ref[q_rows, lanes].astype(jnp.float32))
            o_ref[q_rows, lanes] = (o * gate).astype(o_ref.dtype)

        return scores, finish

    _run_pipelined([item(*t) for t in tiles], WINDOW_AHEAD)


def _gqa_specs(s, q_off, k_off, v_off, g_off):
    gw = GROUP * HEAD_DIM
    wide = lambda off: pl.BlockSpec((None, s, gw), lambda bi, hi: (bi, 0, off // GROUP + hi))
    col = lambda off: pl.BlockSpec((None, s, HEAD_DIM), lambda bi, hi: (bi, 0, off + hi))
    in_specs = [wide(q_off), col(k_off), col(v_off), wide(g_off)]
    out_spec = pl.BlockSpec((None, s, gw), lambda bi, hi: (bi, 0, hi))
    return in_specs, out_spec


def _windowed_sink_attention(proj, slopes, sinks, q_off, k_off, v_off, g_off):
    b, s, _ = proj.shape
    in_specs, out_spec = _gqa_specs(s, q_off, k_off, v_off, g_off)
    smem = pl.BlockSpec(memory_space=pltpu.SMEM)
    return pl.pallas_call(
        _window_kernel,
        grid=(b, KV_HEADS),
        in_specs=[smem, smem] + in_specs,
        out_specs=out_spec,
        out_shape=jax.ShapeDtypeStruct((b, s, MIX_W), jnp.bfloat16),
        scratch_shapes=[pltpu.VMEM((s, 2 * HEAD_DIM), jnp.bfloat16),
                        pltpu.VMEM((HEAD_DIM, s), jnp.bfloat16)],
        compiler_params=_params("parallel", "parallel"),
        name="window_sink_attn",
    )(slopes, sinks, proj, proj, proj, proj)


def _rope_tables(s):
    t = np.arange(s)
    quarter = HEAD_DIM // 4
    inv = jnp.asarray(ROPE_THETA, jnp.float32) ** (-jnp.arange(quarter, dtype=jnp.float32) / quarter)
    ang_r = jnp.asarray(t // GRID_W, jnp.float32)[:, None] * inv[None, :]
    ang_c = jnp.asarray(t % GRID_W, jnp.float32)[:, None] * inv[None, :]
    cos = jnp.concatenate([jnp.cos(ang_r), jnp.cos(ang_r), jnp.cos(ang_c), jnp.cos(ang_c)], axis=-1)
    sin = jnp.concatenate([-jnp.sin(ang_r), jnp.sin(ang_r), -jnp.sin(ang_c), jnp.sin(ang_c)], axis=-1)
    return cos, sin


def _norm_rope(x, gain, cos, sin):
    y = x * lax.rsqrt(jnp.mean(x * x, axis=-1, keepdims=True) + EPS) * gain
    quarter = HEAD_DIM // 4
    lane = lax.broadcasted_iota(jnp.int32, y.shape, 1)
    partner = jnp.where((lane // quarter) % 2 == 0,
                        pltpu.roll(y, HEAD_DIM - quarter, 1),
                        pltpu.roll(y, quarter, 1))
    return y * cos + partner * sin


D_ROWS_PER_DOT = 256


def _axial_kernel(q_ref, k_ref, v_ref, g_ref, cos_ref, sin_ref, gq_ref, gk_ref, o_ref,
                  kp, v1, q4, s_buf, m_buf, p_buf):
    s_len = q_ref.shape[0]
    n_blk = s_len // QBLK
    row_groups = [slice(r, r + D_ROWS_PER_DOT) for r in range(0, GROUP * QBLK, D_ROWS_PER_DOT)]
    kp[...] = _norm_rope(k_ref[...].astype(jnp.float32), gk_ref[...], cos_ref[...], sin_ref[...]).T.astype(kp.dtype)
    v1[:, :HEAD_DIM] = v_ref[...]
    v1[:, HEAD_DIM:] = jnp.ones((s_len, HEAD_DIM), v1.dtype)

    def block_rows(n):
        return pl.ds(pl.multiple_of(n * QBLK, QBLK), QBLK)

    def step(slot, n_prep, n_scores, n_probs, n_out):
        other = 1 - slot
        if n_prep is not None:
            rows = block_rows(n_prep)
            cos = cos_ref[rows, :]
            sin = sin_ref[rows, :]
            for gi in range(GROUP):
                q = _norm_rope(q_ref[rows, gi * HEAD_DIM:(gi + 1) * HEAD_DIM].astype(jnp.float32), gq_ref[...],
                               cos, sin)
                q4[other, gi * QBLK:(gi + 1) * QBLK, :] = (q * (SCALE * LOG2E)).astype(q4.dtype)
        if n_scores is not None:
            for hs in row_groups:
                s = jnp.dot(q4[slot, hs, :], kp[...], preferred_element_type=jnp.float32)
                s_buf[slot, hs, :] = s
                m_buf[slot, hs, :] = jnp.broadcast_to(jnp.max(s, axis=-1, keepdims=True), (D_ROWS_PER_DOT, HEAD_DIM))
        if n_probs is not None:
            for hs in row_groups:
                x = s_buf[other, hs, :] - jnp.tile(m_buf[other, hs, :], (1, s_len // HEAD_DIM))
                p_buf[other, hs, :] = jnp.exp2(x.astype(p_buf.dtype))
        if n_out is not None:
            rows = block_rows(n_out)
            for hs in row_groups:
                acc = jnp.dot(p_buf[slot, hs, :], v1[...], preferred_element_type=jnp.float32)
                o = acc[:, :HEAD_DIM] / acc[:, HEAD_DIM:]
                for gi in range(hs.start // QBLK, hs.stop // QBLK):
                    lanes = slice(gi * HEAD_DIM, (gi + 1) * HEAD_DIM)
                    gate = _silu(g_ref[rows, lanes].astype(jnp.float32))
                    o_ref[rows, lanes] = (o[gi * QBLK - hs.start:(gi + 1) * QBLK - hs.start, :] * gate).astype(o_ref.dtype)

    step(1, 0, None, None, None)
    step(0, 1, 0, None, None)
    step(1, 2, 1, 0, None)

    def pair(i, carry):
        n = 2 * i
        step(0, n + 3, n + 2, n + 1, n)
        step(1, jnp.minimum(n + 4, n_blk - 1), n + 3, n + 2, n + 1)
        return carry

    lax.fori_loop(0, n_blk // 2 - 1, pair, 0)
    step(0, None, None, n_blk - 1, n_blk - 2)
    step(1, None, None, None, n_blk - 1)


def _axial_rope_attention(proj, cos, sin, gq, gk, q_off, k_off, v_off, g_off):
    b, s, _ = proj.shape
    in_specs, out_spec = _gqa_specs(s, q_off, k_off, v_off, g_off)
    table = pl.BlockSpec((s, HEAD_DIM), lambda bi, hi: (0, 0))
    gain = pl.BlockSpec((1, HEAD_DIM), lambda bi, hi: (0, 0))
    stacked = GROUP * QBLK
    return pl.pallas_call(
        _axial_kernel,
        grid=(b, KV_HEADS),
        in_specs=in_specs + [table, table, gain, gain],
        out_specs=out_spec,
        out_shape=jax.ShapeDtypeStruct((b, s, MIX_W), jnp.bfloat16),
        scratch_shapes=[pltpu.VMEM((HEAD_DIM, s), jnp.bfloat16),
                        pltpu.VMEM((s, 2 * HEAD_DIM), jnp.bfloat16),
                        pltpu.VMEM((2, stacked, HEAD_DIM), jnp.bfloat16),
                        pltpu.VMEM((2, stacked, s), jnp.float32),
                        pltpu.VMEM((2, stacked, HEAD_DIM), jnp.float32),
                        pltpu.VMEM((2, stacked, s), jnp.bfloat16)],
        compiler_params=_params("parallel", "parallel"),
        name="axial_rope_attn",
    )(proj, proj, proj, proj, cos, sin, gq.reshape(1, HEAD_DIM), gk.reshape(1, HEAD_DIM))


def _alibi_slopes(n):
    return 2.0 ** (-8.0 * jnp.arange(1, n + 1, dtype=jnp.float32) / n)


def _q_column_scale(widths, q_sections):
    scale = np.ones(sum(widths), np.float32)
    for sec in q_sections:
        start = sum(widths[:sec])
        scale[start:start + widths[sec]] = SCALE * LOG2E
    return jnp.asarray(scale)


def kernel(x, norm_g, final_g, w_in_even, w_out_even, rpb_b, w_in_odd, w_out_odd, sinks_c, qnorm_d, knorm_d):
    b, s, d = x.shape
    bf16 = jnp.bfloat16
    slopes = _alibi_slopes(N_HEADS)
    cos, sin = _rope_tables(s)
    blk = lambda width: width // HEAD_DIM

    h = x.reshape(b * s, d)
    y = _rms_norm_rows(h, norm_g[0], bf16)
    for layer in range(DEPTH):
        i = layer // 2
        final = layer == DEPTH - 1
        g_next = final_g if final else norm_g[layer + 1]
        if layer % 2 == 0:
            widths = [MIX_W] * 8
            offs = [blk(MIX_W) * j for j in range(8)]
            proj = _in_proj(y, w_in_even, i, _q_column_scale(widths, (0, 4))).reshape(b, s, -1)
            mix1, mix2 = _even_mixers(proj, slopes, rpb_b[i], offs)
            w_out = w_out_even
        else:
            widths = [MIX_W, KV_W, KV_W, MIX_W] * 2
            offs = [blk(int(sum(widths[:j]))) for j in range(8)]
            proj = _in_proj(y, w_in_odd, i, _q_column_scale(widths, (0,))).reshape(b, s, -1)
            mix1 = _windowed_sink_attention(proj, slopes, sinks_c[i], *offs[:4])
            mix2 = _axial_rope_attention(proj, cos, sin, qnorm_d[i], knorm_d[i], *offs[4:])
            w_out = w_out_odd
        res = _out_proj(mix1.reshape(b * s, MIX_W), mix2.reshape(b * s, MIX_W), w_out, i, h, g_next, final)
        if final:
            return res.reshape(b, s, d)
        h, y = res
```
